```python
import math
import functools
import jax
import jax.numpy as jnp
from jax import lax
import numpy as np

D_MODEL = 1024
BATCH = 8
SEQ = 2048
DEPTH = 1
DEC_BATCH = 32
DEC_SEQ = 1
PAST_LEN = 16384
PAGE_SIZE = 128

H_A = 4
D_HA = 64
ROT_DIM = D_HA // 4
ROPE_THETA = 500000.0
Q_BLOCK = 128
H_M = 4
DK_M = 64
DV_M = 128
MLSTM_CHUNK = 64
GATE_SOFTCAP = 15.0
N_EXPERTS = 32
TOP_K = 4
D_FF = 1024
SWIGLU_LIMIT = 7.0
SWIGLU_ALPHA = 1.702
LN_EPS = 1e-5
DEEPNORM_ALPHA = (2.0 * DEPTH) ** 0.25
DEEPNORM_BETA = (8.0 * DEPTH) ** -0.25
QA_W = H_A * 2 * D_HA
VA_W = H_A * 2 * D_HA
QM_W = H_M * DK_M
VM_W = H_M * DV_M
N_IN = 2 * QA_W + VA_W + 2 * QM_W + 2 * VM_W + 2 * H_M + 2 * D_MODEL

F32 = jnp.float32

kernel_name = 'hybrid_diffattn_mlstm_moe_step'


def _split_points():
    sizes = (QA_W, QA_W, VA_W, QM_W, QM_W, VM_W, H_M, H_M, VM_W, D_MODEL, D_MODEL)
    pts, acc = [], 0
    for s in sizes[:-1]:
        acc += s
        pts.append(acc)
    return pts


def softcap(x):
    return GATE_SOFTCAP * jnp.tanh(x / GATE_SOFTCAP)


def rope(x, pos):
    inv = ROPE_THETA ** (-jnp.arange(0, ROT_DIM, 2, dtype=F32) / ROT_DIM)
    ang = pos.astype(F32)[:, None] * inv[None, :]
    cos = jnp.cos(ang)[None, :, None, None, :]
    sin = jnp.sin(ang)[None, :, None, None, :]
    xf = x.astype(F32)
    half = ROT_DIM // 2
    x1 = xf[..., :half]
    x2 = xf[..., half:ROT_DIM]
    out = jnp.concatenate([x1 * cos - x2 * sin, x2 * cos + x1 * sin, xf[..., ROT_DIM:]], axis=-1)
    return out.astype(x.dtype)


def rms_norm(x, g):
    xf = x.astype(F32)
    y = xf * lax.rsqrt(jnp.mean(xf * xf, axis=-1, keepdims=True) + LN_EPS)
    return (y * g.astype(F32)).astype(x.dtype)


def head_layer_norm(h, g):
    hf = h.astype(F32)
    mu = jnp.mean(hf, axis=-1, keepdims=True)
    var = jnp.mean(jnp.square(hf - mu), axis=-1, keepdims=True)
    y = (hf - mu) * lax.rsqrt(var + LN_EPS) * g.astype(F32).reshape(H_M, DV_M)
    return y.astype(h.dtype)


def layer_norm(x, g, b):
    xf = x.astype(F32)
    mu = jnp.mean(xf, axis=-1, keepdims=True)
    var = jnp.mean(jnp.square(xf - mu), axis=-1, keepdims=True)
    y = (xf - mu) * lax.rsqrt(var + LN_EPS) * g.astype(F32) + b.astype(F32)
    return y.astype(x.dtype)


def diff_lambda(lq1, lk1, lq2, lk2, lam_init):
    return (jnp.exp(jnp.sum(lq1.astype(F32) * lk1.astype(F32)))
            - jnp.exp(jnp.sum(lq2.astype(F32) * lk2.astype(F32))) + lam_init)


def project(x, w_in, b_igate, b_fgate, pos):
    B, L, _ = x.shape
    z = jnp.einsum('bld,dn->bln', x, w_in)
    qa, ka, va, qm, km, vm, ig, fg, og, ga, gm = jnp.split(z, _split_points(), axis=-1)
    qa = rope(qa.reshape(B, L, H_A, 2, D_HA), pos)
    ka = rope(ka.reshape(B, L, H_A, 2, D_HA), pos)
    va = va.reshape(B, L, H_A, 2 * D_HA)
    qm = qm.reshape(B, L, H_M, DK_M)
    km = km.reshape(B, L, H_M, DK_M) * (DK_M ** -0.5)
    vm = vm.reshape(B, L, H_M, DV_M)
    ig = softcap(ig.astype(F32) + b_igate.astype(F32))
    fg = softcap(fg.astype(F32) + b_fgate.astype(F32))
    return qa, ka, va, qm, km, vm, ig, fg, og, ga, gm


def diff_attn_prompt(q, k, v, lam):
    B, S = q.shape[:2]
    nb = S // Q_BLOCK
    qb = q.reshape(B, nb, Q_BLOCK, H_A, 2, D_HA).swapaxes(0, 1)
    kpos = jnp.arange(S)
    scale = D_HA ** -0.5

    def one_block(args):
        i, qi = args
        qpos = i * Q_BLOCK + jnp.arange(Q_BLOCK)
        mask = kpos[None, :] <= qpos[:, None]
        s = jnp.einsum('bqhcd,bkhcd->bchqk', qi, k).astype(F32) * scale
        p = jax.nn.softmax(jnp.where(mask, s, -jnp.inf), axis=-1)
        p = p[:, 0] - lam * p[:, 1]
        return jnp.einsum('bhqk,bkhe->bqhe', p.astype(v.dtype), v)

    o = lax.map(one_block, (jnp.arange(nb), qb))
    return o.swapaxes(0, 1).reshape(B, S, H_A, 2 * D_HA)


def diff_attn_paged(q, k, v, lam, cache_k, cache_v, page_table):
    B, L = q.shape[:2]
    kp = cache_k[page_table].reshape(B, -1, H_A, 2, D_HA)
    vp = cache_v[page_table].reshape(B, -1, H_A, 2 * D_HA)
    P = kp.shape[1]
    scale = D_HA ** -0.5
    s_past = jnp.einsum('bqhcd,bkhcd->bchqk', q, kp).astype(F32) * scale
    s_new = jnp.einsum('bqhcd,bkhcd->bchqk', q, k).astype(F32) * scale
    causal = jnp.tril(jnp.ones((L, L), dtype=bool))
    s_new = jnp.where(causal, s_new, -jnp.inf)
    p = jax.nn.softmax(jnp.concatenate([s_past, s_new], axis=-1), axis=-1)
    p = (p[:, 0] - lam * p[:, 1]).astype(v.dtype)
    return (jnp.einsum('bhqk,bkhe->bqhe', p[..., :P], vp)
            + jnp.einsum('bhqk,bkhe->bqhe', p[..., P:], v))


def mlstm(q, k, v, ig, fg, c0, n0, m0):
    B, L = q.shape[:2]
    lc = MLSTM_CHUNK if L % MLSTM_CHUNK == 0 else L
    nc = L // lc

    def chunks(t):
        t = t.astype(F32).reshape((B, nc, lc) + t.shape[2:])
        return jnp.moveaxis(jnp.moveaxis(t, 3, 2), 1, 0)

    lf = jax.nn.log_sigmoid(fg)
    xs = (chunks(q), chunks(k), chunks(v), chunks(ig), chunks(lf))
    causal = jnp.tril(jnp.ones((lc, lc), dtype=bool))

    def step(carry, inp):
        C, n, m = carry
        qc, kc, vc, ic, fc = inp
        b = jnp.cumsum(fc, axis=-1)
        D = jnp.where(causal, b[..., :, None] - b[..., None, :] + ic[..., None, :], -jnp.inf)
        m_inter = b + m[..., None]
        m_t = jnp.maximum(m_inter, jnp.max(D, axis=-1))
        w_inter = jnp.exp(m_inter - m_t)
        S = jnp.einsum('bhtd,bhsd->bhts', qc, kc) * jnp.exp(D - m_t[..., None])
        num = (w_inter[..., None] * jnp.einsum('bhtd,bhde->bhte', qc, C)
               + jnp.einsum('bhts,bhse->bhte', S, vc))
        den = w_inter * jnp.einsum('bhtd,bhd->bht', qc, n) + jnp.sum(S, axis=-1)
        h = num / jnp.maximum(jnp.abs(den), jnp.exp(-m_t))[..., None]
        m_new = m_t[..., -1]
        decay = jnp.exp(b[..., -1] + m - m_new)
        wk = jnp.exp(b[..., -1:] - b + ic - m_new[..., None])
        C_new = decay[..., None, None] * C + jnp.einsum('bhs,bhsd,bhse->bhde', wk, kc, vc)
        n_new = decay[..., None] * n + jnp.einsum('bhs,bhsd->bhd', wk, kc)
        return (C_new, n_new, m_new), h

    (C, n, m), hs = lax.scan(step, (c0.astype(F32), n0.astype(F32), m0.astype(F32)), xs)
    h = jnp.moveaxis(jnp.moveaxis(hs, 0, 1), 2, 3).reshape(B, L, H_M, DV_M)
    return h.astype(v.dtype), C, n, m


def moe(x, w_router, b_router, w_gate, b_gate, w_up, b_up, w_down, b_down):
    B, L, D = x.shape
    xt = x.reshape(-1, D)
    logits = jnp.einsum('td,de->te', xt, w_router).astype(F32) + b_router.astype(F32)
    topv, topi = lax.top_k(logits, TOP_K)
    probs = jax.nn.softmax(topv, axis=-1)
    comb = jnp.sum(jax.nn.one_hot(topi, N_EXPERTS, dtype=F32) * probs[..., None], axis=1)
    out = jnp.zeros(xt.shape, F32)
    for e in range(N_EXPERTS):
        g = jnp.minimum(xt @ w_gate[e] + b_gate[e], SWIGLU_LIMIT)
        u = jnp.clip(xt @ w_up[e] + b_up[e], -SWIGLU_LIMIT, SWIGLU_LIMIT)
        hid = g * jax.nn.sigmoid(SWIGLU_ALPHA * g) * (u + 1.0)
        out = out + comb[:, e:e + 1] * (hid @ w_down[e] + b_down[e]).astype(F32)
    return out.astype(x.dtype).reshape(B, L, D)


def decoder_layer(x, pos, attend, c0, n0, m0, lam, lam_init,
                  w_in, b_igate, b_fgate, subln_g, mh_norm_g, w_ba, w_bm, w_o, ln1_g, ln1_b,
                  w_router, b_router, w_gate, b_gate, w_up, b_up, w_down, b_down, ln2_g, ln2_b):
    B, L, _ = x.shape
    qa, ka, va, qm, km, vm, ig, fg, og, ga, gm = project(x, w_in, b_igate, b_fgate, pos)
    attn = attend(qa, ka, va, lam)
    hm, c, n, m = mlstm(qm, km, vm, ig, fg, c0, n0, m0)
    a_in = (rms_norm(attn, subln_g) * (1.0 - lam_init)).reshape(B, L, VA_W)
    a_branch = jnp.einsum('blf,fd->bld', a_in, w_ba)
    m_in = head_layer_norm(hm, mh_norm_g).reshape(B, L, VM_W) * jax.nn.sigmoid(og)
    m_branch = jnp.einsum('blf,fd->bld', m_in, w_bm)
    merged = jax.nn.sigmoid(ga) * a_branch + jax.nn.sigmoid(gm) * m_branch
    mix = jnp.einsum('bld,de->ble', merged, w_o)
    h = layer_norm(DEEPNORM_ALPHA * x + mix, ln1_g, ln1_b)
    ff = moe(h, w_router, b_router, w_gate, b_gate, w_up, b_up, w_down, b_down)
    y = layer_norm(DEEPNORM_ALPHA * h + ff, ln2_g, ln2_b)
    return y, ka.reshape(B, L, H_A, 2 * D_HA), va, c, n, m


def setup_inputs(seed: int = 0) -> dict:
    key = jax.random.key(seed)
    ks = iter(jax.random.split(key, 40))

    def nrm(shape, scale):
        return jax.random.normal(next(ks), shape, F32) * scale

    n_pages = PAST_LEN // PAGE_SIZE
    n_used = DEC_BATCH * n_pages
    n_pool = n_used + (n_used + 3) // 4
    perm = jax.random.permutation(next(ks), n_pool)
    page_table = perm[:n_used].reshape(DEC_BATCH, n_pages).astype(jnp.int32)

    x_prompt = nrm((BATCH, SEQ, D_MODEL), 1.0)
    x_sample = nrm((DEC_BATCH, DEC_SEQ, D_MODEL), 1.0)
    cache_k = nrm((DEPTH, n_pool, PAGE_SIZE, H_A, 2 * D_HA), 1.0)
    cache_v = nrm((DEPTH, n_pool, PAGE_SIZE, H_A, 2 * D_HA), 1.0)
    state_c = nrm((DEPTH, DEC_BATCH, H_M, DK_M, DV_M), 1.0)
    state_n = nrm((DEPTH, DEC_BATCH, H_M, DK_M), 1.0)
    state_m = nrm((DEPTH, DEC_BATCH, H_M), 0.5)

    w_in = nrm((DEPTH, D_MODEL, N_IN), D_MODEL ** -0.5)
    b_igate = nrm((DEPTH, H_M), 0.1)
    b_fgate = 3.0 + 3.0 * jax.random.uniform(next(ks), (DEPTH, H_M), F32)
    lambda_q1 = nrm((DEPTH, D_HA), 0.1)
    lambda_k1 = nrm((DEPTH, D_HA), 0.1)
    lambda_q2 = nrm((DEPTH, D_HA), 0.1)
    lambda_k2 = nrm((DEPTH, D_HA), 0.1)
    subln_g = 1.0 + nrm((DEPTH, 2 * D_HA), 0.01)
    mh_norm_g = 1.0 + nrm((DEPTH, VM_W), 0.01)
    w_ba = nrm((DEPTH, VA_W, D_MODEL), VA_W ** -0.5 * DEEPNORM_BETA)
    w_bm = nrm((DEPTH, VM_W, D_MODEL), VM_W ** -0.5 * DEEPNORM_BETA)
    w_o = nrm((DEPTH, D_MODEL, D_MODEL), D_MODEL ** -0.5 * DEEPNORM_BETA)
    ln1_g = 1.0 + nrm((DEPTH, D_MODEL), 0.01)
    ln1_b = nrm((DEPTH, D_MODEL), 0.01)
    w_router = nrm((DEPTH, D_MODEL, N_EXPERTS), D_MODEL ** -0.5)
    b_router = nrm((DEPTH, N_EXPERTS), 0.01)
    w_gate = nrm((DEPTH, N_EXPERTS, D_MODEL, D_FF), D_MODEL ** -0.5)
    b_gate = nrm((DEPTH, N_EXPERTS, D_FF), 0.01)
    w_up = nrm((DEPTH, N_EXPERTS, D_MODEL, D_FF), D_MODEL ** -0.5)
    b_up = nrm((DEPTH, N_EXPERTS, D_FF), 0.01)
    w_down = nrm((DEPTH, N_EXPERTS, D_FF, D_MODEL), D_FF ** -0.5 * DEEPNORM_BETA)
    b_down = nrm((DEPTH, N_EXPERTS, D_MODEL), 0.01)
    ln2_g = 1.0 + nrm((DEPTH, D_MODEL), 0.01)
    ln2_b = nrm((DEPTH, D_MODEL), 0.01)
    return {
        'x_prompt': x_prompt, 'x_sample': x_sample,
        'cache_k': cache_k, 'cache_v': cache_v,
        'state_c': state_c, 'state_n': state_n, 'state_m': state_m,
        'page_table': page_table,
        'w_in': w_in, 'b_igate': b_igate, 'b_fgate': b_fgate,
        'lambda_q1': lambda_q1, 'lambda_k1': lambda_k1, 'lambda_q2': lambda_q2, 'lambda_k2': lambda_k2,
        'subln_g': subln_g, 'mh_norm_g': mh_norm_g,
        'w_ba': w_ba, 'w_bm': w_bm, 'w_o': w_o, 'ln1_g': ln1_g, 'ln1_b': ln1_b,
        'w_router': w_router, 'b_router': b_router,
        'w_gate': w_gate, 'b_gate': b_gate, 'w_up': w_up, 'b_up': b_up,
        'w_down': w_down, 'b_down': b_down, 'ln2_g': ln2_g, 'ln2_b': ln2_b,
    }


def reference(x_prompt, x_sample, cache_k, cache_v, state_c, state_n, state_m, page_table,
              w_in, b_igate, b_fgate, lambda_q1, lambda_k1, lambda_q2, lambda_k2,
              subln_g, mh_norm_g, w_ba, w_bm, w_o, ln1_g, ln1_b,
              w_router, b_router, w_gate, b_gate, w_up, b_up, w_down, b_down, ln2_g, ln2_b):
    Bp, Lp = x_prompt.shape[:2]
    Ls = x_sample.shape[1]
    past_len = page_table.shape[1] * PAGE_SIZE
    pos_p = jnp.arange(Lp, dtype=jnp.int32)
    pos_s = past_len + jnp.arange(Ls, dtype=jnp.int32)
    y_p, y_s = x_prompt, x_sample
    kp_l, vp_l, cp_l, np_l, mp_l = [], [], [], [], []
    ks_l, vs_l, cs_l, ns_l, ms_l = [], [], [], [], []
    for l in range(DEPTH):
        lam_init = 0.8 - 0.6 * math.exp(-0.3 * l)
        lam = diff_lambda(lambda_q1[l], lambda_k1[l], lambda_q2[l], lambda_k2[l], lam_init)
        lw = (w_in[l], b_igate[l], b_fgate[l], subln_g[l], mh_norm_g[l], w_ba[l], w_bm[l], w_o[l],
              ln1_g[l], ln1_b[l], w_router[l], b_router[l], w_gate[l], b_gate[l], w_up[l], b_up[l],
              w_down[l], b_down[l], ln2_g[l], ln2_b[l])
        c0 = jnp.zeros((Bp, H_M, DK_M, DV_M), F32)
        n0 = jnp.zeros((Bp, H_M, DK_M), F32)
        m0 = jnp.zeros((Bp, H_M), F32)
        y_p, k_r, v_r, c_r, n_r, m_r = decoder_layer(
            y_p, pos_p, diff_attn_prompt, c0, n0, m0, lam, lam_init, *lw)
        kp_l.append(k_r); vp_l.append(v_r); cp_l.append(c_r); np_l.append(n_r); mp_l.append(m_r)
        attend_s = functools.partial(diff_attn_paged, cache_k=cache_k[l], cache_v=cache_v[l],
                                     page_table=page_table)
        y_s, k_r, v_r, c_r, n_r, m_r = decoder_layer(
            y_s, pos_s, attend_s, state_c[l], state_n[l], state_m[l], lam, lam_init, *lw)
        ks_l.append(k_r); vs_l.append(v_r); cs_l.append(c_r); ns_l.append(n_r); ms_l.append(m_r)
    dp, ds = x_prompt.dtype, x_sample.dtype
    y_prompt = y_p
    y_sample = y_s
    k_prompt = jnp.stack(kp_l)
    v_prompt = jnp.stack(vp_l)
    c_prompt = jnp.stack(cp_l).astype(dp)
    n_prompt = jnp.stack(np_l).astype(dp)
    m_prompt = jnp.stack(mp_l).astype(dp)
    k_sample = jnp.stack(ks_l)
    v_sample = jnp.stack(vs_l)
    c_sample = jnp.stack(cs_l).astype(ds)
    n_sample = jnp.stack(ns_l).astype(ds)
    m_sample = jnp.stack(ms_l).astype(ds)
    return (y_prompt, y_sample, k_prompt, v_prompt, c_prompt, n_prompt, m_prompt,
            k_sample, v_sample, c_sample, n_sample, m_sample)
```

```python
import functools
import math

import jax
import jax.numpy as jnp
from jax import lax
from jax.experimental import pallas as pl
from jax.experimental.pallas import tpu as pltpu

F32 = jnp.float32
BF16 = jnp.bfloat16
HIGHEST = lax.Precision.HIGHEST

H_A = 4
D_HA = 64
ROT_DIM = D_HA // 4
ROPE_THETA = 500000.0
H_M = 4
DK_M = 64
DV_M = 128
GATE_SOFTCAP = 15.0
N_EXPERTS = 32
TOP_K = 4
SWIGLU_LIMIT = 7.0
SWIGLU_ALPHA = 1.702
LN_EPS = 1e-5
PAGE_SIZE = 128

QA_W = H_A * 2 * D_HA
QM_W = H_M * DK_M
VM_W = H_M * DV_M
HEAD_W = 2 * D_HA
N_GATE = 2 * H_M
W_A = 3 * QA_W + 2 * QM_W + VM_W

LANES = 128
VMEM_LIMIT = 56 * 1024 * 1024

NT_DIMS = (((1,), (1,)), ((), ()))
TN_DIMS = (((0,), (0,)), ((), ()))


def _params(*sem):
    return pltpu.CompilerParams(dimension_semantics=sem, vmem_limit_bytes=VMEM_LIMIT)


def _softcap(x):
    return GATE_SOFTCAP * jnp.tanh(x / GATE_SOFTCAP)


def _log_sigmoid(x):
    return jnp.minimum(x, 0.0) - jnp.log1p(jnp.exp(-jnp.abs(x)))


def _sigmoid(x):
    return 1.0 / (1.0 + jnp.exp(-x))


def _layer_norm(x, g, b):
    mu = jnp.mean(x, axis=-1, keepdims=True)
    xc = x - mu
    var = jnp.mean(xc * xc, axis=-1, keepdims=True)
    return xc * lax.rsqrt(var + LN_EPS) * g + b


def _rope(t, ra, rb, rc):
    return t * ra + pltpu.roll(t, 8, 1) * rb + pltpu.roll(t, HEAD_W - 8, 1) * rc


def _rope_tables(pos):
    inv = ROPE_THETA ** (-jnp.arange(0, ROT_DIM, 2, dtype=F32) / ROT_DIM)
    ang = pos.astype(F32)[:, None] * inv[None, :]
    cos, sin = jnp.cos(ang), jnp.sin(ang)
    n = pos.shape[0]
    half = ROT_DIM // 2
    rest = D_HA - ROT_DIM
    a = jnp.concatenate([cos, cos, jnp.ones((n, rest), F32)], axis=1)
    b = jnp.concatenate([jnp.zeros((n, half), F32), sin, jnp.zeros((n, rest), F32)], axis=1)
    c = jnp.concatenate([-sin, jnp.zeros((n, half + rest), F32)], axis=1)
    return tuple(jnp.tile(t, (1, 2)) for t in (a, b, c))


def _diff_lambda(lam_ref, lam_init):
    lp = lam_ref[...]
    s1 = jnp.sum(lp[0:1] * lp[1:2], axis=1, keepdims=True)
    s2 = jnp.sum(lp[2:3] * lp[3:4], axis=1, keepdims=True)
    return jnp.exp(s1) - jnp.exp(s2) + lam_init


def _proj_prompt_kernel(x_ref, wa_ref, wgc_ref, wgr_ref, wr_ref, ra_ref, rb_ref, rc_ref,
                        q_ref, kf_ref, kb_ref, vf_ref, vb_ref, qm_ref, km_ref, vm_ref,
                        gcol_ref, grow_ref, og_ref, ga_ref, gm_ref):
    x = x_ref[...].astype(BF16)
    za = jnp.dot(x, wa_ref[...], preferred_element_type=F32)
    ra, rb, rc = ra_ref[...], rb_ref[...], rc_ref[...]
    for h in range(H_A):
        sl = slice(h * HEAD_W, (h + 1) * HEAD_W)
        qh = _rope(za[:, sl], ra, rb, rc)
        q_ref[:, sl] = (qh * (D_HA ** -0.5)).astype(BF16)
        kh = _rope(za[:, QA_W + h * HEAD_W:QA_W + (h + 1) * HEAD_W], ra, rb, rc)
        kf_ref[:, sl] = kh
        kb_ref[:, sl] = kh.astype(BF16)
    v = za[:, 2 * QA_W:3 * QA_W]
    vf_ref[...] = v
    vb_ref[...] = v.astype(BF16)
    o = 3 * QA_W
    qm_ref[...] = za[:, o:o + QM_W].astype(BF16)
    km_ref[...] = (za[:, o + QM_W:o + 2 * QM_W] * (DK_M ** -0.5)).astype(BF16)
    vm_ref[...] = za[:, o + 2 * QM_W:].astype(BF16)
    gcol_ref[...] = jnp.dot(x, wgc_ref[...], preferred_element_type=F32)
    grow_ref[...] = lax.dot_general(wgr_ref[...], x, NT_DIMS, preferred_element_type=F32)
    zr = jnp.dot(x, wr_ref[...], preferred_element_type=F32)
    d = ga_ref.shape[1]
    og_ref[...] = zr[:, :VM_W].astype(BF16)
    ga_ref[...] = zr[:, VM_W:VM_W + d].astype(BF16)
    gm_ref[...] = zr[:, VM_W + d:].astype(BF16)


def _proj_prompt(x2, wa, wgc, wgr, wr, tabs, seq, tm):
    t, d = x2.shape
    nrep = seq // tm
    row = lambda w: pl.BlockSpec((tm, w), lambda i: (i, 0))
    full = lambda a: pl.BlockSpec(a.shape, lambda i: (0,) * a.ndim)
    tab = pl.BlockSpec((tm, HEAD_W), lambda i: (i % nrep, 0))
    sds = lambda w, dt: jax.ShapeDtypeStruct((t, w), dt)
    return pl.pallas_call(
        _proj_prompt_kernel,
        grid=(t // tm,),
        in_specs=[row(d), full(wa), full(wgc), full(wgr), full(wr), tab, tab, tab],
        out_specs=[row(QA_W), row(QA_W), row(QA_W), row(QA_W), row(QA_W),
                   row(QM_W), row(QM_W), row(VM_W), row(LANES),
                   pl.BlockSpec((N_GATE, tm), lambda i: (0, i)),
                   row(VM_W), row(d), row(d)],
        out_shape=[sds(QA_W, BF16), sds(QA_W, F32), sds(QA_W, BF16), sds(QA_W, F32), sds(QA_W, BF16),
                   sds(QM_W, BF16), sds(QM_W, BF16), sds(VM_W, BF16), sds(LANES, F32),
                   jax.ShapeDtypeStruct((N_GATE, t), F32),
                   sds(VM_W, BF16), sds(d, BF16), sds(d, BF16)],
        compiler_params=_params("parallel"),
        name="proj_prompt",
    )(x2, wa, wgc, wgr, wr, *tabs)


def _attn_prompt_kernel(q_ref, k_ref, v_ref, lam_ref, g_ref, o_ref, *, tq, lam_init):
    qi = pl.program_id(2)
    q = q_ref[...]
    lane = lax.broadcasted_iota(jnp.int32, (1, HEAD_W), 1)
    zero = jnp.zeros_like(q)
    qq = jnp.concatenate([jnp.where(lane < D_HA, q, zero), jnp.where(lane >= D_HA, q, zero)], axis=0)
    r = lax.broadcasted_iota(jnp.int32, (tq, tq), 0)
    c = lax.broadcasted_iota(jnp.int32, (tq, tq), 1)
    causal = jnp.concatenate([c <= r, c <= r], axis=0)

    def step(j, carry, masked):
        m, l, acc = carry
        off = pl.multiple_of(j * tq, tq)
        k = k_ref[pl.ds(off, tq), :]
        v = v_ref[pl.ds(off, tq), :]
        s = lax.dot_general(qq, k, NT_DIMS, preferred_element_type=F32)
        if masked:
            s = jnp.where(causal, s, -jnp.inf)
        m_new = jnp.maximum(m, jnp.max(s, axis=1, keepdims=True))
        alpha = jnp.exp(m - m_new)
        p = jnp.exp(s - m_new)
        l = alpha * l + jnp.sum(p, axis=1, keepdims=True)
        acc = alpha * acc + jnp.dot(p.astype(BF16), v, preferred_element_type=F32)
        return m_new, l, acc

    init = (jnp.full((2 * tq, 1), -jnp.inf, F32), jnp.zeros((2 * tq, 1), F32),
            jnp.zeros((2 * tq, HEAD_W), F32))
    carry = lax.fori_loop(0, qi, lambda j, cr: step(j, cr, False), init)
    m, l, acc = step(qi, carry, True)
    lam = _diff_lambda(lam_ref, lam_init)
    o = acc[:tq] / l[:tq] - lam * (acc[tq:] / l[tq:])
    y = o * lax.rsqrt(jnp.mean(o * o, axis=1, keepdims=True) + LN_EPS) * g_ref[...] * (1.0 - lam_init)
    o_ref[...] = y.astype(o_ref.dtype)


def _attn_prompt(q, k, v, lam_p, subln_g, batch, seq, tq, lam_init):
    t = q.shape[0]
    nq = seq // tq
    return pl.pallas_call(
        functools.partial(_attn_prompt_kernel, tq=tq, lam_init=lam_init),
        grid=(batch, H_A, nq),
        in_specs=[pl.BlockSpec((tq, HEAD_W), lambda b, h, i: (b * nq + i, h)),
                  pl.BlockSpec((seq, HEAD_W), lambda b, h, i: (b, h)),
                  pl.BlockSpec((seq, HEAD_W), lambda b, h, i: (b, h)),
                  pl.BlockSpec(lam_p.shape, lambda b, h, i: (0, 0)),
                  pl.BlockSpec((1, HEAD_W), lambda b, h, i: (0, 0))],
        out_specs=pl.BlockSpec((tq, HEAD_W), lambda b, h, i: (b * nq + i, h)),
        out_shape=jax.ShapeDtypeStruct((t, QA_W), BF16),
        compiler_params=_params("parallel", "parallel", "parallel"),
        name="attn_prompt",
    )(q, k, v, lam_p, subln_g)


def _mlstm_prompt_kernel(q_ref, k_ref, v_ref, gcol_ref, grow_ref, bcol_ref, brow_ref, og_ref, g_ref,
                         min_ref, cn_ref, m_ref, *, lc):
    ci = pl.program_id(1)

    @pl.when(ci == 0)
    def _():
        cn_ref[...] = jnp.zeros_like(cn_ref)
        m_ref[...] = jnp.zeros_like(m_ref)

    gc = _softcap(gcol_ref[...] + bcol_ref[...])
    lfc = _log_sigmoid(gc)
    gr = _softcap(grow_ref[...] + brow_ref[...])
    lfr = _log_sigmoid(gr)
    ri = lax.broadcasted_iota(jnp.int32, (lc, lc), 0)
    cj = lax.broadcasted_iota(jnp.int32, (lc, lc), 1)
    tril = cj <= ri
    b_col = jnp.dot(tril.astype(F32), lfc, precision=HIGHEST, preferred_element_type=F32)
    b_row = jnp.dot(lfr, (ri <= cj).astype(F32), precision=HIGHEST, preferred_element_type=F32)

    cn = cn_ref[0]
    cn_bf = cn.astype(BF16)
    q = q_ref[...]
    k = k_ref[...]
    v = v_ref[...]
    m_all = m_ref[0]
    lane_q = lax.broadcasted_iota(jnp.int32, (1, QM_W), 1) // DK_M
    row_q = lax.broadcasted_iota(jnp.int32, (QM_W, 1), 0) // DK_M
    lane_v = lax.broadcasted_iota(jnp.int32, (1, LANES), 1)
    ones_blk = jnp.broadcast_to(jnp.where(lane_v == 0, 1.0, 0.0).astype(BF16), (lc, LANES))
    new_cn = jnp.zeros_like(cn)
    new_m = m_all
    for h in range(H_M):
        sl = slice(h * DV_M, (h + 1) * DV_M)
        qh = jnp.where(lane_q == h, q, jnp.zeros_like(q))
        kh = jnp.where(lane_q == h, k, jnp.zeros_like(k))
        b_c = b_col[:, H_M + h:H_M + h + 1]
        ig_c = gc[:, h:h + 1]
        b_r = b_row[H_M + h:H_M + h + 1, :]
        ig_r = gr[h:h + 1, :]
        m_prev = m_all[:, h:h + 1]
        dmat = jnp.where(tril, b_c - b_r + ig_r, -jnp.inf)
        m_inter = b_c + m_prev
        m_t = jnp.maximum(m_inter, jnp.max(dmat, axis=1, keepdims=True))
        w_inter = jnp.exp(m_inter - m_t)
        s = lax.dot_general(qh, kh, NT_DIMS, preferred_element_type=F32) * jnp.exp(dmat - m_t)
        vext = jnp.concatenate([v[:, sl], ones_blk], axis=1)
        nd = (w_inter * jnp.dot(qh, cn_bf, preferred_element_type=F32)
              + jnp.dot(s.astype(BF16), vext, preferred_element_type=F32))
        num = nd[:, :DV_M]
        den = nd[:, DV_M:DV_M + 1]
        hout = num / jnp.maximum(jnp.abs(den), jnp.exp(-m_t))
        mu = jnp.mean(hout, axis=1, keepdims=True)
        hc = hout - mu
        var = jnp.mean(hc * hc, axis=1, keepdims=True)
        y = hc * lax.rsqrt(var + LN_EPS) * g_ref[:, sl] * _sigmoid(og_ref[:, sl].astype(F32))
        min_ref[:, sl] = y.astype(min_ref.dtype)
        m_new = m_t[lc - 1:lc, :]
        b_last = b_c[lc - 1:lc, :]
        decay = jnp.exp(b_last + m_prev - m_new)
        wk = jnp.exp(b_last - b_c + ig_c - m_new)
        upd = lax.dot_general(kh, (wk * vext.astype(F32)).astype(BF16), TN_DIMS,
                              preferred_element_type=F32)
        new_cn = new_cn + jnp.where(row_q == h, decay * cn, 0.0) + upd
        new_m = jnp.where(lane_v == h, m_new, new_m)
    cn_ref[0] = new_cn
    m_ref[0] = new_m


def _mlstm_prompt(qm, km, vm, gcol, grow, bcol, brow, og, mh_g, batch, seq, lc):
    t = qm.shape[0]
    nc = seq // lc
    row = lambda w: pl.BlockSpec((lc, w), lambda b, c: (b * nc + c, 0))
    const = lambda a: pl.BlockSpec(a.shape, lambda b, c: (0,) * a.ndim)
    cw = 2 * LANES
    return pl.pallas_call(
        functools.partial(_mlstm_prompt_kernel, lc=lc),
        grid=(batch, nc),
        in_specs=[row(QM_W), row(QM_W), row(VM_W), row(LANES),
                  pl.BlockSpec((N_GATE, lc), lambda b, c: (0, b * nc + c)),
                  const(bcol), const(brow), row(VM_W), const(mh_g)],
        out_specs=[row(VM_W),
                   pl.BlockSpec((1, QM_W, cw), lambda b, c: (b, 0, 0)),
                   pl.BlockSpec((1, 1, LANES), lambda b, c: (b, 0, 0))],
        out_shape=[jax.ShapeDtypeStruct((t, VM_W), BF16),
                   jax.ShapeDtypeStruct((batch, QM_W, cw), F32),
                   jax.ShapeDtypeStruct((batch, 1, LANES), F32)],
        compiler_params=_params("parallel", "arbitrary"),
        name="mlstm_prompt",
    )(qm, km, vm, gcol, grow, bcol, brow, og, mh_g)


def _mix_kernel(a_ref, mi_ref, ga_ref, gm_ref, x_ref, wba_ref, wbm_ref, wo_ref, g1_ref, b1_ref,
                wrt_ref, brt_ref, h_ref, hb_ref, comb_ref, mask_ref, pos_ref, cnt_ref, carry_ref,
                *, alpha, precise, tiles_per_group):
    i = pl.program_id(0)

    def mm(a, w_ref):
        if precise:
            return jnp.dot(a.astype(F32), w_ref[...], precision=HIGHEST, preferred_element_type=F32)
        return jnp.dot(a.astype(BF16), w_ref[...], preferred_element_type=F32)

    a_br = mm(a_ref[...], wba_ref)
    m_br = mm(mi_ref[...], wbm_ref)
    merged = _sigmoid(ga_ref[...].astype(F32)) * a_br + _sigmoid(gm_ref[...].astype(F32)) * m_br
    mix = mm(merged, wo_ref)
    h = _layer_norm(alpha * x_ref[...] + mix, g1_ref[...], b1_ref[...])
    h_ref[...] = h
    hb_ref[...] = h.astype(BF16)

    logits = lax.dot_general(wrt_ref[...], h, NT_DIMS, precision=HIGHEST,
                             preferred_element_type=F32) + brt_ref[...]
    ne, tm = logits.shape
    eidx = lax.broadcasted_iota(jnp.int32, (ne, tm), 0)
    work = logits
    sel = jnp.zeros((ne, tm), jnp.bool_)
    top = None
    for _ in range(TOP_K):
        mx = jnp.max(work, axis=0, keepdims=True)
        top = mx if top is None else top
        first = jnp.min(jnp.where(work == mx, eidx, ne), axis=0, keepdims=True)
        pick = eidx == first
        sel = jnp.logical_or(sel, pick)
        work = jnp.where(pick, -jnp.inf, work)
    ex = jnp.where(sel, jnp.exp(logits - top), 0.0)
    comb_ref[...] = ex / jnp.sum(ex, axis=0, keepdims=True)
    maskf = jnp.where(sel, 1.0, 0.0)
    mask_ref[...] = maskf

    @pl.when(i % tiles_per_group == 0)
    def _():
        carry_ref[...] = jnp.zeros_like(carry_ref)

    r = lax.broadcasted_iota(jnp.int32, (tm, tm), 0)
    c = lax.broadcasted_iota(jnp.int32, (tm, tm), 1)
    before = jnp.dot(maskf.astype(BF16), (r < c).astype(BF16), preferred_element_type=F32)
    base = carry_ref[...]
    pos_ref[...] = before + base[:, 0:1]
    total = base + jnp.sum(maskf, axis=1, keepdims=True)
    carry_ref[...] = total
    cnt_ref[0] = total


def _mix(a_in, m_in, ga, gm, x2, wba, wbm, wo, g1, b1, wrt, brt, alpha, precise, tm, group):
    t, d = x2.shape
    ne = wrt.shape[0]
    tpg = group // tm
    ng = t // group
    row = lambda w: pl.BlockSpec((tm, w), lambda i: (i, 0))
    col = pl.BlockSpec((ne, tm), lambda i: (0, i))
    const = lambda a: pl.BlockSpec(a.shape, lambda i: (0,) * a.ndim)
    return pl.pallas_call(
        functools.partial(_mix_kernel, alpha=alpha, precise=precise, tiles_per_group=tpg),
        grid=(t // tm,),
        in_specs=[row(QA_W), row(VM_W), row(d), row(d), row(d), const(wba), const(wbm), const(wo),
                  const(g1), const(b1), const(wrt), const(brt)],
        out_specs=[row(d), row(d), col, col, col,
                   pl.BlockSpec((1, ne, LANES), lambda i: (i // tpg, 0, 0))],
        out_shape=[jax.ShapeDtypeStruct((t, d), F32), jax.ShapeDtypeStruct((t, d), BF16),
                   jax.ShapeDtypeStruct((ne, t), F32), jax.ShapeDtypeStruct((ne, t), F32),
                   jax.ShapeDtypeStruct((ne, t), F32),
                   jax.ShapeDtypeStruct((ng, ne, LANES), F32)],
        scratch_shapes=[pltpu.VMEM((ne, LANES), F32)],
        compiler_params=_params("arbitrary"),
        name="mix_sample" if precise else "mix_prompt",
    )(a_in, m_in, ga, gm, x2, wba, wbm, wo, g1, b1, wrt, brt)


def _moe_kernel(cnt_ref, h_ref, pos_ref, mask_ref, comb_ref, wg_ref, bg_ref, wu_ref, bu_ref,
                wd_ref, bd_ref, o_ref, *, rows):
    g = pl.program_id(0)
    e = pl.program_id(1)
    ne = pl.num_programs(1)

    @pl.when(e == 0)
    def _():
        o_ref[...] = jnp.zeros_like(o_ref)

    n = cnt_ref[g * ne + e]
    pos = pos_ref[pl.ds(e, 1), :]
    msk = mask_ref[pl.ds(e, 1), :] > 0.0
    cw = comb_ref[pl.ds(e, 1), :]
    slot0 = lax.broadcasted_iota(jnp.int32, (rows, 1), 0).astype(F32)

    def tile(s, carry):
        slot = slot0 + (s * rows).astype(F32)
        hit = jnp.logical_and(pos == slot, msk)
        xg = jnp.dot(jnp.where(hit, 1.0, 0.0).astype(BF16), h_ref[...],
                     preferred_element_type=F32).astype(BF16)
        gate = jnp.minimum(jnp.dot(xg, wg_ref[0], preferred_element_type=F32) + bg_ref[0], SWIGLU_LIMIT)
        up = jnp.clip(jnp.dot(xg, wu_ref[0], preferred_element_type=F32) + bu_ref[0],
                      -SWIGLU_LIMIT, SWIGLU_LIMIT)
        hid = gate * _sigmoid(SWIGLU_ALPHA * gate) * (up + 1.0)
        y = jnp.dot(hid.astype(BF16), wd_ref[0], preferred_element_type=F32) + bd_ref[0]
        gw = jnp.where(hit, cw, 0.0).astype(BF16)
        o_ref[...] += lax.dot_general(gw, y.astype(BF16), TN_DIMS, preferred_element_type=F32)
        return carry

    lax.fori_loop(0, (n + rows - 1) // rows, tile, 0)


def _moe(counts, hb, pos, mask, comb, wg, bg, wu, bu, wd, bd, group, rows):
    t, d = hb.shape
    ne, _, f = wg.shape
    ng = t // group
    tok = pl.BlockSpec((ne, group), lambda g, e, c: (0, g))
    wspec = lambda a: pl.BlockSpec((1,) + a.shape[1:], lambda g, e, c: (e, 0, 0))
    grid_spec = pltpu.PrefetchScalarGridSpec(
        num_scalar_prefetch=1,
        grid=(ng, ne),
        in_specs=[pl.BlockSpec((group, d), lambda g, e, c: (g, 0)), tok, tok, tok,
                  wspec(wg), wspec(bg), wspec(wu), wspec(bu), wspec(wd), wspec(bd)],
        out_specs=pl.BlockSpec((group, d), lambda g, e, c: (g, 0)),
    )
    return pl.pallas_call(
        functools.partial(_moe_kernel, rows=rows),
        grid_spec=grid_spec,
        out_shape=jax.ShapeDtypeStruct((t, d), F32),
        compiler_params=_params("parallel", "arbitrary"),
        name="moe",
    )(counts, hb, pos, mask, comb, wg, bg, wu, bu, wd, bd)


def _ln2_kernel(h_ref, ff_ref, g_ref, b_ref, y_ref, *, alpha):
    y_ref[...] = _layer_norm(alpha * h_ref[...] + ff_ref[...], g_ref[...], b_ref[...])


def _ln2(h, ff, g2, b2, alpha, tm, ff_block0):
    t, d = h.shape
    const = lambda a: pl.BlockSpec(a.shape, lambda i: (0,) * a.ndim)
    return pl.pallas_call(
        functools.partial(_ln2_kernel, alpha=alpha),
        grid=(t // tm,),
        in_specs=[pl.BlockSpec((tm, d), lambda i: (i, 0)),
                  pl.BlockSpec((tm, d), lambda i: (ff_block0 + i, 0)), const(g2), const(b2)],
        out_specs=pl.BlockSpec((tm, d), lambda i: (i, 0)),
        out_shape=jax.ShapeDtypeStruct((t, d), F32),
        compiler_params=_params("parallel"),
        name="ln2",
    )(h, ff, g2, b2)


def _proj_sample_kernel(x_ref, w_ref, z_ref):
    z_ref[...] = jnp.dot(x_ref[...], w_ref[...], precision=HIGHEST, preferred_element_type=F32)


def _proj_sample(xs, w, chunk):
    n, d = xs.shape
    width = w.shape[1]
    return pl.pallas_call(
        _proj_sample_kernel,
        grid=(width // chunk,),
        in_specs=[pl.BlockSpec((n, d), lambda j: (0, 0)), pl.BlockSpec((d, chunk), lambda j: (0, j))],
        out_specs=pl.BlockSpec((n, chunk), lambda j: (0, j)),
        out_shape=jax.ShapeDtypeStruct((n, width), F32),
        compiler_params=_params("parallel"),
        name="proj_sample",
    )(xs, w)


def _paged_attn_kernel(pt_ref, z_ref, ra_ref, rb_ref, rc_ref, lam_ref, g_ref, *rest,
                       ppb, lam_init):
    k_refs = rest[:ppb]
    v_refs = rest[ppb:2 * ppb]
    kout_ref, vout_ref, a_ref, q_scr, m_scr, l_scr, acc_scr = rest[2 * ppb:]
    j = pl.program_id(1)
    nj = pl.num_programs(1)
    nrow = 2 * H_A
    rsel = lax.broadcasted_iota(jnp.int32, (nrow, QA_W), 0)
    lsel = lax.broadcasted_iota(jnp.int32, (nrow, QA_W), 1) // D_HA
    ra, rb, rc = ra_ref[...], rb_ref[...], rc_ref[...]

    @pl.when(j == 0)
    def _():
        z = z_ref[0]
        q = jnp.concatenate([_rope(z[:, h * HEAD_W:(h + 1) * HEAD_W], ra, rb, rc)
                             for h in range(H_A)], axis=1) * (D_HA ** -0.5)
        q_scr[...] = jnp.where(rsel == lsel, jnp.broadcast_to(q, (nrow, QA_W)), 0.0)
        m_scr[...] = jnp.full_like(m_scr, -jnp.inf)
        l_scr[...] = jnp.zeros_like(l_scr)
        acc_scr[...] = jnp.zeros_like(acc_scr)

    qbd = q_scr[...]
    s = jnp.concatenate(
        [lax.dot_general(qbd, kr[0], NT_DIMS, precision=HIGHEST, preferred_element_type=F32)
         for kr in k_refs], axis=1)
    m_old = m_scr[...]
    m_new = jnp.maximum(m_old, jnp.max(s, axis=1, keepdims=True))
    alpha = jnp.exp(m_old - m_new)
    p = jnp.exp(s - m_new)
    l_scr[...] = alpha * l_scr[...] + jnp.sum(p, axis=1, keepdims=True)
    pv = None
    for i, vr in enumerate(v_refs):
        t = jnp.dot(p[:, i * PAGE_SIZE:(i + 1) * PAGE_SIZE], vr[0], precision=HIGHEST,
                    preferred_element_type=F32)
        pv = t if pv is None else pv + t
    acc_scr[...] = alpha * acc_scr[...] + pv
    m_scr[...] = m_new

    @pl.when(j == nj - 1)
    def _():
        z = z_ref[0]
        k_new = jnp.concatenate([_rope(z[:, QA_W + h * HEAD_W:QA_W + (h + 1) * HEAD_W], ra, rb, rc)
                                 for h in range(H_A)], axis=1)
        v_new = z[:, 2 * QA_W:3 * QA_W]
        kout_ref[0] = k_new
        vout_ref[0] = v_new
        s_self = jnp.sum(qbd * k_new, axis=1, keepdims=True)
        m_old2 = m_scr[...]
        m_fin = jnp.maximum(m_old2, s_self)
        a2 = jnp.exp(m_old2 - m_fin)
        p_self = jnp.exp(s_self - m_fin)
        l_fin = a2 * l_scr[...] + p_self
        o_all = (a2 * acc_scr[...] + p_self * v_new) / l_fin
        lam = _diff_lambda(lam_ref, lam_init)
        outs = []
        for h in range(H_A):
            sl = slice(h * HEAD_W, (h + 1) * HEAD_W)
            o = o_all[2 * h:2 * h + 1, sl] - lam * o_all[2 * h + 1:2 * h + 2, sl]
            outs.append(o * lax.rsqrt(jnp.mean(o * o, axis=1, keepdims=True) + LN_EPS)
                        * g_ref[...] * (1.0 - lam_init))
        a_ref[0] = jnp.concatenate(outs, axis=1)


def _paged_attn(page_table, z3, tabs, lam_p, subln_g, ck, cv, ppb, lam_init):
    nb, npages = page_table.shape
    width = z3.shape[2]
    nj = npages // ppb
    pt = page_table.reshape(-1)
    const = lambda a: pl.BlockSpec(a.shape, lambda b, j, p: (0,) * a.ndim)
    page = lambda i: pl.BlockSpec((1, PAGE_SIZE, QA_W),
                                  lambda b, j, p: (p[b * npages + j * ppb + i], 0, 0))
    seq3 = lambda w: pl.BlockSpec((1, 1, w), lambda b, j, p: (b, 0, 0))
    grid_spec = pltpu.PrefetchScalarGridSpec(
        num_scalar_prefetch=1,
        grid=(nb, nj),
        in_specs=[seq3(width), const(tabs[0]), const(tabs[1]), const(tabs[2]), const(lam_p),
                  const(subln_g)] + [page(i) for i in range(ppb)] + [page(i) for i in range(ppb)],
        out_specs=[seq3(QA_W), seq3(QA_W), seq3(QA_W)],
        scratch_shapes=[pltpu.VMEM((2 * H_A, QA_W), F32), pltpu.VMEM((2 * H_A, 1), F32),
                        pltpu.VMEM((2 * H_A, 1), F32), pltpu.VMEM((2 * H_A, QA_W), F32)],
    )
    sds = jax.ShapeDtypeStruct((nb, 1, QA_W), F32)
    return pl.pallas_call(
        functools.partial(_paged_attn_kernel, ppb=ppb, lam_init=lam_init),
        grid_spec=grid_spec,
        out_shape=[sds, sds, sds],
        compiler_params=_params("parallel", "arbitrary"),
        name="paged_attn",
    )(pt, z3, *tabs, lam_p, subln_g, *([ck] * ppb), *([cv] * ppb))


def _mlstm_step_kernel(gates_ref, m0_ref, big_ref, bfg_ref, qc_ref, kc_ref, v_ref, og_ref, c0_ref,
                       n0_ref, g_ref, min_ref, c_ref, n_ref, m_ref):
    b = pl.program_id(0)
    outs = []
    for h in range(H_M):
        sl = slice(h * DV_M, (h + 1) * DV_M)
        ig = _softcap(jnp.full((1, LANES), gates_ref[b, h] + big_ref[h], F32))
        lf = _log_sigmoid(_softcap(jnp.full((1, LANES), gates_ref[b, H_M + h] + bfg_ref[h], F32)))
        m0 = jnp.full((1, LANES), m0_ref[b, h], F32)
        m_inter = lf + m0
        m_t = jnp.maximum(m_inter, ig)
        w_inter = jnp.exp(m_inter - m_t)
        w_new = jnp.exp(ig - m_t)
        qc = qc_ref[0, h]
        kc = kc_ref[0, h] * (DK_M ** -0.5)
        vr = v_ref[0][:, sl]
        c0 = c0_ref[0, h]
        n0 = n0_ref[0, h]
        qk = jnp.sum(qc * kc, axis=0, keepdims=True)
        s = qk * w_new
        num = w_inter * jnp.sum(qc * c0, axis=0, keepdims=True) + s * vr
        den = w_inter * jnp.sum(qc * n0, axis=0, keepdims=True) + s
        hout = num / jnp.maximum(jnp.abs(den), jnp.exp(-m_t))
        mu = jnp.mean(hout, axis=1, keepdims=True)
        hc = hout - mu
        var = jnp.mean(hc * hc, axis=1, keepdims=True)
        outs.append(hc * lax.rsqrt(var + LN_EPS) * g_ref[:, sl] * _sigmoid(og_ref[0][:, sl]))
        c_ref[0, h] = w_inter * c0 + (w_new * kc) * vr
        n_ref[0, h] = w_inter[:, 0:1] * n0 + w_new[:, 0:1] * kc
        m_ref[0, h:h + 1, :] = m_t
    min_ref[0] = jnp.concatenate(outs, axis=1)


def _mlstm_step(gates, m0, big, bfg, qc, kc, v3, og3, c0, n0c, mh_g):
    nb = c0.shape[0]
    smem = pl.BlockSpec(memory_space=pltpu.SMEM)
    per = lambda a: pl.BlockSpec((1,) + a.shape[1:], lambda b: (b,) + (0,) * (a.ndim - 1))
    const = lambda a: pl.BlockSpec(a.shape, lambda b: (0,) * a.ndim)
    return pl.pallas_call(
        _mlstm_step_kernel,
        grid=(nb,),
        in_specs=[smem, smem, smem, smem, per(qc), per(kc), per(v3), per(og3), per(c0), per(n0c),
                  const(mh_g)],
        out_specs=[per(v3), per(c0), per(n0c), pl.BlockSpec((1, H_M, LANES), lambda b: (b, 0, 0))],
        out_shape=[jax.ShapeDtypeStruct(v3.shape, F32), jax.ShapeDtypeStruct(c0.shape, F32),
                   jax.ShapeDtypeStruct(n0c.shape, F32),
                   jax.ShapeDtypeStruct((nb, H_M, LANES), F32)],
        compiler_params=_params("parallel"),
        name="mlstm_step",
    )(gates, m0, big, bfg, qc, kc, v3, og3, c0, n0c, mh_g)


def _pick_tile(n, target):
    t = min(n, target)
    while n % t:
        t //= 2
    return t


def _layer(l, depth, x_p, x_s, cache_k, cache_v, c0, n0, m0, page_table, w):
    (w_in, b_ig, b_fg, lq1, lk1, lq2, lk2, subln_g, mh_g, w_ba, w_bm, w_o, ln1_g, ln1_b,
     w_router, b_router, w_gate, b_gate, w_up, b_up, w_down, b_down, ln2_g, ln2_b) = w
    alpha = (2.0 * depth) ** 0.25
    lam_init = 0.8 - 0.6 * math.exp(-0.3 * l)
    bp, seq, d = x_p.shape
    ns = x_s.shape[0]
    t = bp * seq
    ne = w_router.shape[1]
    past = page_table.shape[1] * PAGE_SIZE

    w_a = w_in[:, :W_A]
    w_gt = w_in[:, W_A:W_A + N_GATE]
    w_r = w_in[:, W_A + N_GATE:]
    w_gc = jnp.pad(w_gt, ((0, 0), (0, LANES - N_GATE)))
    lam_p = jnp.stack([lq1, lk1, lq2, lk2])
    sub_g = subln_g.reshape(1, HEAD_W)
    mh_g2 = mh_g.reshape(1, VM_W)
    bcol = jnp.pad(jnp.concatenate([b_ig, b_fg]), (0, LANES - N_GATE)).reshape(1, LANES)
    brow = jnp.concatenate([b_ig, b_fg]).reshape(N_GATE, 1)
    g1, b1 = ln1_g.reshape(1, d), ln1_b.reshape(1, d)
    g2, b2 = ln2_g.reshape(1, d), ln2_b.reshape(1, d)
    wrt = w_router.T
    brt = b_router.reshape(ne, 1)

    x2 = x_p.reshape(t, d)
    tm = _pick_tile(seq, 256)
    tabs_p = _rope_tables(jnp.arange(seq, dtype=jnp.int32))
    (qa, k_f, k_b, v_f, v_b, qm, km, vm, gcol, grow, og, ga, gm) = _proj_prompt(
        x2, w_a.astype(BF16), w_gc.astype(BF16), w_gt.T.astype(BF16), w_r.astype(BF16), tabs_p, seq, tm)
    a_in = _attn_prompt(qa, k_b, v_b, lam_p, sub_g, bp, seq, _pick_tile(seq, 256), lam_init)
    m_in, cn_p, m_p = _mlstm_prompt(qm, km, vm, gcol, grow, bcol, brow, og, mh_g2, bp, seq,
                                    _pick_tile(seq, 256))
    group = _pick_tile(t, 1024)
    h_p, hb_p, comb_p, mask_p, pos_p, cnt_p = _mix(
        a_in, m_in, ga, gm, x2, w_ba.astype(BF16), w_bm.astype(BF16), w_o.astype(BF16), g1, b1, wrt, brt,
        alpha, False, tm, group)

    xs2 = x_s.reshape(ns, d)
    chunk = 7 * LANES
    width = -(-(w_in.shape[1] + LANES - N_GATE) // chunk) * chunk
    w_cat = jnp.concatenate([w_a, w_gc, w_r], axis=1)
    w_cat = jnp.pad(w_cat, ((0, 0), (0, width - w_cat.shape[1])))
    z_s = _proj_sample(xs2, w_cat, chunk)
    tabs_s = _rope_tables(jnp.full((1,), past, jnp.int32))
    ppb = _pick_tile(page_table.shape[1], 8)
    n_pool = cache_k.shape[1]
    k_s, v_s, a_s = _paged_attn(page_table + l * n_pool, z_s.reshape(ns, 1, width), tabs_s, lam_p, sub_g,
                                cache_k.reshape(-1, PAGE_SIZE, QA_W),
                                cache_v.reshape(-1, PAGE_SIZE, QA_W), ppb, lam_init)
    o = 3 * QA_W
    qc = z_s[:, o:o + QM_W].reshape(ns, H_M, DK_M, 1)
    kc = z_s[:, o + QM_W:o + 2 * QM_W].reshape(ns, H_M, DK_M, 1)
    v3 = z_s[:, o + 2 * QM_W:W_A].reshape(ns, 1, VM_W)
    gates = z_s[:, W_A:W_A + N_GATE]
    r0 = W_A + LANES
    og3 = z_s[:, r0:r0 + VM_W].reshape(ns, 1, VM_W)
    ga_s = z_s[:, r0 + VM_W:r0 + VM_W + d]
    gm_s = z_s[:, r0 + VM_W + d:r0 + VM_W + 2 * d]
    m_in_s, c_s, n_s, m_s = _mlstm_step(gates, m0, b_ig, b_fg, qc, kc, v3, og3, c0,
                                        n0.reshape(ns, H_M, DK_M, 1), mh_g2)
    h_s, hb_s, comb_s, mask_s, pos_s, cnt_s = _mix(
        a_s.reshape(ns, QA_W), m_in_s.reshape(ns, VM_W), ga_s, gm_s, xs2, w_ba, w_bm, w_o, g1, b1,
        wrt, brt, alpha, True, ns, ns)

    padt = lambda a: jnp.pad(a, ((0, 0), (0, group - ns)))
    hb_all = jnp.concatenate([hb_p, jnp.pad(hb_s, ((0, group - ns), (0, 0)))], axis=0)
    comb_all = jnp.concatenate([comb_p, padt(comb_s)], axis=1)
    mask_all = jnp.concatenate([mask_p, padt(mask_s)], axis=1)
    pos_all = jnp.concatenate([pos_p, padt(pos_s)], axis=1)
    counts = jnp.concatenate([cnt_p[:, :, 0], cnt_s[:, :, 0]], axis=0).astype(jnp.int32).reshape(-1)
    ff = _moe(counts, hb_all, pos_all, mask_all, comb_all,
              w_gate.astype(BF16), b_gate.reshape(ne, 1, -1), w_up.astype(BF16), b_up.reshape(ne, 1, -1),
              w_down.astype(BF16), b_down.reshape(ne, 1, -1), group, 128)
    y_p = _ln2(h_p, ff, g2, b2, alpha, tm, 0)
    y_s = _ln2(h_s, ff, g2, b2, alpha, ns, t // ns)

    cw = cn_p[:, :, :DV_M].reshape(bp, H_M, DK_M, DV_M)
    nw = cn_p[:, :, DV_M].reshape(bp, H_M, DK_M)
    outs_p = (y_p.reshape(bp, seq, d), k_f.reshape(bp, seq, H_A, HEAD_W), v_f.reshape(bp, seq, H_A, HEAD_W),
              cw, nw, m_p[:, 0, :H_M])
    outs_s = (y_s.reshape(ns, 1, d), k_s.reshape(ns, 1, H_A, HEAD_W), v_s.reshape(ns, 1, H_A, HEAD_W),
              c_s, n_s.reshape(ns, H_M, DK_M), m_s[:, :, 0])
    return outs_p, outs_s


def kernel(x_prompt, x_sample, cache_k, cache_v, state_c, state_n, state_m, page_table, w_in, b_igate, b_fgate, lambda_q1, lambda_k1, lambda_q2, lambda_k2, subln_g, mh_norm_g, w_ba, w_bm, w_o, ln1_g, ln1_b, w_router, b_router, w_gate, b_gate, w_up, b_up, w_down, b_down, ln2_g, ln2_b):
    depth = w_in.shape[0]
    assert x_sample.shape[1] == 1, "the sample pass handles one new token per sequence"
    weights = (w_in, b_igate, b_fgate, lambda_q1, lambda_k1, lambda_q2, lambda_k2, subln_g, mh_norm_g,
               w_ba, w_bm, w_o, ln1_g, ln1_b, w_router, b_router, w_gate, b_gate, w_up, b_up,
               w_down, b_down, ln2_g, ln2_b)
    y_p, y_s = x_prompt, x_sample
    acc_p = [[] for _ in range(5)]
    acc_s = [[] for _ in range(5)]
    for l in range(depth):
        outs_p, outs_s = _layer(l, depth, y_p, y_s, cache_k, cache_v, state_c[l], state_n[l],
                                state_m[l], page_table, tuple(a[l] for a in weights))
        y_p, y_s = outs_p[0], outs_s[0]
        for i in range(5):
            acc_p[i].append(outs_p[1 + i])
            acc_s[i].append(outs_s[1 + i])
    return (y_p, y_s, *(jnp.stack(a) for a in acc_p), *(jnp.stack(a) for a in acc_s))
```

```python
import functools
import math

import jax
import jax.numpy as jnp
from jax import lax
from jax.experimental import pallas as pl
from jax.experimental.pallas import tpu as pltpu

F32 = jnp.float32
BF16 = jnp.bfloat16
HIGHEST = lax.Precision.HIGHEST

H_A = 4
D_HA = 64
ROT_DIM = D_HA // 4
ROPE_THETA = 500000.0
H_M = 4
DK_M = 64
DV_M = 128
GATE_SOFTCAP = 15.0
N_EXPERTS = 32
TOP_K = 4
SWIGLU_LIMIT = 7.0
SWIGLU_ALPHA = 1.702
LN_EPS = 1e-5
PAGE_SIZE = 128

QA_W = H_A * 2 * D_HA
QM_W = H_M * DK_M
VM_W = H_M * DV_M
HEAD_W = 2 * D_HA
N_GATE = 2 * H_M
W_A = 3 * QA_W + 2 * QM_W + VM_W

LANES = 128
VMEM_LIMIT = 56 * 1024 * 1024

ATTN_TQ = 256
ATTN_TK = 512
ATTN_HEADS_PER_STEP = 4
MOE_ROWS = 128
MOE_GROUP = 7 * LANES
MOE_NSUB = 3

NT_DIMS = (((1,), (1,)), ((), ()))
TN_DIMS = (((0,), (0,)), ((), ()))


def _params(*sem):
    return pltpu.CompilerParams(dimension_semantics=sem, vmem_limit_bytes=VMEM_LIMIT)


def _softcap(x):
    return GATE_SOFTCAP * jnp.tanh(x / GATE_SOFTCAP)


def _log_sigmoid(x):
    return jnp.minimum(x, 0.0) - jnp.log1p(jnp.exp(-jnp.abs(x)))


def _sigmoid(x):
    return 1.0 / (1.0 + jnp.exp(-x))


def _split_bf16(x):
    hi = x.astype(BF16)
    lo = (x - hi.astype(F32)).astype(BF16)
    return hi, lo


def _stack_split(x):
    hi, lo = _split_bf16(x)
    return jnp.concatenate([hi, lo], axis=0)


def _fold_split(t):
    n = t.shape[0] // 2
    return t[:n] + t[n:]


def _layer_norm(x, g, b):
    mu = jnp.mean(x, axis=-1, keepdims=True)
    xc = x - mu
    var = jnp.mean(xc * xc, axis=-1, keepdims=True)
    return xc * lax.rsqrt(var + LN_EPS) * g + b


def _rope(t, ra, rb, rc):
    return t * ra + pltpu.roll(t, 8, 1) * rb + pltpu.roll(t, HEAD_W - 8, 1) * rc


def _rope_tables(pos):
    inv = ROPE_THETA ** (-jnp.arange(0, ROT_DIM, 2, dtype=F32) / ROT_DIM)
    ang = pos.astype(F32)[:, None] * inv[None, :]
    cos, sin = jnp.cos(ang), jnp.sin(ang)
    n = pos.shape[0]
    half = ROT_DIM // 2
    rest = D_HA - ROT_DIM
    a = jnp.concatenate([cos, cos, jnp.ones((n, rest), F32)], axis=1)
    b = jnp.concatenate([jnp.zeros((n, half), F32), sin, jnp.zeros((n, rest), F32)], axis=1)
    c = jnp.concatenate([-sin, jnp.zeros((n, half + rest), F32)], axis=1)
    return tuple(jnp.tile(t, (1, 2)) for t in (a, b, c))


def _diff_lambda(lam_ref, lam_init):
    lp = lam_ref[...]
    s1 = jnp.sum(lp[0:1] * lp[1:2], axis=1, keepdims=True)
    s2 = jnp.sum(lp[2:3] * lp[3:4], axis=1, keepdims=True)
    return jnp.exp(s1) - jnp.exp(s2) + lam_init


def _proj_prompt_kernel(x_ref, wa_ref, wgc_ref, wgr_ref, wr_ref, ra_ref, rb_ref, rc_ref,
                        q_ref, kf_ref, kb_ref, vf_ref, vb_ref, qm_ref, km_ref, vm_ref,
                        gcol_ref, grow_ref, og_ref, ga_ref, gm_ref):
    x = x_ref[...].astype(BF16)
    za = jnp.dot(x, wa_ref[...], preferred_element_type=F32)
    ra, rb, rc = ra_ref[...], rb_ref[...], rc_ref[...]
    for h in range(H_A):
        sl = slice(h * HEAD_W, (h + 1) * HEAD_W)
        qh = _rope(za[:, sl], ra, rb, rc)
        q_ref[:, sl] = (qh * (D_HA ** -0.5)).astype(BF16)
        kh = _rope(za[:, QA_W + h * HEAD_W:QA_W + (h + 1) * HEAD_W], ra, rb, rc)
        kf_ref[:, sl] = kh
        kb_ref[:, sl] = kh.astype(BF16)
    v = za[:, 2 * QA_W:3 * QA_W]
    vf_ref[...] = v
    vb_ref[...] = v.astype(BF16)
    o = 3 * QA_W
    qm_ref[...] = za[:, o:o + QM_W].astype(BF16)
    km_ref[...] = (za[:, o + QM_W:o + 2 * QM_W] * (DK_M ** -0.5)).astype(BF16)
    vm_ref[...] = za[:, o + 2 * QM_W:].astype(BF16)
    gcol_ref[...] = jnp.dot(x, wgc_ref[...], preferred_element_type=F32)
    grow_ref[...] = lax.dot_general(wgr_ref[...], x, NT_DIMS, preferred_element_type=F32)
    zr = jnp.dot(x, wr_ref[...], preferred_element_type=F32)
    d = ga_ref.shape[1]
    og_ref[...] = zr[:, :VM_W].astype(BF16)
    ga_ref[...] = zr[:, VM_W:VM_W + d].astype(BF16)
    gm_ref[...] = zr[:, VM_W + d:].astype(BF16)


def _proj_prompt(x2, wa, wgc, wgr, wr, tabs, seq, tm):
    t, d = x2.shape
    nrep = seq // tm
    row = lambda w: pl.BlockSpec((tm, w), lambda i: (i, 0))
    full = lambda a: pl.BlockSpec(a.shape, lambda i: (0,) * a.ndim)
    tab = pl.BlockSpec((tm, HEAD_W), lambda i: (i % nrep, 0))
    sds = lambda w, dt: jax.ShapeDtypeStruct((t, w), dt)
    return pl.pallas_call(
        _proj_prompt_kernel,
        grid=(t // tm,),
        in_specs=[row(d), full(wa), full(wgc), full(wgr), full(wr), tab, tab, tab],
        out_specs=[row(QA_W), row(QA_W), row(QA_W), row(QA_W), row(QA_W),
                   row(QM_W), row(QM_W), row(VM_W), row(LANES),
                   pl.BlockSpec((N_GATE, tm), lambda i: (0, i)),
                   row(VM_W), row(d), row(d)],
        out_shape=[sds(QA_W, BF16), sds(QA_W, F32), sds(QA_W, BF16), sds(QA_W, F32), sds(QA_W, BF16),
                   sds(QM_W, BF16), sds(QM_W, BF16), sds(VM_W, BF16), sds(LANES, F32),
                   jax.ShapeDtypeStruct((N_GATE, t), F32),
                   sds(VM_W, BF16), sds(d, BF16), sds(d, BF16)],
        compiler_params=_params("parallel"),
        name="proj_prompt",
    )(x2, wa, wgc, wgr, wr, *tabs)


def _attn_prompt_kernel(q_ref, k_ref, v_ref, lam_ref, g_ref, o_ref, *, tq, tk, hps, lam_init):
    qi = pl.program_id(2)
    nfull = (qi * tq) // tk
    lane = lax.broadcasted_iota(jnp.int32, (1, HEAD_W), 1)
    r = lax.broadcasted_iota(jnp.int32, (tq, tk), 0) + qi * tq
    c = lax.broadcasted_iota(jnp.int32, (tq, tk), 1) + nfull * tk
    causal = jnp.concatenate([c <= r, c <= r], axis=0)
    ones_blk = jnp.broadcast_to(jnp.where(lane == 0, 1.0, 0.0).astype(BF16), (tk, HEAD_W))
    qqs = []
    for hh in range(hps):
        q = q_ref[:, hh * HEAD_W:(hh + 1) * HEAD_W]
        zero = jnp.zeros_like(q)
        qqs.append(jnp.concatenate([jnp.where(lane < D_HA, q, zero), jnp.where(lane >= D_HA, q, zero)],
                                   axis=0))

    def step(j, carry, masked):
        off = pl.multiple_of(j * tk, tk)
        out = []
        for hh in range(hps):
            m, acc = carry[hh]
            sl = slice(hh * HEAD_W, (hh + 1) * HEAD_W)
            k = k_ref[pl.ds(off, tk), sl]
            vext = jnp.concatenate([v_ref[pl.ds(off, tk), sl], ones_blk], axis=1)
            s = lax.dot_general(qqs[hh], k, NT_DIMS, preferred_element_type=F32)
            if masked:
                s = jnp.where(causal, s, -jnp.inf)
            m_new = jnp.maximum(m, jnp.max(s, axis=1, keepdims=True))
            p = jnp.exp(s - m_new)
            acc = jnp.exp(m - m_new) * acc + jnp.dot(p.astype(BF16), vext, preferred_element_type=F32)
            out.append((m_new, acc))
        return tuple(out)

    init = tuple((jnp.full((2 * tq, 1), -jnp.inf, F32), jnp.zeros((2 * tq, 2 * HEAD_W), F32))
                 for _ in range(hps))
    carry = lax.fori_loop(0, nfull, lambda j, cr: step(j, cr, False), init)
    carry = step(nfull, carry, True)
    lam = _diff_lambda(lam_ref, lam_init)
    for hh in range(hps):
        acc = carry[hh][1]
        o1 = acc[:tq, :HEAD_W] / acc[:tq, HEAD_W:HEAD_W + 1]
        o2 = acc[tq:, :HEAD_W] / acc[tq:, HEAD_W:HEAD_W + 1]
        o = o1 - lam * o2
        y = o * lax.rsqrt(jnp.mean(o * o, axis=1, keepdims=True) + LN_EPS) * g_ref[...] * (1.0 - lam_init)
        o_ref[:, hh * HEAD_W:(hh + 1) * HEAD_W] = y.astype(o_ref.dtype)


def _attn_prompt(q, k, v, lam_p, subln_g, batch, seq, lam_init):
    t = q.shape[0]
    tk = _pick_tile(seq, ATTN_TK)
    tq = _pick_tile(tk, ATTN_TQ)
    nq = seq // tq
    hps = ATTN_HEADS_PER_STEP
    w = hps * HEAD_W
    return pl.pallas_call(
        functools.partial(_attn_prompt_kernel, tq=tq, tk=tk, hps=hps, lam_init=lam_init),
        grid=(batch, H_A // hps, nq),
        in_specs=[pl.BlockSpec((tq, w), lambda b, h, i: (b * nq + i, h)),
                  pl.BlockSpec((seq, w), lambda b, h, i: (b, h)),
                  pl.BlockSpec((seq, w), lambda b, h, i: (b, h)),
                  pl.BlockSpec(lam_p.shape, lambda b, h, i: (0, 0)),
                  pl.BlockSpec((1, HEAD_W), lambda b, h, i: (0, 0))],
        out_specs=pl.BlockSpec((tq, w), lambda b, h, i: (b * nq + i, h)),
        out_shape=jax.ShapeDtypeStruct((t, QA_W), BF16),
        compiler_params=_params("parallel", "parallel", "parallel"),
        name="attn_prompt",
    )(q, k, v, lam_p, subln_g)


def _mlstm_prompt_kernel(q_ref, k_ref, v_ref, gcol_ref, grow_ref, bcol_ref, brow_ref, og_ref, g_ref,
                         min_ref, cn_ref, m_ref, *, lc):
    ci = pl.program_id(1)

    @pl.when(ci == 0)
    def _():
        cn_ref[...] = jnp.zeros_like(cn_ref)
        m_ref[...] = jnp.zeros_like(m_ref)

    gc = _softcap(gcol_ref[...] + bcol_ref[...])
    lfc = _log_sigmoid(gc)
    gr = _softcap(grow_ref[...] + brow_ref[...])
    lfr = _log_sigmoid(gr)
    ri = lax.broadcasted_iota(jnp.int32, (lc, lc), 0)
    cj = lax.broadcasted_iota(jnp.int32, (lc, lc), 1)
    tril = cj <= ri
    b_col = jnp.dot(tril.astype(F32), lfc, precision=HIGHEST, preferred_element_type=F32)
    b_row = jnp.dot(lfr, (ri <= cj).astype(F32), precision=HIGHEST, preferred_element_type=F32)

    cn = cn_ref[0]
    cn_bf = cn.astype(BF16)
    q = q_ref[...]
    k = k_ref[...]
    v = v_ref[...]
    m_all = m_ref[0]
    lane_q = lax.broadcasted_iota(jnp.int32, (1, QM_W), 1) // DK_M
    row_q = lax.broadcasted_iota(jnp.int32, (QM_W, 1), 0) // DK_M
    lane_v = lax.broadcasted_iota(jnp.int32, (1, LANES), 1)
    ones_blk = jnp.broadcast_to(jnp.where(lane_v == 0, 1.0, 0.0).astype(BF16), (lc, LANES))
    new_cn = jnp.zeros_like(cn)
    new_m = m_all
    for h in range(H_M):
        sl = slice(h * DV_M, (h + 1) * DV_M)
        qh = jnp.where(lane_q == h, q, jnp.zeros_like(q))
        kh = jnp.where(lane_q == h, k, jnp.zeros_like(k))
        b_c = b_col[:, H_M + h:H_M + h + 1]
        ig_c = gc[:, h:h + 1]
        b_r = b_row[H_M + h:H_M + h + 1, :]
        ig_r = gr[h:h + 1, :]
        m_prev = m_all[:, h:h + 1]
        dmat = jnp.where(tril, b_c - b_r + ig_r, -jnp.inf)
        m_inter = b_c + m_prev
        m_t = jnp.maximum(m_inter, jnp.max(dmat, axis=1, keepdims=True))
        w_inter = jnp.exp(m_inter - m_t)
        s = lax.dot_general(qh, kh, NT_DIMS, preferred_element_type=F32) * jnp.exp(dmat - m_t)
        vext = jnp.concatenate([v[:, sl], ones_blk], axis=1)
        nd = (w_inter * jnp.dot(qh, cn_bf, preferred_element_type=F32)
              + jnp.dot(s.astype(BF16), vext, preferred_element_type=F32))
        num = nd[:, :DV_M]
        den = nd[:, DV_M:DV_M + 1]
        hout = num / jnp.maximum(jnp.abs(den), jnp.exp(-m_t))
        mu = jnp.mean(hout, axis=1, keepdims=True)
        hc = hout - mu
        var = jnp.mean(hc * hc, axis=1, keepdims=True)
        y = hc * lax.rsqrt(var + LN_EPS) * g_ref[:, sl] * _sigmoid(og_ref[:, sl].astype(F32))
        min_ref[:, sl] = y.astype(min_ref.dtype)
        m_new = m_t[lc - 1:lc, :]
        b_last = b_c[lc - 1:lc, :]
        decay = jnp.exp(b_last + m_prev - m_new)
        wk = jnp.exp(b_last - b_c + ig_c - m_new)
        upd = lax.dot_general(kh, (wk * vext.astype(F32)).astype(BF16), TN_DIMS,
                              preferred_element_type=F32)
        new_cn = new_cn + jnp.where(row_q == h, decay * cn, 0.0) + upd
        new_m = jnp.where(lane_v == h, m_new, new_m)
    cn_ref[0] = new_cn
    m_ref[0] = new_m


def _mlstm_prompt(qm, km, vm, gcol, grow, bcol, brow, og, mh_g, batch, seq, lc):
    t = qm.shape[0]
    nc = seq // lc
    row = lambda w: pl.BlockSpec((lc, w), lambda b, c: (b * nc + c, 0))
    const = lambda a: pl.BlockSpec(a.shape, lambda b, c: (0,) * a.ndim)
    cw = 2 * LANES
    return pl.pallas_call(
        functools.partial(_mlstm_prompt_kernel, lc=lc),
        grid=(batch, nc),
        in_specs=[row(QM_W), row(QM_W), row(VM_W), row(LANES),
                  pl.BlockSpec((N_GATE, lc), lambda b, c: (0, b * nc + c)),
                  const(bcol), const(brow), row(VM_W), const(mh_g)],
        out_specs=[row(VM_W),
                   pl.BlockSpec((1, QM_W, cw), lambda b, c: (b, 0, 0)),
                   pl.BlockSpec((1, 1, LANES), lambda b, c: (b, 0, 0))],
        out_shape=[jax.ShapeDtypeStruct((t, VM_W), BF16),
                   jax.ShapeDtypeStruct((batch, QM_W, cw), F32),
                   jax.ShapeDtypeStruct((batch, 1, LANES), F32)],
        compiler_params=_params("parallel", "arbitrary"),
        name="mlstm_prompt",
    )(qm, km, vm, gcol, grow, bcol, brow, og, mh_g)


def _mix_kernel(a_ref, mi_ref, ga_ref, gm_ref, x_ref, wba_ref, wbm_ref, wo_ref, g1_ref, b1_ref,
                wrt_ref, brt_ref, h_ref, hb_ref, comb_ref, mask_ref, *, alpha, precise):
    def mm(a, w_ref):
        if precise:
            return jnp.dot(a.astype(F32), w_ref[...], precision=HIGHEST, preferred_element_type=F32)
        return jnp.dot(a.astype(BF16), w_ref[...], preferred_element_type=F32)

    a_br = mm(a_ref[...], wba_ref)
    m_br = mm(mi_ref[...], wbm_ref)
    merged = _sigmoid(ga_ref[...].astype(F32)) * a_br + _sigmoid(gm_ref[...].astype(F32)) * m_br
    mix = mm(merged, wo_ref)
    h = _layer_norm(alpha * x_ref[...] + mix, g1_ref[...], b1_ref[...])
    h_ref[...] = h
    hb_ref[...] = h.astype(BF16)

    ne = wrt_ref.shape[0]
    if precise:
        logits = lax.dot_general(wrt_ref[...], h, NT_DIMS, precision=HIGHEST, preferred_element_type=F32)
    else:
        ws = _stack_split(wrt_ref[...])
        h_hi, h_lo = _split_bf16(h)
        logits = (_fold_split(lax.dot_general(ws, h_hi, NT_DIMS, preferred_element_type=F32))
                  + lax.dot_general(ws[:ne], h_lo, NT_DIMS, preferred_element_type=F32))
    logits = logits + brt_ref[...]
    tm = logits.shape[1]
    eidx = lax.broadcasted_iota(jnp.int32, (ne, tm), 0)
    work = logits
    sel = jnp.zeros((ne, tm), jnp.bool_)
    top = None
    for _ in range(TOP_K):
        mx = jnp.max(work, axis=0, keepdims=True)
        top = mx if top is None else top
        first = jnp.min(jnp.where(work == mx, eidx, ne), axis=0, keepdims=True)
        pick = eidx == first
        sel = jnp.logical_or(sel, pick)
        work = jnp.where(pick, -jnp.inf, work)
    ex = jnp.where(sel, jnp.exp(logits - top), 0.0)
    comb_ref[...] = ex / jnp.sum(ex, axis=0, keepdims=True)
    mask_ref[...] = jnp.where(sel, 1.0, 0.0)


def _mix(a_in, m_in, ga, gm, x2, wba, wbm, wo, g1, b1, wrt, brt, alpha, precise, tm):
    t, d = x2.shape
    ne = wrt.shape[0]
    row = lambda w: pl.BlockSpec((tm, w), lambda i: (i, 0))
    col = pl.BlockSpec((ne, tm), lambda i: (0, i))
    const = lambda a: pl.BlockSpec(a.shape, lambda i: (0,) * a.ndim)
    return pl.pallas_call(
        functools.partial(_mix_kernel, alpha=alpha, precise=precise),
        grid=(t // tm,),
        in_specs=[row(QA_W), row(VM_W), row(d), row(d), row(d), const(wba), const(wbm), const(wo),
                  const(g1), const(b1), const(wrt), const(brt)],
        out_specs=[row(d), row(d), col, col],
        out_shape=[jax.ShapeDtypeStruct((t, d), F32), jax.ShapeDtypeStruct((t, d), BF16),
                   jax.ShapeDtypeStruct((ne, t), F32), jax.ShapeDtypeStruct((ne, t), F32)],
        compiler_params=_params("parallel"),
        name="mix_sample" if precise else "mix_prompt",
    )(a_in, m_in, ga, gm, x2, wba, wbm, wo, g1, b1, wrt, brt)


def _route_kernel(mask_ref, pos_ref, cnt_ref):
    maskf = mask_ref[...]
    g = maskf.shape[1]
    r = lax.broadcasted_iota(jnp.int32, (g, g), 0)
    c = lax.broadcasted_iota(jnp.int32, (g, g), 1)
    before = jnp.dot(maskf.astype(BF16), (r < c).astype(BF16), preferred_element_type=F32)
    pos_ref[...] = jnp.where(maskf > 0.0, before, -1.0)
    cnt_ref[0] = jnp.broadcast_to(jnp.sum(maskf, axis=1, keepdims=True), cnt_ref.shape[1:])


def _route(mask, group):
    ne, t = mask.shape
    ng = t // group
    return pl.pallas_call(
        _route_kernel,
        grid=(ng,),
        in_specs=[pl.BlockSpec((ne, group), lambda g: (0, g))],
        out_specs=[pl.BlockSpec((ne, group), lambda g: (0, g)),
                   pl.BlockSpec((1, ne, LANES), lambda g: (g, 0, 0))],
        out_shape=[jax.ShapeDtypeStruct((ne, t), F32), jax.ShapeDtypeStruct((ng, ne, LANES), F32)],
        compiler_params=_params("parallel"),
        name="route",
    )(mask)


def _moe_kernel(cnt_ref, h_ref, pos_ref, comb_ref, wg_ref, bg_ref, wu_ref, bu_ref,
                wd_ref, bd_ref, o_ref, *, rows, group, nsub):
    sg = pl.program_id(0)
    e = pl.program_id(1)
    ne = pl.num_programs(1)

    @pl.when(e == 0)
    def _():
        o_ref[...] = jnp.zeros_like(o_ref)

    slot0 = lax.broadcasted_iota(jnp.int32, (rows, 1), 0).astype(F32)
    for sub in range(nsub):
        tok = slice(sub * group, (sub + 1) * group)
        n = cnt_ref[(sg * nsub + sub) * ne + e]
        pos = pos_ref[pl.ds(e, 1), tok]
        cw = comb_ref[pl.ds(e, 1), tok]

        def tile(s, carry, tok=tok, pos=pos, cw=cw):
            hit = pos == slot0 + (s * rows).astype(F32)
            xg = jnp.dot(jnp.where(hit, 1.0, 0.0).astype(BF16), h_ref[tok, :],
                         preferred_element_type=F32).astype(BF16)
            gate = jnp.minimum(jnp.dot(xg, wg_ref[0], preferred_element_type=F32) + bg_ref[0],
                               SWIGLU_LIMIT)
            up = jnp.clip(jnp.dot(xg, wu_ref[0], preferred_element_type=F32) + bu_ref[0],
                          -SWIGLU_LIMIT, SWIGLU_LIMIT)
            hid = gate * _sigmoid(SWIGLU_ALPHA * gate) * (up + 1.0)
            y = jnp.dot(hid.astype(BF16), wd_ref[0], preferred_element_type=F32) + bd_ref[0]
            gw = jnp.where(hit, cw, 0.0).astype(BF16)
            o_ref[tok, :] += lax.dot_general(gw, y.astype(BF16), TN_DIMS, preferred_element_type=F32)
            return carry

        lax.fori_loop(0, (n + rows - 1) // rows, tile, 0)


def _moe(counts, hb, pos, comb, wg, bg, wu, bu, wd, bd, group, nsub, rows):
    t, d = hb.shape
    ne = wg.shape[0]
    sgroup = group * nsub
    tok = pl.BlockSpec((ne, sgroup), lambda g, e, c: (0, g))
    wspec = lambda a: pl.BlockSpec((1,) + a.shape[1:], lambda g, e, c: (e, 0, 0))
    grid_spec = pltpu.PrefetchScalarGridSpec(
        num_scalar_prefetch=1,
        grid=(t // sgroup, ne),
        in_specs=[pl.BlockSpec((sgroup, d), lambda g, e, c: (g, 0)), tok, tok,
                  wspec(wg), wspec(bg), wspec(wu), wspec(bu), wspec(wd), wspec(bd)],
        out_specs=pl.BlockSpec((sgroup, d), lambda g, e, c: (g, 0)),
    )
    return pl.pallas_call(
        functools.partial(_moe_kernel, rows=rows, group=group, nsub=nsub),
        grid_spec=grid_spec,
        out_shape=jax.ShapeDtypeStruct((t, d), F32),
        compiler_params=_params("parallel", "arbitrary"),
        name="moe",
    )(counts, hb, pos, comb, wg, bg, wu, bu, wd, bd)


def _ln2_kernel(h_ref, ff_ref, g_ref, b_ref, y_ref, *, alpha):
    y_ref[...] = _layer_norm(alpha * h_ref[...] + ff_ref[...], g_ref[...], b_ref[...])


def _ln2(h, ff, g2, b2, alpha, tm, ff_block0):
    t, d = h.shape
    const = lambda a: pl.BlockSpec(a.shape, lambda i: (0,) * a.ndim)
    return pl.pallas_call(
        functools.partial(_ln2_kernel, alpha=alpha),
        grid=(t // tm,),
        in_specs=[pl.BlockSpec((tm, d), lambda i: (i, 0)),
                  pl.BlockSpec((tm, d), lambda i: (ff_block0 + i, 0)), const(g2), const(b2)],
        out_specs=pl.BlockSpec((tm, d), lambda i: (i, 0)),
        out_shape=jax.ShapeDtypeStruct((t, d), F32),
        compiler_params=_params("parallel"),
        name="ln2",
    )(h, ff, g2, b2)


def _proj_sample_kernel(x_ref, w_ref, z_ref):
    z_ref[...] = jnp.dot(x_ref[...], w_ref[...], precision=HIGHEST, preferred_element_type=F32)


def _proj_sample(xs, w, chunk):
    n, d = xs.shape
    width = w.shape[1]
    return pl.pallas_call(
        _proj_sample_kernel,
        grid=(width // chunk,),
        in_specs=[pl.BlockSpec((n, d), lambda j: (0, 0)), pl.BlockSpec((d, chunk), lambda j: (0, j))],
        out_specs=pl.BlockSpec((n, chunk), lambda j: (0, j)),
        out_shape=jax.ShapeDtypeStruct((n, width), F32),
        compiler_params=_params("parallel"),
        name="proj_sample",
    )(xs, w)


def _paged_attn_kernel(pt_ref, z_ref, ra_ref, rb_ref, rc_ref, lam_ref, g_ref, *rest,
                       ppb, lam_init):
    k_refs = rest[:ppb]
    v_refs = rest[ppb:2 * ppb]
    kout_ref, vout_ref, a_ref, q_scr, qs_scr, m_scr, l_scr, acc_scr = rest[2 * ppb:]
    j = pl.program_id(1)
    nj = pl.num_programs(1)
    nrow = 2 * H_A
    prow = PAGE_SIZE * H_A
    ra, rb, rc = ra_ref[...], rb_ref[...], rc_ref[...]
    lane = lax.broadcasted_iota(jnp.int32, (1, HEAD_W), 1)

    def per_row(t):
        return jnp.concatenate([t[:, (r // 2) * HEAD_W:(r // 2 + 1) * HEAD_W] for r in range(nrow)], axis=0)

    @pl.when(j == 0)
    def _():
        z = z_ref[0]
        rows = []
        for h in range(H_A):
            qh = _rope(z[:, h * HEAD_W:(h + 1) * HEAD_W], ra, rb, rc) * (D_HA ** -0.5)
            rows += [jnp.where(lane < D_HA, qh, 0.0), jnp.where(lane >= D_HA, qh, 0.0)]
        q = jnp.concatenate(rows, axis=0)
        q_scr[...] = q
        qs_scr[...] = _stack_split(q)
        m_scr[...] = jnp.full_like(m_scr, -jnp.inf)
        l_scr[...] = jnp.zeros_like(l_scr)
        acc_scr[...] = jnp.zeros_like(acc_scr)

    qs = qs_scr[...]
    parts = []
    for kr in k_refs:
        k_hi, k_lo = _split_bf16(kr[0])
        t = (lax.dot_general(qs, k_hi, NT_DIMS, preferred_element_type=F32)
             + lax.dot_general(qs, k_lo, NT_DIMS, preferred_element_type=F32))
        parts.append(_fold_split(t))
    s = jnp.concatenate(parts, axis=1)
    r_head = lax.broadcasted_iota(jnp.int32, s.shape, 0) // 2
    c_head = lax.broadcasted_iota(jnp.int32, s.shape, 1) % H_A
    s = jnp.where(r_head == c_head, s, -jnp.inf)
    m_old = m_scr[...]
    m_new = jnp.maximum(m_old, jnp.max(s, axis=1, keepdims=True))
    alpha = jnp.exp(m_old - m_new)
    p = jnp.exp(s - m_new)
    l_scr[...] = alpha * l_scr[...] + jnp.sum(p, axis=1, keepdims=True)
    ps = _stack_split(p)
    pv = None
    for i, vr in enumerate(v_refs):
        v_hi, v_lo = _split_bf16(vr[0])
        pi = ps[:, i * prow:(i + 1) * prow]
        t = jnp.dot(pi, v_hi, preferred_element_type=F32) + jnp.dot(pi, v_lo, preferred_element_type=F32)
        pv = t if pv is None else pv + t
    acc_scr[...] = alpha * acc_scr[...] + _fold_split(pv)
    m_scr[...] = m_new

    @pl.when(j == nj - 1)
    def _():
        z = z_ref[0]
        k_new = jnp.concatenate([_rope(z[:, QA_W + h * HEAD_W:QA_W + (h + 1) * HEAD_W], ra, rb, rc)
                                 for h in range(H_A)], axis=1)
        v_new = z[:, 2 * QA_W:3 * QA_W]
        kout_ref[0] = k_new
        vout_ref[0] = v_new
        s_self = jnp.sum(q_scr[...] * per_row(k_new), axis=1, keepdims=True)
        m_old2 = m_scr[...]
        m_fin = jnp.maximum(m_old2, s_self)
        a2 = jnp.exp(m_old2 - m_fin)
        p_self = jnp.exp(s_self - m_fin)
        l_fin = a2 * l_scr[...] + p_self
        o_all = (a2 * acc_scr[...] + p_self * per_row(v_new)) / l_fin
        lam = _diff_lambda(lam_ref, lam_init)
        outs = []
        for h in range(H_A):
            o = o_all[2 * h:2 * h + 1] - lam * o_all[2 * h + 1:2 * h + 2]
            outs.append(o * lax.rsqrt(jnp.mean(o * o, axis=1, keepdims=True) + LN_EPS)
                        * g_ref[...] * (1.0 - lam_init))
        a_ref[0] = jnp.concatenate(outs, axis=1)


def _paged_attn(page_table, z3, tabs, lam_p, subln_g, ck, cv, ppb, lam_init):
    nb, npages = page_table.shape
    width = z3.shape[2]
    nj = npages // ppb
    pt = page_table.reshape(-1)
    prow = PAGE_SIZE * H_A
    nrow = 2 * H_A
    const = lambda a: pl.BlockSpec(a.shape, lambda b, j, p: (0,) * a.ndim)
    page = lambda i: pl.BlockSpec((1, prow, HEAD_W),
                                  lambda b, j, p: (p[b * npages + j * ppb + i], 0, 0))
    seq3 = lambda w: pl.BlockSpec((1, 1, w), lambda b, j, p: (b, 0, 0))
    grid_spec = pltpu.PrefetchScalarGridSpec(
        num_scalar_prefetch=1,
        grid=(nb, nj),
        in_specs=[seq3(width), const(tabs[0]), const(tabs[1]), const(tabs[2]), const(lam_p),
                  const(subln_g)] + [page(i) for i in range(ppb)] + [page(i) for i in range(ppb)],
        out_specs=[seq3(QA_W), seq3(QA_W), seq3(QA_W)],
        scratch_shapes=[pltpu.VMEM((nrow, HEAD_W), F32), pltpu.VMEM((2 * nrow, HEAD_W), BF16),
                        pltpu.VMEM((nrow, 1), F32), pltpu.VMEM((nrow, 1), F32),
                        pltpu.VMEM((nrow, HEAD_W), F32)],
    )
    sds = jax.ShapeDtypeStruct((nb, 1, QA_W), F32)
    return pl.pallas_call(
        functools.partial(_paged_attn_kernel, ppb=ppb, lam_init=lam_init),
        grid_spec=grid_spec,
        out_shape=[sds, sds, sds],
        compiler_params=_params("parallel", "arbitrary"),
        name="paged_attn",
    )(pt, z3, *tabs, lam_p, subln_g, *([ck] * ppb), *([cv] * ppb))


def _mlstm_step_kernel(gates_ref, m0_ref, big_ref, bfg_ref, qc_ref, kc_ref, v_ref, og_ref, c0_ref,
                       n0_ref, g_ref, min_ref, c_ref, n_ref, m_ref):
    b = pl.program_id(0)
    outs = []
    for h in range(H_M):
        sl = slice(h * DV_M, (h + 1) * DV_M)
        ig = _softcap(jnp.full((1, LANES), gates_ref[b, h] + big_ref[h], F32))
        lf = _log_sigmoid(_softcap(jnp.full((1, LANES), gates_ref[b, H_M + h] + bfg_ref[h], F32)))
        m0 = jnp.full((1, LANES), m0_ref[b, h], F32)
        m_inter = lf + m0
        m_t = jnp.maximum(m_inter, ig)
        w_inter = jnp.exp(m_inter - m_t)
        w_new = jnp.exp(ig - m_t)
        qc = qc_ref[0, h]
        kc = kc_ref[0, h] * (DK_M ** -0.5)
        vr = v_ref[0][:, sl]
        c0 = c0_ref[0, h]
        n0 = n0_ref[0, h]
        qk = jnp.sum(qc * kc, axis=0, keepdims=True)
        s = qk * w_new
        num = w_inter * jnp.sum(qc * c0, axis=0, keepdims=True) + s * vr
        den = w_inter * jnp.sum(qc * n0, axis=0, keepdims=True) + s
        hout = num / jnp.maximum(jnp.abs(den), jnp.exp(-m_t))
        mu = jnp.mean(hout, axis=1, keepdims=True)
        hc = hout - mu
        var = jnp.mean(hc * hc, axis=1, keepdims=True)
        outs.append(hc * lax.rsqrt(var + LN_EPS) * g_ref[:, sl] * _sigmoid(og_ref[0][:, sl]))
        c_ref[0, h] = w_inter * c0 + (w_new * kc) * vr
        n_ref[0, h] = w_inter[:, 0:1] * n0 + w_new[:, 0:1] * kc
        m_ref[0, h:h + 1, :] = m_t
    min_ref[0] = jnp.concatenate(outs, axis=1)


def _mlstm_step(gates, m0, big, bfg, qc, kc, v3, og3, c0, n0c, mh_g):
    nb = c0.shape[0]
    smem = pl.BlockSpec(memory_space=pltpu.SMEM)
    per = lambda a: pl.BlockSpec((1,) + a.shape[1:], lambda b: (b,) + (0,) * (a.ndim - 1))
    const = lambda a: pl.BlockSpec(a.shape, lambda b: (0,) * a.ndim)
    return pl.pallas_call(
        _mlstm_step_kernel,
        grid=(nb,),
        in_specs=[smem, smem, smem, smem, per(qc), per(kc), per(v3), per(og3), per(c0), per(n0c),
                  const(mh_g)],
        out_specs=[per(v3), per(c0), per(n0c), pl.BlockSpec((1, H_M, LANES), lambda b: (b, 0, 0))],
        out_shape=[jax.ShapeDtypeStruct(v3.shape, F32), jax.ShapeDtypeStruct(c0.shape, F32),
                   jax.ShapeDtypeStruct(n0c.shape, F32),
                   jax.ShapeDtypeStruct((nb, H_M, LANES), F32)],
        compiler_params=_params("parallel"),
        name="mlstm_step",
    )(gates, m0, big, bfg, qc, kc, v3, og3, c0, n0c, mh_g)


def _pick_tile(n, target):
    t = min(n, target)
    while n % t:
        t //= 2
    return t


def _layer(l, depth, x_p, x_s, cache_k, cache_v, c0, n0, m0, page_table, w):
    (w_in, b_ig, b_fg, lq1, lk1, lq2, lk2, subln_g, mh_g, w_ba, w_bm, w_o, ln1_g, ln1_b,
     w_router, b_router, w_gate, b_gate, w_up, b_up, w_down, b_down, ln2_g, ln2_b) = w
    alpha = (2.0 * depth) ** 0.25
    lam_init = 0.8 - 0.6 * math.exp(-0.3 * l)
    bp, seq, d = x_p.shape
    ns = x_s.shape[0]
    t = bp * seq
    ne = w_router.shape[1]
    past = page_table.shape[1] * PAGE_SIZE

    w_a = w_in[:, :W_A]
    w_gt = w_in[:, W_A:W_A + N_GATE]
    w_r = w_in[:, W_A + N_GATE:]
    w_gc = jnp.pad(w_gt, ((0, 0), (0, LANES - N_GATE)))
    lam_p = jnp.stack([lq1, lk1, lq2, lk2])
    sub_g = subln_g.reshape(1, HEAD_W)
    mh_g2 = mh_g.reshape(1, VM_W)
    bcol = jnp.pad(jnp.concatenate([b_ig, b_fg]), (0, LANES - N_GATE)).reshape(1, LANES)
    brow = jnp.concatenate([b_ig, b_fg]).reshape(N_GATE, 1)
    g1, b1 = ln1_g.reshape(1, d), ln1_b.reshape(1, d)
    g2, b2 = ln2_g.reshape(1, d), ln2_b.reshape(1, d)
    wrt = w_router.T
    brt = b_router.reshape(ne, 1)

    x2 = x_p.reshape(t, d)
    tm = _pick_tile(seq, 256)
    tabs_p = _rope_tables(jnp.arange(seq, dtype=jnp.int32))
    (qa, k_f, k_b, v_f, v_b, qm, km, vm, gcol, grow, og, ga, gm) = _proj_prompt(
        x2, w_a.astype(BF16), w_gc.astype(BF16), w_gt.T.astype(BF16), w_r.astype(BF16), tabs_p, seq, tm)
    a_in = _attn_prompt(qa, k_b, v_b, lam_p, sub_g, bp, seq, lam_init)
    m_in, cn_p, m_p = _mlstm_prompt(qm, km, vm, gcol, grow, bcol, brow, og, mh_g2, bp, seq,
                                    _pick_tile(seq, 256))
    h_p, hb_p, comb_p, mask_p = _mix(
        a_in, m_in, ga, gm, x2, w_ba.astype(BF16), w_bm.astype(BF16), w_o.astype(BF16), g1, b1, wrt, brt,
        alpha, False, tm)

    xs2 = x_s.reshape(ns, d)
    chunk = 7 * LANES
    width = -(-(w_in.shape[1] + LANES - N_GATE) // chunk) * chunk
    w_cat = jnp.concatenate([w_a, w_gc, w_r], axis=1)
    w_cat = jnp.pad(w_cat, ((0, 0), (0, width - w_cat.shape[1])))
    z_s = _proj_sample(xs2, w_cat, chunk)
    tabs_s = _rope_tables(jnp.full((1,), past, jnp.int32))
    ppb = _pick_tile(page_table.shape[1], 16)
    n_pool = cache_k.shape[1]
    k_s, v_s, a_s = _paged_attn(page_table + l * n_pool, z_s.reshape(ns, 1, width), tabs_s, lam_p, sub_g,
                                cache_k.reshape(-1, PAGE_SIZE * H_A, HEAD_W),
                                cache_v.reshape(-1, PAGE_SIZE * H_A, HEAD_W), ppb, lam_init)
    o = 3 * QA_W
    qc = z_s[:, o:o + QM_W].reshape(ns, H_M, DK_M, 1)
    kc = z_s[:, o + QM_W:o + 2 * QM_W].reshape(ns, H_M, DK_M, 1)
    v3 = z_s[:, o + 2 * QM_W:W_A].reshape(ns, 1, VM_W)
    gates = z_s[:, W_A:W_A + N_GATE]
    r0 = W_A + LANES
    og3 = z_s[:, r0:r0 + VM_W].reshape(ns, 1, VM_W)
    ga_s = z_s[:, r0 + VM_W:r0 + VM_W + d]
    gm_s = z_s[:, r0 + VM_W + d:r0 + VM_W + 2 * d]
    m_in_s, c_s, n_s, m_s = _mlstm_step(gates, m0, b_ig, b_fg, qc, kc, v3, og3, c0,
                                        n0.reshape(ns, H_M, DK_M, 1), mh_g2)
    h_s, hb_s, comb_s, mask_s = _mix(
        a_s.reshape(ns, QA_W), m_in_s.reshape(ns, VM_W), ga_s, gm_s, xs2, w_ba, w_bm, w_o, g1, b1,
        wrt, brt, alpha, True, ns)

    sgroup = MOE_GROUP * MOE_NSUB
    t_pad = -(-(t + ns) // sgroup) * sgroup
    padt = lambda p, s: jnp.pad(jnp.concatenate([p, s], axis=1), ((0, 0), (0, t_pad - t - ns)))
    hb_all = jnp.pad(jnp.concatenate([hb_p, hb_s], axis=0), ((0, t_pad - t - ns), (0, 0)))
    comb_all = padt(comb_p, comb_s)
    pos_all, cnt = _route(padt(mask_p, mask_s), MOE_GROUP)
    counts = cnt[:, :, 0].astype(jnp.int32).reshape(-1)
    ff = _moe(counts, hb_all, pos_all, comb_all,
              w_gate.astype(BF16), b_gate.reshape(ne, 1, -1), w_up.astype(BF16), b_up.reshape(ne, 1, -1),
              w_down.astype(BF16), b_down.reshape(ne, 1, -1), MOE_GROUP, MOE_NSUB, MOE_ROWS)
    y_p = _ln2(h_p, ff, g2, b2, alpha, tm, 0)
    y_s = _ln2(h_s, ff, g2, b2, alpha, ns, t // ns)

    cw = cn_p[:, :, :DV_M].reshape(bp, H_M, DK_M, DV_M)
    nw = cn_p[:, :, DV_M].reshape(bp, H_M, DK_M)
    outs_p = (y_p.reshape(bp, seq, d), k_f.reshape(bp, seq, H_A, HEAD_W), v_f.reshape(bp, seq, H_A, HEAD_W),
              cw, nw, m_p[:, 0, :H_M])
    outs_s = (y_s.reshape(ns, 1, d), k_s.reshape(ns, 1, H_A, HEAD_W), v_s.reshape(ns, 1, H_A, HEAD_W),
              c_s, n_s.reshape(ns, H_M, DK_M), m_s[:, :, 0])
    return outs_p, outs_s


def kernel(x_prompt, x_sample, cache_k, cache_v, state_c, state_n, state_m, page_table, w_in, b_igate, b_fgate, lambda_q1, lambda_k1, lambda_q2, lambda_k2, subln_g, mh_norm_g, w_ba, w_bm, w_o, ln1_g, ln1_b, w_router, b_router, w_gate, b_gate, w_up, b_up, w_down, b_down, ln2_g, ln2_b):
    depth = w_in.shape[0]
    assert x_sample.shape[1] == 1, "the sample pass handles one new token per sequence"
    weights = (w_in, b_igate, b_fgate, lambda_q1, lambda_k1, lambda_q2, lambda_k2, subln_g, mh_norm_g,
               w_ba, w_bm, w_o, ln1_g, ln1_b, w_router, b_router, w_gate, b_gate, w_up, b_up,
               w_down, b_down, ln2_g, ln2_b)
    y_p, y_s = x_prompt, x_sample
    acc_p = [[] for _ in range(5)]
    acc_s = [[] for _ in range(5)]
    for l in range(depth):
        outs_p, outs_s = _layer(l, depth, y_p, y_s, cache_k, cache_v, state_c[l], state_n[l],
                                state_m[l], page_table, tuple(a[l] for a in weights))
        y_p, y_s = outs_p[0], outs_s[0]
        for i in range(5):
            acc_p[i].append(outs_p[1 + i])
            acc_s[i].append(outs_s[1 + i])
    return (y_p, y_s, *(jnp.stack(a) for a in acc_p), *(jnp.stack(a) for a in acc_s))
```

```python
import functools
import math

import jax
import jax.numpy as jnp
from jax import lax
from jax.experimental import pallas as pl
from jax.experimental.pallas import tpu as pltpu

F32 = jnp.float32
BF16 = jnp.bfloat16
HIGHEST = lax.Precision.HIGHEST

H_A = 4
D_HA = 64
ROT_DIM = D_HA // 4
ROPE_THETA = 500000.0
H_M = 4
DK_M = 64
DV_M = 128
GATE_SOFTCAP = 15.0
N_EXPERTS = 32
TOP_K = 4
SWIGLU_LIMIT = 7.0
SWIGLU_ALPHA = 1.702
LN_EPS = 1e-5
PAGE_SIZE = 128

QA_W = H_A * 2 * D_HA
QM_W = H_M * DK_M
VM_W = H_M * DV_M
HEAD_W = 2 * D_HA
N_GATE = 2 * H_M
W_A = 3 * QA_W + 2 * QM_W + VM_W

LANES = 128
VMEM_LIMIT = 56 * 1024 * 1024

ATTN_TQ = 256
ATTN_TK = 512
ATTN_HEADS_PER_STEP = 4
MLSTM_SEQS_PER_STEP = 2
MOE_ROWS = 128
MOE_GROUP = 7 * LANES
MOE_NSUB = 2
MOE_SCATTER_EXPERTS = 8

NT_DIMS = (((1,), (1,)), ((), ()))
TN_DIMS = (((0,), (0,)), ((), ()))


def _params(*sem):
    return pltpu.CompilerParams(dimension_semantics=sem, vmem_limit_bytes=VMEM_LIMIT)


def _softcap(x):
    return GATE_SOFTCAP * jnp.tanh(x / GATE_SOFTCAP)


def _log_sigmoid(x):
    return jnp.minimum(x, 0.0) - jnp.log1p(jnp.exp(-jnp.abs(x)))


def _sigmoid(x):
    return 1.0 / (1.0 + jnp.exp(-x))


def _split_bf16(x):
    hi = x.astype(BF16)
    lo = (x - hi.astype(F32)).astype(BF16)
    return hi, lo


def _stack_split(x):
    hi, lo = _split_bf16(x)
    return jnp.concatenate([hi, lo], axis=0)


def _fold_split(t):
    n = t.shape[0] // 2
    return t[:n] + t[n:]


def _layer_norm(x, g, b):
    mu = jnp.mean(x, axis=-1, keepdims=True)
    xc = x - mu
    var = jnp.mean(xc * xc, axis=-1, keepdims=True)
    return xc * lax.rsqrt(var + LN_EPS) * g + b


def _rope(t, ra, rb, rc):
    return t * ra + pltpu.roll(t, 8, 1) * rb + pltpu.roll(t, HEAD_W - 8, 1) * rc


def _rope_tables(pos):
    inv = ROPE_THETA ** (-jnp.arange(0, ROT_DIM, 2, dtype=F32) / ROT_DIM)
    ang = pos.astype(F32)[:, None] * inv[None, :]
    cos, sin = jnp.cos(ang), jnp.sin(ang)
    n = pos.shape[0]
    half = ROT_DIM // 2
    rest = D_HA - ROT_DIM
    a = jnp.concatenate([cos, cos, jnp.ones((n, rest), F32)], axis=1)
    b = jnp.concatenate([jnp.zeros((n, half), F32), sin, jnp.zeros((n, rest), F32)], axis=1)
    c = jnp.concatenate([-sin, jnp.zeros((n, half + rest), F32)], axis=1)
    return tuple(jnp.tile(t, (1, 2)) for t in (a, b, c))


def _diff_lambda(lam_ref, lam_init):
    lp = lam_ref[...]
    s1 = jnp.sum(lp[0:1] * lp[1:2], axis=1, keepdims=True)
    s2 = jnp.sum(lp[2:3] * lp[3:4], axis=1, keepdims=True)
    return jnp.exp(s1) - jnp.exp(s2) + lam_init


def _proj_prompt_kernel(x_ref, wa_ref, wgc_ref, wgr_ref, wr_ref, ra_ref, rb_ref, rc_ref,
                        q_ref, kf_ref, kb_ref, vf_ref, vb_ref, qm_ref, km_ref, vm_ref,
                        gcol_ref, grow_ref, og_ref, ga_ref, gm_ref):
    x = x_ref[...].astype(BF16)
    tm = x.shape[0]
    za = jnp.dot(x, wa_ref[...], preferred_element_type=F32)
    ra, rb, rc = ra_ref[...], rb_ref[...], rc_ref[...]
    for h in range(H_A):
        sl = slice(h * HEAD_W, (h + 1) * HEAD_W)
        qh = _rope(za[:, sl], ra, rb, rc)
        q_ref[:, sl] = (qh * (D_HA ** -0.5)).astype(BF16)
        kh = _rope(za[:, QA_W + h * HEAD_W:QA_W + (h + 1) * HEAD_W], ra, rb, rc)
        kf_ref[pl.ds(h, tm, stride=H_A), :] = kh
        kb_ref[:, sl] = kh.astype(BF16)
        vf_ref[pl.ds(h, tm, stride=H_A), :] = za[:, 2 * QA_W + h * HEAD_W:2 * QA_W + (h + 1) * HEAD_W]
    vb_ref[...] = za[:, 2 * QA_W:3 * QA_W].astype(BF16)
    o = 3 * QA_W
    qm_ref[...] = za[:, o:o + QM_W].astype(BF16)
    km_ref[...] = (za[:, o + QM_W:o + 2 * QM_W] * (DK_M ** -0.5)).astype(BF16)
    vm_ref[...] = za[:, o + 2 * QM_W:].astype(BF16)
    gcol_ref[...] = jnp.dot(x, wgc_ref[...], preferred_element_type=F32)
    grow_ref[...] = lax.dot_general(wgr_ref[...], x, NT_DIMS, preferred_element_type=F32)
    zr = jnp.dot(x, wr_ref[...], preferred_element_type=F32)
    d = ga_ref.shape[1]
    og_ref[...] = zr[:, :VM_W].astype(BF16)
    ga_ref[...] = zr[:, VM_W:VM_W + d].astype(BF16)
    gm_ref[...] = zr[:, VM_W + d:].astype(BF16)


def _proj_prompt(x2, wa, wgc, wgr, wr, tabs, seq, tm):
    t, d = x2.shape
    nrep = seq // tm
    row = lambda w: pl.BlockSpec((tm, w), lambda i: (i, 0))
    full = lambda a: pl.BlockSpec(a.shape, lambda i: (0,) * a.ndim)
    tab = pl.BlockSpec((tm, HEAD_W), lambda i: (i % nrep, 0))
    sds = lambda w, dt: jax.ShapeDtypeStruct((t, w), dt)
    kv_spec = pl.BlockSpec((tm * H_A, HEAD_W), lambda i: (i, 0))
    kv_sds = jax.ShapeDtypeStruct((t * H_A, HEAD_W), F32)
    return pl.pallas_call(
        _proj_prompt_kernel,
        grid=(t // tm,),
        in_specs=[row(d), full(wa), full(wgc), full(wgr), full(wr), tab, tab, tab],
        out_specs=[row(QA_W), kv_spec, row(QA_W), kv_spec, row(QA_W),
                   row(QM_W), row(QM_W), row(VM_W), row(LANES),
                   pl.BlockSpec((N_GATE, tm), lambda i: (0, i)),
                   row(VM_W), row(d), row(d)],
        out_shape=[sds(QA_W, BF16), kv_sds, sds(QA_W, BF16), kv_sds, sds(QA_W, BF16),
                   sds(QM_W, BF16), sds(QM_W, BF16), sds(VM_W, BF16), sds(LANES, F32),
                   jax.ShapeDtypeStruct((N_GATE, t), F32),
                   sds(VM_W, BF16), sds(d, BF16), sds(d, BF16)],
        compiler_params=_params("parallel"),
        name="proj_prompt",
    )(x2, wa, wgc, wgr, wr, *tabs)


def _attn_prompt_kernel(q_ref, k_ref, v_ref, lam_ref, g_ref, o_ref, *, tq, tk, hps, lam_init):
    qi = pl.program_id(2)
    nfull = (qi * tq) // tk
    lane = lax.broadcasted_iota(jnp.int32, (1, HEAD_W), 1)
    r = lax.broadcasted_iota(jnp.int32, (tq, tk), 0) + qi * tq
    c = lax.broadcasted_iota(jnp.int32, (tq, tk), 1) + nfull * tk
    causal = jnp.concatenate([c <= r, c <= r], axis=0)
    ones_blk = jnp.broadcast_to(jnp.where(lane == 0, 1.0, 0.0).astype(BF16), (tk, HEAD_W))
    qqs = []
    for hh in range(hps):
        q = q_ref[:, hh * HEAD_W:(hh + 1) * HEAD_W]
        zero = jnp.zeros_like(q)
        qqs.append(jnp.concatenate([jnp.where(lane < D_HA, q, zero), jnp.where(lane >= D_HA, q, zero)],
                                   axis=0))

    def step(j, carry, masked):
        off = pl.multiple_of(j * tk, tk)
        out = []
        for hh in range(hps):
            m, acc = carry[hh]
            sl = slice(hh * HEAD_W, (hh + 1) * HEAD_W)
            k = k_ref[pl.ds(off, tk), sl]
            vext = jnp.concatenate([v_ref[pl.ds(off, tk), sl], ones_blk], axis=1)
            s = lax.dot_general(qqs[hh], k, NT_DIMS, preferred_element_type=F32)
            if masked:
                s = jnp.where(causal, s, -jnp.inf)
            m_new = jnp.maximum(m, jnp.max(s, axis=1, keepdims=True))
            p = jnp.exp(s - m_new)
            acc = jnp.exp(m - m_new) * acc + jnp.dot(p.astype(BF16), vext, preferred_element_type=F32)
            out.append((m_new, acc))
        return tuple(out)

    init = tuple((jnp.full((2 * tq, 1), -jnp.inf, F32), jnp.zeros((2 * tq, 2 * HEAD_W), F32))
                 for _ in range(hps))
    carry = lax.fori_loop(0, nfull, lambda j, cr: step(j, cr, False), init)
    carry = step(nfull, carry, True)
    lam = _diff_lambda(lam_ref, lam_init)
    for hh in range(hps):
        acc = carry[hh][1]
        o1 = acc[:tq, :HEAD_W] / acc[:tq, HEAD_W:HEAD_W + 1]
        o2 = acc[tq:, :HEAD_W] / acc[tq:, HEAD_W:HEAD_W + 1]
        o = o1 - lam * o2
        y = o * lax.rsqrt(jnp.mean(o * o, axis=1, keepdims=True) + LN_EPS) * g_ref[...] * (1.0 - lam_init)
        o_ref[:, hh * HEAD_W:(hh + 1) * HEAD_W] = y.astype(o_ref.dtype)


def _attn_prompt(q, k, v, lam_p, subln_g, batch, seq, lam_init):
    t = q.shape[0]
    tk = _pick_tile(seq, ATTN_TK)
    tq = _pick_tile(tk, ATTN_TQ)
    nq = seq // tq
    hps = ATTN_HEADS_PER_STEP
    w = hps * HEAD_W
    return pl.pallas_call(
        functools.partial(_attn_prompt_kernel, tq=tq, tk=tk, hps=hps, lam_init=lam_init),
        grid=(batch, H_A // hps, nq),
        in_specs=[pl.BlockSpec((tq, w), lambda b, h, i: (b * nq + i, h)),
                  pl.BlockSpec((seq, w), lambda b, h, i: (b, h)),
                  pl.BlockSpec((seq, w), lambda b, h, i: (b, h)),
                  pl.BlockSpec(lam_p.shape, lambda b, h, i: (0, 0)),
                  pl.BlockSpec((1, HEAD_W), lambda b, h, i: (0, 0))],
        out_specs=pl.BlockSpec((tq, w), lambda b, h, i: (b * nq + i, h)),
        out_shape=jax.ShapeDtypeStruct((t, QA_W), BF16),
        compiler_params=_params("parallel", "parallel", "parallel"),
        name="attn_prompt",
    )(q, k, v, lam_p, subln_g)


def _mlstm_prompt_kernel(q_ref, k_ref, v_ref, gcol_ref, grow_ref, bcol_ref, brow_ref, og_ref, g_ref,
                         min_ref, cn_ref, m_ref, *, lc, bps):
    ci = pl.program_id(1)

    @pl.when(ci == 0)
    def _():
        cn_ref[...] = jnp.zeros_like(cn_ref)
        m_ref[...] = jnp.zeros_like(m_ref)

    for i in range(bps):
        _mlstm_chunk(q_ref.at[0, i], k_ref.at[0, i], v_ref.at[0, i], gcol_ref.at[0, i], grow_ref[:, 0, i, :],
                     bcol_ref, brow_ref, og_ref.at[0, i], g_ref, min_ref.at[0, i], cn_ref.at[0, i],
                     m_ref.at[0, i], lc)


def _mlstm_chunk(q_ref, k_ref, v_ref, gcol_ref, grow, bcol_ref, brow_ref, og_ref, g_ref,
                 min_ref, cn_ref, m_ref, lc):
    gc = _softcap(gcol_ref[...] + bcol_ref[...])
    lfc = _log_sigmoid(gc)
    gr = _softcap(grow + brow_ref[...])
    lfr = _log_sigmoid(gr)
    ri = lax.broadcasted_iota(jnp.int32, (lc, lc), 0)
    cj = lax.broadcasted_iota(jnp.int32, (lc, lc), 1)
    tril = cj <= ri
    b_col = jnp.dot(tril.astype(F32), lfc, precision=HIGHEST, preferred_element_type=F32)
    b_row = jnp.dot(lfr, (ri <= cj).astype(F32), precision=HIGHEST, preferred_element_type=F32)

    cn = cn_ref[...]
    cn_bf = cn.astype(BF16)
    q = q_ref[...]
    k = k_ref[...]
    v = v_ref[...]
    m_all = m_ref[...]
    lane_q = lax.broadcasted_iota(jnp.int32, (1, QM_W), 1) // DK_M
    row_q = lax.broadcasted_iota(jnp.int32, (QM_W, 1), 0) // DK_M
    lane_v = lax.broadcasted_iota(jnp.int32, (1, LANES), 1)
    ones_blk = jnp.broadcast_to(jnp.where(lane_v == 0, 1.0, 0.0).astype(BF16), (lc, LANES))
    new_cn = jnp.zeros_like(cn)
    new_m = m_all
    for h in range(H_M):
        sl = slice(h * DV_M, (h + 1) * DV_M)
        qh = jnp.where(lane_q == h, q, jnp.zeros_like(q))
        kh = jnp.where(lane_q == h, k, jnp.zeros_like(k))
        b_c = b_col[:, H_M + h:H_M + h + 1]
        ig_c = gc[:, h:h + 1]
        b_r = b_row[H_M + h:H_M + h + 1, :]
        ig_r = gr[h:h + 1, :]
        m_prev = m_all[:, h:h + 1]
        dmat = jnp.where(tril, b_c - b_r + ig_r, -jnp.inf)
        m_inter = b_c + m_prev
        m_t = jnp.maximum(m_inter, jnp.max(dmat, axis=1, keepdims=True))
        w_inter = jnp.exp(m_inter - m_t)
        s = lax.dot_general(qh, kh, NT_DIMS, preferred_element_type=F32) * jnp.exp(dmat - m_t)
        vext = jnp.concatenate([v[:, sl], ones_blk], axis=1)
        nd = (w_inter * jnp.dot(qh, cn_bf, preferred_element_type=F32)
              + jnp.dot(s.astype(BF16), vext, preferred_element_type=F32))
        num = nd[:, :DV_M]
        den = nd[:, DV_M:DV_M + 1]
        hout = num / jnp.maximum(jnp.abs(den), jnp.exp(-m_t))
        mu = jnp.mean(hout, axis=1, keepdims=True)
        hc = hout - mu
        var = jnp.mean(hc * hc, axis=1, keepdims=True)
        y = hc * lax.rsqrt(var + LN_EPS) * g_ref[:, sl] * _sigmoid(og_ref[:, sl].astype(F32))
        min_ref[:, sl] = y.astype(min_ref.dtype)
        m_new = m_t[lc - 1:lc, :]
        b_last = b_c[lc - 1:lc, :]
        decay = jnp.exp(b_last + m_prev - m_new)
        wk = jnp.exp(b_last - b_c + ig_c - m_new)
        upd = lax.dot_general(kh, (wk * vext.astype(F32)).astype(BF16), TN_DIMS,
                              preferred_element_type=F32)
        new_cn = new_cn + jnp.where(row_q == h, decay * cn, 0.0) + upd
        new_m = jnp.where(lane_v == h, m_new, new_m)
    cn_ref[...] = new_cn
    m_ref[...] = new_m


def _mlstm_prompt(qm, km, vm, gcol, grow, bcol, brow, og, mh_g, batch, seq, lc):
    t = qm.shape[0]
    bps = MLSTM_SEQS_PER_STEP if batch % MLSTM_SEQS_PER_STEP == 0 else 1
    nb = batch // bps
    row = lambda w: pl.BlockSpec((1, bps, lc, w), lambda b, c: (b, 0, c, 0))
    const = lambda a: pl.BlockSpec(a.shape, lambda b, c: (0,) * a.ndim)
    seqs = lambda a: a.reshape(nb, bps, seq, a.shape[1])
    cw = 2 * LANES
    m_in, cn, m = pl.pallas_call(
        functools.partial(_mlstm_prompt_kernel, lc=lc, bps=bps),
        grid=(nb, seq // lc),
        in_specs=[row(QM_W), row(QM_W), row(VM_W), row(LANES),
                  pl.BlockSpec((N_GATE, 1, bps, lc), lambda b, c: (0, b, 0, c)),
                  const(bcol), const(brow), row(VM_W), const(mh_g)],
        out_specs=[row(VM_W),
                   pl.BlockSpec((1, bps, QM_W, cw), lambda b, c: (b, 0, 0, 0)),
                   pl.BlockSpec((1, bps, 1, LANES), lambda b, c: (b, 0, 0, 0))],
        out_shape=[jax.ShapeDtypeStruct((nb, bps, seq, VM_W), BF16),
                   jax.ShapeDtypeStruct((nb, bps, QM_W, cw), F32),
                   jax.ShapeDtypeStruct((nb, bps, 1, LANES), F32)],
        compiler_params=_params("parallel", "arbitrary"),
        name="mlstm_prompt",
    )(seqs(qm), seqs(km), seqs(vm), seqs(gcol), grow.reshape(N_GATE, nb, bps, seq), bcol, brow,
      seqs(og), mh_g)
    return m_in.reshape(t, VM_W), cn.reshape(batch, QM_W, cw), m.reshape(batch, 1, LANES)


def _mix_kernel(a_ref, mi_ref, ga_ref, gm_ref, x_ref, wba_ref, wbm_ref, wo_ref, g1_ref, b1_ref,
                wrt_ref, brt_ref, h_ref, hb_ref, comb_ref, mask_ref, *, alpha, precise):
    def mm(a, w_ref):
        if precise:
            return jnp.dot(a.astype(F32), w_ref[...], precision=HIGHEST, preferred_element_type=F32)
        return jnp.dot(a.astype(BF16), w_ref[...], preferred_element_type=F32)

    a_br = mm(a_ref[...], wba_ref)
    m_br = mm(mi_ref[...], wbm_ref)
    merged = _sigmoid(ga_ref[...].astype(F32)) * a_br + _sigmoid(gm_ref[...].astype(F32)) * m_br
    mix = mm(merged, wo_ref)
    h = _layer_norm(alpha * x_ref[...] + mix, g1_ref[...], b1_ref[...])
    h_ref[...] = h
    hb_ref[...] = h.astype(BF16)

    ne = wrt_ref.shape[0]
    if precise:
        logits = lax.dot_general(wrt_ref[...], h, NT_DIMS, precision=HIGHEST, preferred_element_type=F32)
    else:
        ws = _stack_split(wrt_ref[...])
        h_hi, h_lo = _split_bf16(h)
        logits = (_fold_split(lax.dot_general(ws, h_hi, NT_DIMS, preferred_element_type=F32))
                  + lax.dot_general(ws[:ne], h_lo, NT_DIMS, preferred_element_type=F32))
    logits = logits + brt_ref[...]
    tm = logits.shape[1]
    eidx = lax.broadcasted_iota(jnp.int32, (ne, tm), 0)
    work = logits
    sel = jnp.zeros((ne, tm), jnp.bool_)
    top = None
    for _ in range(TOP_K):
        mx = jnp.max(work, axis=0, keepdims=True)
        top = mx if top is None else top
        first = jnp.min(jnp.where(work == mx, eidx, ne), axis=0, keepdims=True)
        pick = eidx == first
        sel = jnp.logical_or(sel, pick)
        work = jnp.where(pick, -jnp.inf, work)
    ex = jnp.where(sel, jnp.exp(logits - top), 0.0)
    comb_ref[...] = ex / jnp.sum(ex, axis=0, keepdims=True)
    mask_ref[...] = jnp.where(sel, 1.0, 0.0)


def _mix(a_in, m_in, ga, gm, x2, wba, wbm, wo, g1, b1, wrt, brt, alpha, precise, tm):
    t, d = x2.shape
    ne = wrt.shape[0]
    row = lambda w: pl.BlockSpec((tm, w), lambda i: (i, 0))
    col = pl.BlockSpec((ne, tm), lambda i: (0, i))
    const = lambda a: pl.BlockSpec(a.shape, lambda i: (0,) * a.ndim)
    return pl.pallas_call(
        functools.partial(_mix_kernel, alpha=alpha, precise=precise),
        grid=(t // tm,),
        in_specs=[row(QA_W), row(VM_W), row(d), row(d), row(d), const(wba), const(wbm), const(wo),
                  const(g1), const(b1), const(wrt), const(brt)],
        out_specs=[row(d), row(d), col, col],
        out_shape=[jax.ShapeDtypeStruct((t, d), F32), jax.ShapeDtypeStruct((t, d), BF16),
                   jax.ShapeDtypeStruct((ne, t), F32), jax.ShapeDtypeStruct((ne, t), F32)],
        compiler_params=_params("parallel"),
        name="mix_sample" if precise else "mix_prompt",
    )(a_in, m_in, ga, gm, x2, wba, wbm, wo, g1, b1, wrt, brt)


def _route_kernel(mask_ref, pos_ref, cnt_ref):
    maskf = mask_ref[...]
    g = maskf.shape[1]
    r = lax.broadcasted_iota(jnp.int32, (g, g), 0)
    c = lax.broadcasted_iota(jnp.int32, (g, g), 1)
    before = jnp.dot(maskf.astype(BF16), (r < c).astype(BF16), preferred_element_type=F32)
    pos_ref[...] = jnp.where(maskf > 0.0, before, -1.0)
    cnt_ref[0] = jnp.broadcast_to(jnp.sum(maskf, axis=1, keepdims=True), cnt_ref.shape[1:])


def _route(mask, group):
    ne, t = mask.shape
    ng = t // group
    return pl.pallas_call(
        _route_kernel,
        grid=(ng,),
        in_specs=[pl.BlockSpec((ne, group), lambda g: (0, g))],
        out_specs=[pl.BlockSpec((ne, group), lambda g: (0, g)),
                   pl.BlockSpec((1, ne, LANES), lambda g: (g, 0, 0))],
        out_shape=[jax.ShapeDtypeStruct((ne, t), F32), jax.ShapeDtypeStruct((ng, ne, LANES), F32)],
        compiler_params=_params("parallel"),
        name="route",
    )(mask)


def _moe_kernel(cnt_ref, h_ref, pos_ref, comb_ref, wg_ref, bg_ref, wu_ref, bu_ref,
                wd_ref, bd_ref, o_ref, y_scr, *, rows, group, nsub):
    sg = pl.program_id(0)
    e = pl.program_id(1)
    ne = pl.num_programs(1)
    n_exp = pos_ref.shape[0]

    @pl.when(e == 0)
    def _():
        o_ref[...] = jnp.zeros_like(o_ref)

    slot0 = lax.broadcasted_iota(jnp.int32, (rows, 1), 0).astype(F32)

    def ffn(hit, tok):
        xg = jnp.dot(jnp.where(hit, 1.0, 0.0).astype(BF16), h_ref[tok, :],
                     preferred_element_type=F32).astype(BF16)
        gate = jnp.minimum(jnp.dot(xg, wg_ref[0], preferred_element_type=F32) + bg_ref[0], SWIGLU_LIMIT)
        up = jnp.clip(jnp.dot(xg, wu_ref[0], preferred_element_type=F32) + bu_ref[0],
                      -SWIGLU_LIMIT, SWIGLU_LIMIT)
        hid = gate * _sigmoid(SWIGLU_ALPHA * gate) * (up + 1.0)
        return (jnp.dot(hid.astype(BF16), wd_ref[0], preferred_element_type=F32) + bd_ref[0]).astype(BF16)

    for sub in range(nsub):
        tok = slice(sub * group, (sub + 1) * group)
        y_scr[sub, pl.ds(pl.multiple_of(e * rows, rows), rows), :] = ffn(pos_ref[pl.ds(e, 1), tok] == slot0, tok)

    for sub in range(nsub):
        tok = slice(sub * group, (sub + 1) * group)
        n = cnt_ref[(sg * nsub + sub) * ne + e]
        pos = pos_ref[pl.ds(e, 1), tok]
        cw = comb_ref[pl.ds(e, 1), tok]

        def tile(s, carry, tok=tok, pos=pos, cw=cw):
            hit = pos == slot0 + (s * rows).astype(F32)
            gw = jnp.where(hit, cw, 0.0).astype(BF16)
            o_ref[tok, :] += lax.dot_general(gw, ffn(hit, tok), TN_DIMS, preferred_element_type=F32)
            return carry

        lax.fori_loop(1, (n + rows - 1) // rows, tile, 0)

    @pl.when(e == ne - 1)
    def _():
        for sub in range(nsub):
            tok = slice(sub * group, (sub + 1) * group)
            for c0 in range(0, n_exp, MOE_SCATTER_EXPERTS):
                gw = jnp.concatenate(
                    [jnp.where(pos_ref[ee:ee + 1, tok] == slot0, comb_ref[ee:ee + 1, tok], 0.0).astype(BF16)
                     for ee in range(c0, c0 + MOE_SCATTER_EXPERTS)], axis=0)
                ys = y_scr[sub, c0 * rows:(c0 + MOE_SCATTER_EXPERTS) * rows, :]
                o_ref[tok, :] += lax.dot_general(gw, ys, TN_DIMS, preferred_element_type=F32)


def _moe(counts, hb, pos, comb, wg, bg, wu, bu, wd, bd, group, nsub, rows):
    t, d = hb.shape
    ne = wg.shape[0]
    sgroup = group * nsub
    tok = pl.BlockSpec((ne, sgroup), lambda g, e, c: (0, g))
    wspec = lambda a: pl.BlockSpec((1,) + a.shape[1:], lambda g, e, c: (e, 0, 0))
    grid_spec = pltpu.PrefetchScalarGridSpec(
        num_scalar_prefetch=1,
        grid=(t // sgroup, ne),
        in_specs=[pl.BlockSpec((sgroup, d), lambda g, e, c: (g, 0)), tok, tok,
                  wspec(wg), wspec(bg), wspec(wu), wspec(bu), wspec(wd), wspec(bd)],
        out_specs=pl.BlockSpec((sgroup, d), lambda g, e, c: (g, 0)),
        scratch_shapes=[pltpu.VMEM((nsub, ne * rows, d), BF16)],
    )
    return pl.pallas_call(
        functools.partial(_moe_kernel, rows=rows, group=group, nsub=nsub),
        grid_spec=grid_spec,
        out_shape=jax.ShapeDtypeStruct((t, d), F32),
        compiler_params=_params("parallel", "arbitrary"),
        name="moe",
    )(counts, hb, pos, comb, wg, bg, wu, bu, wd, bd)


def _ln2_kernel(h_ref, ff_ref, g_ref, b_ref, y_ref, *, alpha):
    y_ref[...] = _layer_norm(alpha * h_ref[...] + ff_ref[...], g_ref[...], b_ref[...])


def _ln2(h, ff, g2, b2, alpha, tm, ff_block0):
    t, d = h.shape
    const = lambda a: pl.BlockSpec(a.shape, lambda i: (0,) * a.ndim)
    return pl.pallas_call(
        functools.partial(_ln2_kernel, alpha=alpha),
        grid=(t // tm,),
        in_specs=[pl.BlockSpec((tm, d), lambda i: (i, 0)),
                  pl.BlockSpec((tm, d), lambda i: (ff_block0 + i, 0)), const(g2), const(b2)],
        out_specs=pl.BlockSpec((tm, d), lambda i: (i, 0)),
        out_shape=jax.ShapeDtypeStruct((t, d), F32),
        compiler_params=_params("parallel"),
        name="ln2",
    )(h, ff, g2, b2)


def _proj_sample_kernel(x_ref, w_ref, z_ref):
    z_ref[...] = jnp.dot(x_ref[...], w_ref[...], precision=HIGHEST, preferred_element_type=F32)


def _proj_sample(xs, w, chunk):
    n, d = xs.shape
    width = w.shape[1]
    return pl.pallas_call(
        _proj_sample_kernel,
        grid=(width // chunk,),
        in_specs=[pl.BlockSpec((n, d), lambda j: (0, 0)), pl.BlockSpec((d, chunk), lambda j: (0, j))],
        out_specs=pl.BlockSpec((n, chunk), lambda j: (0, j)),
        out_shape=jax.ShapeDtypeStruct((n, width), F32),
        compiler_params=_params("parallel"),
        name="proj_sample",
    )(xs, w)


def _paged_attn_kernel(pt_ref, z_ref, ra_ref, rb_ref, rc_ref, lam_ref, g_ref, *rest,
                       ppb, lam_init):
    k_refs = rest[:ppb]
    v_refs = rest[ppb:2 * ppb]
    kout_ref, vout_ref, a_ref, q_scr, qs_scr, m_scr, l_scr, acc_scr = rest[2 * ppb:]
    j = pl.program_id(1)
    nj = pl.num_programs(1)
    nrow = 2 * H_A
    prow = PAGE_SIZE * H_A
    ra, rb, rc = ra_ref[...], rb_ref[...], rc_ref[...]
    lane = lax.broadcasted_iota(jnp.int32, (1, HEAD_W), 1)

    def per_row(t):
        return jnp.concatenate([t[:, (r // 2) * HEAD_W:(r // 2 + 1) * HEAD_W] for r in range(nrow)], axis=0)

    @pl.when(j == 0)
    def _():
        z = z_ref[0]
        rows = []
        for h in range(H_A):
            qh = _rope(z[:, h * HEAD_W:(h + 1) * HEAD_W], ra, rb, rc) * (D_HA ** -0.5)
            rows += [jnp.where(lane < D_HA, qh, 0.0), jnp.where(lane >= D_HA, qh, 0.0)]
        q = jnp.concatenate(rows, axis=0)
        q_scr[...] = q
        qs_scr[...] = _stack_split(q)
        m_scr[...] = jnp.full_like(m_scr, -jnp.inf)
        l_scr[...] = jnp.zeros_like(l_scr)
        acc_scr[...] = jnp.zeros_like(acc_scr)

    qs = qs_scr[...]
    parts = []
    for kr in k_refs:
        k_hi, k_lo = _split_bf16(kr[0])
        t = (lax.dot_general(qs, k_hi, NT_DIMS, preferred_element_type=F32)
             + lax.dot_general(qs, k_lo, NT_DIMS, preferred_element_type=F32))
        parts.append(_fold_split(t))
    s = jnp.concatenate(parts, axis=1)
    r_head = lax.broadcasted_iota(jnp.int32, s.shape, 0) // 2
    c_head = lax.broadcasted_iota(jnp.int32, s.shape, 1) % H_A
    s = jnp.where(r_head == c_head, s, -jnp.inf)
    m_old = m_scr[...]
    m_new = jnp.maximum(m_old, jnp.max(s, axis=1, keepdims=True))
    alpha = jnp.exp(m_old - m_new)
    p = jnp.exp(s - m_new)
    l_scr[...] = alpha * l_scr[...] + jnp.sum(p, axis=1, keepdims=True)
    ps = _stack_split(p)
    pv = None
    for i, vr in enumerate(v_refs):
        v_hi, v_lo = _split_bf16(vr[0])
        pi = ps[:, i * prow:(i + 1) * prow]
        t = jnp.dot(pi, v_hi, preferred_element_type=F32) + jnp.dot(pi, v_lo, preferred_element_type=F32)
        pv = t if pv is None else pv + t
    acc_scr[...] = alpha * acc_scr[...] + _fold_split(pv)
    m_scr[...] = m_new

    @pl.when(j == nj - 1)
    def _():
        z = z_ref[0]
        k_new = jnp.concatenate([_rope(z[:, QA_W + h * HEAD_W:QA_W + (h + 1) * HEAD_W], ra, rb, rc)
                                 for h in range(H_A)], axis=1)
        v_new = z[:, 2 * QA_W:3 * QA_W]
        kout_ref[0] = k_new
        vout_ref[0] = v_new
        s_self = jnp.sum(q_scr[...] * per_row(k_new), axis=1, keepdims=True)
        m_old2 = m_scr[...]
        m_fin = jnp.maximum(m_old2, s_self)
        a2 = jnp.exp(m_old2 - m_fin)
        p_self = jnp.exp(s_self - m_fin)
        l_fin = a2 * l_scr[...] + p_self
        o_all = (a2 * acc_scr[...] + p_self * per_row(v_new)) / l_fin
        lam = _diff_lambda(lam_ref, lam_init)
        outs = []
        for h in range(H_A):
            o = o_all[2 * h:2 * h + 1] - lam * o_all[2 * h + 1:2 * h + 2]
            outs.append(o * lax.rsqrt(jnp.mean(o * o, axis=1, keepdims=True) + LN_EPS)
                        * g_ref[...] * (1.0 - lam_init))
        a_ref[0] = jnp.concatenate(outs, axis=1)


def _paged_attn(page_table, z3, tabs, lam_p, subln_g, ck, cv, ppb, lam_init):
    nb, npages = page_table.shape
    width = z3.shape[2]
    nj = npages // ppb
    pt = page_table.reshape(-1)
    prow = PAGE_SIZE * H_A
    nrow = 2 * H_A
    const = lambda a: pl.BlockSpec(a.shape, lambda b, j, p: (0,) * a.ndim)
    page = lambda i: pl.BlockSpec((1, prow, HEAD_W),
                                  lambda b, j, p: (p[b * npages + j * ppb + i], 0, 0))
    seq3 = lambda w: pl.BlockSpec((1, 1, w), lambda b, j, p: (b, 0, 0))
    grid_spec = pltpu.PrefetchScalarGridSpec(
        num_scalar_prefetch=1,
        grid=(nb, nj),
        in_specs=[seq3(width), const(tabs[0]), const(tabs[1]), const(tabs[2]), const(lam_p),
                  const(subln_g)] + [page(i) for i in range(ppb)] + [page(i) for i in range(ppb)],
        out_specs=[seq3(QA_W), seq3(QA_W), seq3(QA_W)],
        scratch_shapes=[pltpu.VMEM((nrow, HEAD_W), F32), pltpu.VMEM((2 * nrow, HEAD_W), BF16),
                        pltpu.VMEM((nrow, 1), F32), pltpu.VMEM((nrow, 1), F32),
                        pltpu.VMEM((nrow, HEAD_W), F32)],
    )
    sds = jax.ShapeDtypeStruct((nb, 1, QA_W), F32)
    return pl.pallas_call(
        functools.partial(_paged_attn_kernel, ppb=ppb, lam_init=lam_init),
        grid_spec=grid_spec,
        out_shape=[sds, sds, sds],
        compiler_params=_params("parallel", "arbitrary"),
        name="paged_attn",
    )(pt, z3, *tabs, lam_p, subln_g, *([ck] * ppb), *([cv] * ppb))


def _mlstm_step_kernel(gates_ref, m0_ref, big_ref, bfg_ref, qc_ref, kc_ref, v_ref, og_ref, c0_ref,
                       n0_ref, g_ref, min_ref, c_ref, n_ref, m_ref):
    b = pl.program_id(0)
    outs = []
    for h in range(H_M):
        sl = slice(h * DV_M, (h + 1) * DV_M)
        ig = _softcap(jnp.full((1, LANES), gates_ref[b, h] + big_ref[h], F32))
        lf = _log_sigmoid(_softcap(jnp.full((1, LANES), gates_ref[b, H_M + h] + bfg_ref[h], F32)))
        m0 = jnp.full((1, LANES), m0_ref[b, h], F32)
        m_inter = lf + m0
        m_t = jnp.maximum(m_inter, ig)
        w_inter = jnp.exp(m_inter - m_t)
        w_new = jnp.exp(ig - m_t)
        qc = qc_ref[0, h]
        kc = kc_ref[0, h] * (DK_M ** -0.5)
        vr = v_ref[0][:, sl]
        c0 = c0_ref[0, h]
        n0 = n0_ref[0, h]
        qk = jnp.sum(qc * kc, axis=0, keepdims=True)
        s = qk * w_new
        num = w_inter * jnp.sum(qc * c0, axis=0, keepdims=True) + s * vr
        den = w_inter * jnp.sum(qc * n0, axis=0, keepdims=True) + s
        hout = num / jnp.maximum(jnp.abs(den), jnp.exp(-m_t))
        mu = jnp.mean(hout, axis=1, keepdims=True)
        hc = hout - mu
        var = jnp.mean(hc * hc, axis=1, keepdims=True)
        outs.append(hc * lax.rsqrt(var + LN_EPS) * g_ref[:, sl] * _sigmoid(og_ref[0][:, sl]))
        c_ref[0, h] = w_inter * c0 + (w_new * kc) * vr
        n_ref[0, h] = w_inter[:, 0:1] * n0 + w_new[:, 0:1] * kc
        m_ref[0, h:h + 1, :] = m_t
    min_ref[0] = jnp.concatenate(outs, axis=1)


def _mlstm_step(gates, m0, big, bfg, qc, kc, v3, og3, c0, n0c, mh_g):
    nb = c0.shape[0]
    smem = pl.BlockSpec(memory_space=pltpu.SMEM)
    per = lambda a: pl.BlockSpec((1,) + a.shape[1:], lambda b: (b,) + (0,) * (a.ndim - 1))
    const = lambda a: pl.BlockSpec(a.shape, lambda b: (0,) * a.ndim)
    return pl.pallas_call(
        _mlstm_step_kernel,
        grid=(nb,),
        in_specs=[smem, smem, smem, smem, per(qc), per(kc), per(v3), per(og3), per(c0), per(n0c),
                  const(mh_g)],
        out_specs=[per(v3), per(c0), per(n0c), pl.BlockSpec((1, H_M, LANES), lambda b: (b, 0, 0))],
        out_shape=[jax.ShapeDtypeStruct(v3.shape, F32), jax.ShapeDtypeStruct(c0.shape, F32),
                   jax.ShapeDtypeStruct(n0c.shape, F32),
                   jax.ShapeDtypeStruct((nb, H_M, LANES), F32)],
        compiler_params=_params("parallel"),
        name="mlstm_step",
    )(gates, m0, big, bfg, qc, kc, v3, og3, c0, n0c, mh_g)


def _pick_tile(n, target):
    t = min(n, target)
    while n % t:
        t //= 2
    return t


def _layer(l, depth, x_p, x_s, cache_k, cache_v, c0, n0, m0, page_table, w):
    (w_in, b_ig, b_fg, lq1, lk1, lq2, lk2, subln_g, mh_g, w_ba, w_bm, w_o, ln1_g, ln1_b,
     w_router, b_router, w_gate, b_gate, w_up, b_up, w_down, b_down, ln2_g, ln2_b) = w
    alpha = (2.0 * depth) ** 0.25
    lam_init = 0.8 - 0.6 * math.exp(-0.3 * l)
    bp, seq, d = x_p.shape
    ns = x_s.shape[0]
    t = bp * seq
    ne = w_router.shape[1]
    past = page_table.shape[1] * PAGE_SIZE

    w_a = w_in[:, :W_A]
    w_gt = w_in[:, W_A:W_A + N_GATE]
    w_r = w_in[:, W_A + N_GATE:]
    w_gc = jnp.pad(w_gt, ((0, 0), (0, LANES - N_GATE)))
    lam_p = jnp.stack([lq1, lk1, lq2, lk2])
    sub_g = subln_g.reshape(1, HEAD_W)
    mh_g2 = mh_g.reshape(1, VM_W)
    bcol = jnp.pad(jnp.concatenate([b_ig, b_fg]), (0, LANES - N_GATE)).reshape(1, LANES)
    brow = jnp.concatenate([b_ig, b_fg]).reshape(N_GATE, 1)
    g1, b1 = ln1_g.reshape(1, d), ln1_b.reshape(1, d)
    g2, b2 = ln2_g.reshape(1, d), ln2_b.reshape(1, d)
    wrt = w_router.T
    brt = b_router.reshape(ne, 1)

    x2 = x_p.reshape(t, d)
    tm = _pick_tile(seq, 256)
    tabs_p = _rope_tables(jnp.arange(seq, dtype=jnp.int32))
    (qa, k_f, k_b, v_f, v_b, qm, km, vm, gcol, grow, og, ga, gm) = _proj_prompt(
        x2, w_a.astype(BF16), w_gc.astype(BF16), w_gt.T.astype(BF16), w_r.astype(BF16), tabs_p, seq, tm)
    a_in = _attn_prompt(qa, k_b, v_b, lam_p, sub_g, bp, seq, lam_init)
    m_in, cn_p, m_p = _mlstm_prompt(qm, km, vm, gcol, grow, bcol, brow, og, mh_g2, bp, seq,
                                    _pick_tile(seq, 256))
    h_p, hb_p, comb_p, mask_p = _mix(
        a_in, m_in, ga, gm, x2, w_ba.astype(BF16), w_bm.astype(BF16), w_o.astype(BF16), g1, b1, wrt, brt,
        alpha, False, tm)

    xs2 = x_s.reshape(ns, d)
    chunk = 7 * LANES
    width = -(-(w_in.shape[1] + LANES - N_GATE) // chunk) * chunk
    w_cat = jnp.concatenate([w_a, w_gc, w_r], axis=1)
    w_cat = jnp.pad(w_cat, ((0, 0), (0, width - w_cat.shape[1])))
    z_s = _proj_sample(xs2, w_cat, chunk)
    tabs_s = _rope_tables(jnp.full((1,), past, jnp.int32))
    ppb = _pick_tile(page_table.shape[1], 32)
    n_pool = cache_k.shape[1]
    k_s, v_s, a_s = _paged_attn(page_table + l * n_pool, z_s.reshape(ns, 1, width), tabs_s, lam_p, sub_g,
                                cache_k.reshape(-1, PAGE_SIZE * H_A, HEAD_W),
                                cache_v.reshape(-1, PAGE_SIZE * H_A, HEAD_W), ppb, lam_init)
    o = 3 * QA_W
    qc = z_s[:, o:o + QM_W].reshape(ns, H_M, DK_M, 1)
    kc = z_s[:, o + QM_W:o + 2 * QM_W].reshape(ns, H_M, DK_M, 1)
    v3 = z_s[:, o + 2 * QM_W:W_A].reshape(ns, 1, VM_W)
    gates = z_s[:, W_A:W_A + N_GATE]
    r0 = W_A + LANES
    og3 = z_s[:, r0:r0 + VM_W].reshape(ns, 1, VM_W)
    ga_s = z_s[:, r0 + VM_W:r0 + VM_W + d]
    gm_s = z_s[:, r0 + VM_W + d:r0 + VM_W + 2 * d]
    m_in_s, c_s, n_s, m_s = _mlstm_step(gates, m0, b_ig, b_fg, qc, kc, v3, og3, c0,
                                        n0.reshape(ns, H_M, DK_M, 1), mh_g2)
    h_s, hb_s, comb_s, mask_s = _mix(
        a_s.reshape(ns, QA_W), m_in_s.reshape(ns, VM_W), ga_s, gm_s, xs2, w_ba, w_bm, w_o, g1, b1,
        wrt, brt, alpha, True, ns)

    sgroup = MOE_GROUP * MOE_NSUB
    t_pad = -(-(t + ns) // sgroup) * sgroup
    padt = lambda p, s: jnp.pad(jnp.concatenate([p, s], axis=1), ((0, 0), (0, t_pad - t - ns)))
    hb_all = jnp.pad(jnp.concatenate([hb_p, hb_s], axis=0), ((0, t_pad - t - ns), (0, 0)))
    comb_all = padt(comb_p, comb_s)
    pos_all, cnt = _route(padt(mask_p, mask_s), MOE_GROUP)
    counts = cnt[:, :, 0].astype(jnp.int32).reshape(-1)
    ff = _moe(counts, hb_all, pos_all, comb_all,
              w_gate.astype(BF16), b_gate.reshape(ne, 1, -1), w_up.astype(BF16), b_up.reshape(ne, 1, -1),
              w_down.astype(BF16), b_down.reshape(ne, 1, -1), MOE_GROUP, MOE_NSUB, MOE_ROWS)
    y_p = _ln2(h_p, ff, g2, b2, alpha, tm, 0)
    y_s = _ln2(h_s, ff, g2, b2, alpha, ns, t // ns)

    cw = cn_p[:, :, :DV_M].reshape(bp, H_M, DK_M, DV_M)
    nw = cn_p[:, :, DV_M].reshape(bp, H_M, DK_M)
    outs_p = (y_p.reshape(bp, seq, d), k_f.reshape(bp, seq, H_A, HEAD_W), v_f.reshape(bp, seq, H_A, HEAD_W),
              cw, nw, m_p[:, 0, :H_M])
    outs_s = (y_s.reshape(ns, 1, d), k_s.reshape(ns, 1, H_A, HEAD_W), v_s.reshape(ns, 1, H_A, HEAD_W),
              c_s, n_s.reshape(ns, H_M, DK_M), m_s[:, :, 0])
    return outs_p, outs_s


def kernel(x_prompt, x_sample, cache_k, cache_v, state_c, state_n, state_m, page_table, w_in, b_igate, b_fgate, lambda_q1, lambda_k1, lambda_q2, lambda_k2, subln_g, mh_norm_g, w_ba, w_bm, w_o, ln1_g, ln1_b, w_router, b_router, w_gate, b_gate, w_up, b_up, w_down, b_down, ln2_g, ln2_b):
    depth = w_in.shape[0]
    assert x_sample.shape[1] == 1, "the sample pass handles one new token per sequence"
    weights = (w_in, b_igate, b_fgate, lambda_q1, lambda_k1, lambda_q2, lambda_k2, subln_g, mh_norm_g,
               w_ba, w_bm, w_o, ln1_g, ln1_b, w_router, b_router, w_gate, b_gate, w_up, b_up,
               w_down, b_down, ln2_g, ln2_b)
    y_p, y_s = x_prompt, x_sample
    acc_p = [[] for _ in range(5)]
    acc_s = [[] for _ in range(5)]
    for l in range(depth):
        outs_p, outs_s = _layer(l, depth, y_p, y_s, cache_k, cache_v, state_c[l], state_n[l],
                                state_m[l], page_table, tuple(a[l] for a in weights))
        y_p, y_s = outs_p[0], outs_s[0]
        for i in range(5):
            acc_p[i].append(outs_p[1 + i])
            acc_s[i].append(outs_s[1 + i])
    return (y_p, y_s, *(jnp.stack(a) for a in acc_p), *(jnp.stack(a) for a in acc_s))
```

```python
import functools
import math

import jax
import jax.numpy as jnp
from jax import lax
from jax.experimental import pallas as pl
from jax.experimental.pallas import tpu as pltpu

F32 = jnp.float32
BF16 = jnp.bfloat16
HIGHEST = lax.Precision.HIGHEST

H_A = 4
D_HA = 64
ROT_DIM = D_HA // 4
ROPE_THETA = 500000.0
H_M = 4
DK_M = 64
DV_M = 128
GATE_SOFTCAP = 15.0
N_EXPERTS = 32
TOP_K = 4
SWIGLU_LIMIT = 7.0
SWIGLU_ALPHA = 1.702
LN_EPS = 1e-5
PAGE_SIZE = 128

QA_W = H_A * 2 * D_HA
QM_W = H_M * DK_M
VM_W = H_M * DV_M
HEAD_W = 2 * D_HA
N_GATE = 2 * H_M
W_A = 3 * QA_W + 2 * QM_W + VM_W

LANES = 128
VMEM_LIMIT = 56 * 1024 * 1024
SIDE_SLAB_BYTES = 2 * 1024 * 1024

ATTN_TQ = 256
ATTN_TK = 512
ATTN_HEADS_PER_STEP = 4
MLSTM_SEQS_PER_STEP = 2
MOE_ROWS = 128
MOE_GROUP = 7 * LANES
MOE_NSUB = 2
MOE_SCATTER_EXPERTS = 8

NT_DIMS = (((1,), (1,)), ((), ()))
TN_DIMS = (((0,), (0,)), ((), ()))


def _params(*sem):
    return pltpu.CompilerParams(dimension_semantics=sem, vmem_limit_bytes=VMEM_LIMIT)


def _softcap(x):
    return GATE_SOFTCAP * jnp.tanh(x / GATE_SOFTCAP)


def _log_sigmoid(x):
    return jnp.minimum(x, 0.0) - jnp.log1p(jnp.exp(-jnp.abs(x)))


def _sigmoid(x):
    return 1.0 / (1.0 + jnp.exp(-x))


def _split_bf16(x):
    hi = x.astype(BF16)
    lo = (x - hi.astype(F32)).astype(BF16)
    return hi, lo


def _stack_split(x):
    hi, lo = _split_bf16(x)
    return jnp.concatenate([hi, lo], axis=0)


def _fold_split(t):
    n = t.shape[0] // 2
    return t[:n] + t[n:]


def _layer_norm(x, g, b):
    mu = jnp.mean(x, axis=-1, keepdims=True)
    xc = x - mu
    var = jnp.mean(xc * xc, axis=-1, keepdims=True)
    return xc * lax.rsqrt(var + LN_EPS) * g + b


def _rope(t, ra, rb, rc):
    return t * ra + pltpu.roll(t, 8, 1) * rb + pltpu.roll(t, HEAD_W - 8, 1) * rc


def _rope_tables(pos):
    inv = ROPE_THETA ** (-jnp.arange(0, ROT_DIM, 2, dtype=F32) / ROT_DIM)
    ang = pos.astype(F32)[:, None] * inv[None, :]
    cos, sin = jnp.cos(ang), jnp.sin(ang)
    n = pos.shape[0]
    half = ROT_DIM // 2
    rest = D_HA - ROT_DIM
    a = jnp.concatenate([cos, cos, jnp.ones((n, rest), F32)], axis=1)
    b = jnp.concatenate([jnp.zeros((n, half), F32), sin, jnp.zeros((n, rest), F32)], axis=1)
    c = jnp.concatenate([-sin, jnp.zeros((n, half + rest), F32)], axis=1)
    return tuple(jnp.tile(t, (1, 2)) for t in (a, b, c))


def _diff_lambda(lam_ref, lam_init):
    lp = lam_ref[...]
    s1 = jnp.sum(lp[0:1] * lp[1:2], axis=1, keepdims=True)
    s2 = jnp.sum(lp[2:3] * lp[3:4], axis=1, keepdims=True)
    return jnp.exp(s1) - jnp.exp(s2) + lam_init


def _proj_prompt_kernel(x_ref, wa_ref, wgc_ref, wgr_ref, wr_ref, ra_ref, rb_ref, rc_ref,
                        q_ref, kf_ref, kb_ref, vf_ref, vb_ref, qm_ref, km_ref, vm_ref,
                        gcol_ref, grow_ref, og_ref, ga_ref, gm_ref):
    x = x_ref[...].astype(BF16)
    tm = x.shape[0]
    za = jnp.dot(x, wa_ref[...], preferred_element_type=F32)
    ra, rb, rc = ra_ref[...], rb_ref[...], rc_ref[...]
    for h in range(H_A):
        sl = slice(h * HEAD_W, (h + 1) * HEAD_W)
        qh = _rope(za[:, sl], ra, rb, rc)
        q_ref[:, sl] = (qh * (D_HA ** -0.5)).astype(BF16)
        kh = _rope(za[:, QA_W + h * HEAD_W:QA_W + (h + 1) * HEAD_W], ra, rb, rc)
        kf_ref[pl.ds(h, tm, stride=H_A), :] = kh
        kb_ref[:, sl] = kh.astype(BF16)
        vf_ref[pl.ds(h, tm, stride=H_A), :] = za[:, 2 * QA_W + h * HEAD_W:2 * QA_W + (h + 1) * HEAD_W]
    vb_ref[...] = za[:, 2 * QA_W:3 * QA_W].astype(BF16)
    o = 3 * QA_W
    qm_ref[...] = za[:, o:o + QM_W].astype(BF16)
    km_ref[...] = (za[:, o + QM_W:o + 2 * QM_W] * (DK_M ** -0.5)).astype(BF16)
    vm_ref[...] = za[:, o + 2 * QM_W:].astype(BF16)
    gcol_ref[...] = jnp.dot(x, wgc_ref[...], preferred_element_type=F32)
    grow_ref[...] = lax.dot_general(wgr_ref[...], x, NT_DIMS, preferred_element_type=F32)
    zr = jnp.dot(x, wr_ref[...], preferred_element_type=F32)
    d = ga_ref.shape[1]
    og_ref[...] = zr[:, :VM_W].astype(BF16)
    ga_ref[...] = zr[:, VM_W:VM_W + d].astype(BF16)
    gm_ref[...] = zr[:, VM_W + d:].astype(BF16)


def _proj_prompt(x2, wa, wgc, wgr, wr, tabs, seq, tm):
    t, d = x2.shape
    nrep = seq // tm
    row = lambda w: pl.BlockSpec((tm, w), lambda i: (i, 0))
    full = lambda a: pl.BlockSpec(a.shape, lambda i: (0,) * a.ndim)
    tab = pl.BlockSpec((tm, HEAD_W), lambda i: (i % nrep, 0))
    sds = lambda w, dt: jax.ShapeDtypeStruct((t, w), dt)
    kv_spec = pl.BlockSpec((tm * H_A, HEAD_W), lambda i: (i, 0))
    kv_sds = jax.ShapeDtypeStruct((t * H_A, HEAD_W), F32)
    return pl.pallas_call(
        _proj_prompt_kernel,
        grid=(t // tm,),
        in_specs=[row(d), full(wa), full(wgc), full(wgr), full(wr), tab, tab, tab],
        out_specs=[row(QA_W), kv_spec, row(QA_W), kv_spec, row(QA_W),
                   row(QM_W), row(QM_W), row(VM_W), row(LANES),
                   pl.BlockSpec((N_GATE, tm), lambda i: (0, i)),
                   row(VM_W), row(d), row(d)],
        out_shape=[sds(QA_W, BF16), kv_sds, sds(QA_W, BF16), kv_sds, sds(QA_W, BF16),
                   sds(QM_W, BF16), sds(QM_W, BF16), sds(VM_W, BF16), sds(LANES, F32),
                   jax.ShapeDtypeStruct((N_GATE, t), F32),
                   sds(VM_W, BF16), sds(d, BF16), sds(d, BF16)],
        compiler_params=_params("parallel"),
        name="proj_prompt",
    )(x2, wa, wgc, wgr, wr, *tabs)


def _attn_prompt_kernel(q_ref, k_ref, v_ref, lam_ref, g_ref, *rest, tq, tk, hps, lam_init):
    nside = (len(rest) - 1) // 2
    o_ref = rest[nside]
    for src, dst in zip(rest[:nside], rest[nside + 1:]):
        dst[...] = src[...].astype(dst.dtype)
    qi = pl.program_id(2)
    nfull = (qi * tq) // tk
    lane = lax.broadcasted_iota(jnp.int32, (1, HEAD_W), 1)
    r = lax.broadcasted_iota(jnp.int32, (tq, tk), 0) + qi * tq
    c = lax.broadcasted_iota(jnp.int32, (tq, tk), 1) + nfull * tk
    causal = jnp.concatenate([c <= r, c <= r], axis=0)
    ones_blk = jnp.broadcast_to(jnp.where(lane == 0, 1.0, 0.0).astype(BF16), (tk, HEAD_W))
    qqs = []
    for hh in range(hps):
        q = q_ref[:, hh * HEAD_W:(hh + 1) * HEAD_W]
        zero = jnp.zeros_like(q)
        qqs.append(jnp.concatenate([jnp.where(lane < D_HA, q, zero), jnp.where(lane >= D_HA, q, zero)],
                                   axis=0))

    def step(j, carry, masked):
        off = pl.multiple_of(j * tk, tk)
        out = []
        for hh in range(hps):
            m, acc = carry[hh]
            sl = slice(hh * HEAD_W, (hh + 1) * HEAD_W)
            k = k_ref[pl.ds(off, tk), sl]
            vext = jnp.concatenate([v_ref[pl.ds(off, tk), sl], ones_blk], axis=1)
            s = lax.dot_general(qqs[hh], k, NT_DIMS, preferred_element_type=F32)
            if masked:
                s = jnp.where(causal, s, -jnp.inf)
            m_new = jnp.maximum(m, jnp.max(s, axis=1, keepdims=True))
            p = jnp.exp(s - m_new)
            acc = jnp.exp(m - m_new) * acc + jnp.dot(p.astype(BF16), vext, preferred_element_type=F32)
            out.append((m_new, acc))
        return tuple(out)

    init = tuple((jnp.full((2 * tq, 1), -jnp.inf, F32), jnp.zeros((2 * tq, 2 * HEAD_W), F32))
                 for _ in range(hps))
    carry = lax.fori_loop(0, nfull, lambda j, cr: step(j, cr, False), init)
    carry = step(nfull, carry, True)
    lam = _diff_lambda(lam_ref, lam_init)
    for hh in range(hps):
        acc = carry[hh][1]
        o1 = acc[:tq, :HEAD_W] / acc[:tq, HEAD_W:HEAD_W + 1]
        o2 = acc[tq:, :HEAD_W] / acc[tq:, HEAD_W:HEAD_W + 1]
        o = o1 - lam * o2
        y = o * lax.rsqrt(jnp.mean(o * o, axis=1, keepdims=True) + LN_EPS) * g_ref[...] * (1.0 - lam_init)
        o_ref[:, hh * HEAD_W:(hh + 1) * HEAD_W] = y.astype(o_ref.dtype)


def _attn_prompt(q, k, v, lam_p, subln_g, batch, seq, lam_init, side):
    t = q.shape[0]
    tk = _pick_tile(seq, ATTN_TK)
    tq = _pick_tile(tk, ATTN_TQ)
    nq = seq // tq
    hps = ATTN_HEADS_PER_STEP
    nh = H_A // hps
    w = hps * HEAD_W
    steps = batch * nh * nq
    side2 = [a.reshape(-1, a.shape[-1]) for a in side]
    slab = [a.shape[0] // steps for a in side2]
    if any(a.shape[0] % steps or s % 16 or s * a.shape[1] * 4 > SIDE_SLAB_BYTES for a, s in zip(side2, slab)):
        side2, slab = [], []
    side_specs = [pl.BlockSpec((s, a.shape[1]), lambda b, h, i: ((b * nh + h) * nq + i, 0))
                  for a, s in zip(side2, slab)]
    outs = pl.pallas_call(
        functools.partial(_attn_prompt_kernel, tq=tq, tk=tk, hps=hps, lam_init=lam_init),
        grid=(batch, nh, nq),
        in_specs=[pl.BlockSpec((tq, w), lambda b, h, i: (b * nq + i, h)),
                  pl.BlockSpec((seq, w), lambda b, h, i: (b, h)),
                  pl.BlockSpec((seq, w), lambda b, h, i: (b, h)),
                  pl.BlockSpec(lam_p.shape, lambda b, h, i: (0, 0)),
                  pl.BlockSpec((1, HEAD_W), lambda b, h, i: (0, 0))] + side_specs,
        out_specs=[pl.BlockSpec((tq, w), lambda b, h, i: (b * nq + i, h))] + side_specs,
        out_shape=[jax.ShapeDtypeStruct((t, QA_W), BF16)]
        + [jax.ShapeDtypeStruct(a.shape, BF16) for a in side2],
        compiler_params=_params("parallel", "parallel", "parallel"),
        name="attn_prompt",
    )(q, k, v, lam_p, subln_g, *side2)
    if side2:
        return outs[0], [o.reshape(a.shape) for o, a in zip(outs[1:], side)]
    return outs[0], [a.astype(BF16) for a in side]


def _mlstm_prompt_kernel(q_ref, k_ref, v_ref, gcol_ref, grow_ref, bcol_ref, brow_ref, og_ref, g_ref,
                         min_ref, cn_ref, m_ref, *, lc, bps):
    ci = pl.program_id(1)

    @pl.when(ci == 0)
    def _():
        cn_ref[...] = jnp.zeros_like(cn_ref)
        m_ref[...] = jnp.zeros_like(m_ref)

    for i in range(bps):
        _mlstm_chunk(q_ref.at[0, i], k_ref.at[0, i], v_ref.at[0, i], gcol_ref.at[0, i], grow_ref[:, 0, i, :],
                     bcol_ref, brow_ref, og_ref.at[0, i], g_ref, min_ref.at[0, i], cn_ref.at[0, i],
                     m_ref.at[0, i], lc)


def _mlstm_chunk(q_ref, k_ref, v_ref, gcol_ref, grow, bcol_ref, brow_ref, og_ref, g_ref,
                 min_ref, cn_ref, m_ref, lc):
    gc = _softcap(gcol_ref[...] + bcol_ref[...])
    lfc = _log_sigmoid(gc)
    gr = _softcap(grow + brow_ref[...])
    lfr = _log_sigmoid(gr)
    ri = lax.broadcasted_iota(jnp.int32, (lc, lc), 0)
    cj = lax.broadcasted_iota(jnp.int32, (lc, lc), 1)
    tril = cj <= ri
    b_col = jnp.dot(tril.astype(F32), lfc, precision=HIGHEST, preferred_element_type=F32)
    b_row = jnp.dot(lfr, (ri <= cj).astype(F32), precision=HIGHEST, preferred_element_type=F32)

    cn = cn_ref[...]
    cn_bf = cn.astype(BF16)
    q = q_ref[...]
    k = k_ref[...]
    v = v_ref[...]
    m_all = m_ref[...]
    lane_q = lax.broadcasted_iota(jnp.int32, (1, QM_W), 1) // DK_M
    row_q = lax.broadcasted_iota(jnp.int32, (QM_W, 1), 0) // DK_M
    lane_v = lax.broadcasted_iota(jnp.int32, (1, LANES), 1)
    ones_blk = jnp.broadcast_to(jnp.where(lane_v == 0, 1.0, 0.0).astype(BF16), (lc, LANES))
    new_cn = jnp.zeros_like(cn)
    new_m = m_all
    for h in range(H_M):
        sl = slice(h * DV_M, (h + 1) * DV_M)
        qh = jnp.where(lane_q == h, q, jnp.zeros_like(q))
        kh = jnp.where(lane_q == h, k, jnp.zeros_like(k))
        b_c = b_col[:, H_M + h:H_M + h + 1]
        ig_c = gc[:, h:h + 1]
        b_r = b_row[H_M + h:H_M + h + 1, :]
        ig_r = gr[h:h + 1, :]
        m_prev = m_all[:, h:h + 1]
        dmat = jnp.where(tril, b_c - b_r + ig_r, -jnp.inf)
        m_inter = b_c + m_prev
        m_t = jnp.maximum(m_inter, jnp.max(dmat, axis=1, keepdims=True))
        w_inter = jnp.exp(m_inter - m_t)
        s = lax.dot_general(qh, kh, NT_DIMS, preferred_element_type=F32) * jnp.exp(dmat - m_t)
        vext = jnp.concatenate([v[:, sl], ones_blk], axis=1)
        nd = (w_inter * jnp.dot(qh, cn_bf, preferred_element_type=F32)
              + jnp.dot(s.astype(BF16), vext, preferred_element_type=F32))
        num = nd[:, :DV_M]
        den = nd[:, DV_M:DV_M + 1]
        hout = num / jnp.maximum(jnp.abs(den), jnp.exp(-m_t))
        mu = jnp.mean(hout, axis=1, keepdims=True)
        hc = hout - mu
        var = jnp.mean(hc * hc, axis=1, keepdims=True)
        y = hc * lax.rsqrt(var + LN_EPS) * g_ref[:, sl] * _sigmoid(og_ref[:, sl].astype(F32))
        min_ref[:, sl] = y.astype(min_ref.dtype)
        m_new = m_t[lc - 1:lc, :]
        b_last = b_c[lc - 1:lc, :]
        decay = jnp.exp(b_last + m_prev - m_new)
        wk = jnp.exp(b_last - b_c + ig_c - m_new)
        upd = lax.dot_general(kh, (wk * vext.astype(F32)).astype(BF16), TN_DIMS,
                              preferred_element_type=F32)
        new_cn = new_cn + jnp.where(row_q == h, decay * cn, 0.0) + upd
        new_m = jnp.where(lane_v == h, m_new, new_m)
    cn_ref[...] = new_cn
    m_ref[...] = new_m


def _mlstm_prompt(qm, km, vm, gcol, grow, bcol, brow, og, mh_g, batch, seq, lc):
    t = qm.shape[0]
    bps = MLSTM_SEQS_PER_STEP if batch % MLSTM_SEQS_PER_STEP == 0 else 1
    nb = batch // bps
    row = lambda w: pl.BlockSpec((1, bps, lc, w), lambda b, c: (b, 0, c, 0))
    const = lambda a: pl.BlockSpec(a.shape, lambda b, c: (0,) * a.ndim)
    seqs = lambda a: a.reshape(nb, bps, seq, a.shape[1])
    cw = 2 * LANES
    m_in, cn, m = pl.pallas_call(
        functools.partial(_mlstm_prompt_kernel, lc=lc, bps=bps),
        grid=(nb, seq // lc),
        in_specs=[row(QM_W), row(QM_W), row(VM_W), row(LANES),
                  pl.BlockSpec((N_GATE, 1, bps, lc), lambda b, c: (0, b, 0, c)),
                  const(bcol), const(brow), row(VM_W), const(mh_g)],
        out_specs=[row(VM_W),
                   pl.BlockSpec((1, bps, QM_W, cw), lambda b, c: (b, 0, 0, 0)),
                   pl.BlockSpec((1, bps, 1, LANES), lambda b, c: (b, 0, 0, 0))],
        out_shape=[jax.ShapeDtypeStruct((nb, bps, seq, VM_W), BF16),
                   jax.ShapeDtypeStruct((nb, bps, QM_W, cw), F32),
                   jax.ShapeDtypeStruct((nb, bps, 1, LANES), F32)],
        compiler_params=_params("parallel", "arbitrary"),
        name="mlstm_prompt",
    )(seqs(qm), seqs(km), seqs(vm), seqs(gcol), grow.reshape(N_GATE, nb, bps, seq), bcol, brow,
      seqs(og), mh_g)
    return m_in.reshape(t, VM_W), cn.reshape(batch, QM_W, cw), m.reshape(batch, 1, LANES)


def _mix_kernel(a_ref, mi_ref, ga_ref, gm_ref, x_ref, wba_ref, wbm_ref, wo_ref, g1_ref, b1_ref,
                wrt_ref, brt_ref, h_ref, hb_ref, comb_ref, mask_ref, *, alpha, precise):
    def mm(a, w_ref):
        if precise:
            return jnp.dot(a.astype(F32), w_ref[...], precision=HIGHEST, preferred_element_type=F32)
        return jnp.dot(a.astype(BF16), w_ref[...], preferred_element_type=F32)

    a_br = mm(a_ref[...], wba_ref)
    m_br = mm(mi_ref[...], wbm_ref)
    merged = _sigmoid(ga_ref[...].astype(F32)) * a_br + _sigmoid(gm_ref[...].astype(F32)) * m_br
    mix = mm(merged, wo_ref)
    h = _layer_norm(alpha * x_ref[...] + mix, g1_ref[...], b1_ref[...])
    h_ref[...] = h
    hb_ref[...] = h.astype(BF16)

    ne = wrt_ref.shape[0]
    if precise:
        logits = lax.dot_general(wrt_ref[...], h, NT_DIMS, precision=HIGHEST, preferred_element_type=F32)
    else:
        ws = _stack_split(wrt_ref[...])
        h_hi, h_lo = _split_bf16(h)
        logits = (_fold_split(lax.dot_general(ws, h_hi, NT_DIMS, preferred_element_type=F32))
                  + lax.dot_general(ws[:ne], h_lo, NT_DIMS, preferred_element_type=F32))
    logits = logits + brt_ref[...]
    tm = logits.shape[1]
    eidx = lax.broadcasted_iota(jnp.int32, (ne, tm), 0)
    work = logits
    sel = jnp.zeros((ne, tm), jnp.bool_)
    top = None
    for _ in range(TOP_K):
        mx = jnp.max(work, axis=0, keepdims=True)
        top = mx if top is None else top
        first = jnp.min(jnp.where(work == mx, eidx, ne), axis=0, keepdims=True)
        pick = eidx == first
        sel = jnp.logical_or(sel, pick)
        work = jnp.where(pick, -jnp.inf, work)
    ex = jnp.where(sel, jnp.exp(logits - top), 0.0)
    comb_ref[...] = ex / jnp.sum(ex, axis=0, keepdims=True)
    mask_ref[...] = jnp.where(sel, 1.0, 0.0)


def _mix(a_in, m_in, ga, gm, x2, wba, wbm, wo, g1, b1, wrt, brt, alpha, precise, tm):
    t, d = x2.shape
    ne = wrt.shape[0]
    row = lambda w: pl.BlockSpec((tm, w), lambda i: (i, 0))
    col = pl.BlockSpec((ne, tm), lambda i: (0, i))
    const = lambda a: pl.BlockSpec(a.shape, lambda i: (0,) * a.ndim)
    return pl.pallas_call(
        functools.partial(_mix_kernel, alpha=alpha, precise=precise),
        grid=(t // tm,),
        in_specs=[row(QA_W), row(VM_W), row(d), row(d), row(d), const(wba), const(wbm), const(wo),
                  const(g1), const(b1), const(wrt), const(brt)],
        out_specs=[row(d), row(d), col, col],
        out_shape=[jax.ShapeDtypeStruct((t, d), F32), jax.ShapeDtypeStruct((t, d), BF16),
                   jax.ShapeDtypeStruct((ne, t), F32), jax.ShapeDtypeStruct((ne, t), F32)],
        compiler_params=_params("parallel"),
        name="mix_sample" if precise else "mix_prompt",
    )(a_in, m_in, ga, gm, x2, wba, wbm, wo, g1, b1, wrt, brt)


def _route_kernel(mask_ref, pos_ref, cnt_ref):
    maskf = mask_ref[...]
    g = maskf.shape[1]
    r = lax.broadcasted_iota(jnp.int32, (g, g), 0)
    c = lax.broadcasted_iota(jnp.int32, (g, g), 1)
    before = jnp.dot(maskf.astype(BF16), (r < c).astype(BF16), preferred_element_type=F32)
    pos_ref[...] = jnp.where(maskf > 0.0, before, -1.0)
    cnt_ref[0] = jnp.broadcast_to(jnp.sum(maskf, axis=1, keepdims=True), cnt_ref.shape[1:])


def _route(mask, group):
    ne, t = mask.shape
    ng = t // group
    return pl.pallas_call(
        _route_kernel,
        grid=(ng,),
        in_specs=[pl.BlockSpec((ne, group), lambda g: (0, g))],
        out_specs=[pl.BlockSpec((ne, group), lambda g: (0, g)),
                   pl.BlockSpec((1, ne, LANES), lambda g: (g, 0, 0))],
        out_shape=[jax.ShapeDtypeStruct((ne, t), F32), jax.ShapeDtypeStruct((ng, ne, LANES), F32)],
        compiler_params=_params("parallel"),
        name="route",
    )(mask)


def _moe_kernel(cnt_ref, nact_ref, h_ref, pos_ref, comb_ref, wg_ref, bg_ref, wu_ref, bu_ref,
                wd_ref, bd_ref, o_ref, y_scr, *, rows, group, nsub):
    sg = pl.program_id(0)
    e = pl.program_id(1)
    ne = pl.num_programs(1)
    n_exp = pos_ref.shape[0]

    @pl.when(e == 0)
    def _():
        o_ref[...] = jnp.zeros_like(o_ref)

    slot0 = lax.broadcasted_iota(jnp.int32, (rows, 1), 0).astype(F32)

    def ffn(hit, tok):
        xg = jnp.dot(jnp.where(hit, 1.0, 0.0).astype(BF16), h_ref[tok, :],
                     preferred_element_type=F32).astype(BF16)
        gate = jnp.minimum(jnp.dot(xg, wg_ref[0], preferred_element_type=F32) + bg_ref[0], SWIGLU_LIMIT)
        up = jnp.clip(jnp.dot(xg, wu_ref[0], preferred_element_type=F32) + bu_ref[0],
                      -SWIGLU_LIMIT, SWIGLU_LIMIT)
        hid = gate * _sigmoid(SWIGLU_ALPHA * gate) * (up + 1.0)
        return (jnp.dot(hid.astype(BF16), wd_ref[0], preferred_element_type=F32) + bd_ref[0]).astype(BF16)

    yrows = pl.ds(pl.multiple_of(e * rows, rows), rows)

    def first_tile(sub):
        tok = slice(sub * group, (sub + 1) * group)
        y_scr[sub, yrows, :] = ffn(pos_ref[pl.ds(e, 1), tok] == slot0, tok)

    nact = nact_ref[sg]

    @pl.when(nact == nsub)
    def _():
        for sub in range(nsub):
            first_tile(sub)

    @pl.when(nact < nsub)
    def _():
        for sub in range(nsub):
            pl.when(sub < nact)(functools.partial(first_tile, sub))

            @pl.when(sub >= nact)
            def _(sub=sub):
                y_scr[sub, yrows, :] = jnp.zeros((rows, y_scr.shape[2]), y_scr.dtype)

    for sub in range(nsub):
        tok = slice(sub * group, (sub + 1) * group)
        n = cnt_ref[(sg * nsub + sub) * ne + e]
        pos = pos_ref[pl.ds(e, 1), tok]
        cw = comb_ref[pl.ds(e, 1), tok]

        def tile(s, carry, tok=tok, pos=pos, cw=cw):
            hit = pos == slot0 + (s * rows).astype(F32)
            gw = jnp.where(hit, cw, 0.0).astype(BF16)
            o_ref[tok, :] += lax.dot_general(gw, ffn(hit, tok), TN_DIMS, preferred_element_type=F32)
            return carry

        lax.fori_loop(1, (n + rows - 1) // rows, tile, 0)

    @pl.when(e == ne - 1)
    def _():
        for sub in range(nsub):
            tok = slice(sub * group, (sub + 1) * group)
            for c0 in range(0, n_exp, MOE_SCATTER_EXPERTS):
                gw = jnp.concatenate(
                    [jnp.where(pos_ref[ee:ee + 1, tok] == slot0, comb_ref[ee:ee + 1, tok], 0.0).astype(BF16)
                     for ee in range(c0, c0 + MOE_SCATTER_EXPERTS)], axis=0)
                ys = y_scr[sub, c0 * rows:(c0 + MOE_SCATTER_EXPERTS) * rows, :]
                o_ref[tok, :] += lax.dot_general(gw, ys, TN_DIMS, preferred_element_type=F32)


def _moe(counts, nact, hb, pos, comb, wg, bg, wu, bu, wd, bd, group, nsub, rows):
    t, d = hb.shape
    ne = wg.shape[0]
    sgroup = group * nsub
    tok = pl.BlockSpec((ne, sgroup), lambda g, e, c, a: (0, g))
    wspec = lambda w: pl.BlockSpec((1,) + w.shape[1:], lambda g, e, c, a: (e, 0, 0))
    grid_spec = pltpu.PrefetchScalarGridSpec(
        num_scalar_prefetch=2,
        grid=(t // sgroup, ne),
        in_specs=[pl.BlockSpec((sgroup, d), lambda g, e, c, a: (g, 0)), tok, tok,
                  wspec(wg), wspec(bg), wspec(wu), wspec(bu), wspec(wd), wspec(bd)],
        out_specs=pl.BlockSpec((sgroup, d), lambda g, e, c, a: (g, 0)),
        scratch_shapes=[pltpu.VMEM((nsub, ne * rows, d), BF16)],
    )
    return pl.pallas_call(
        functools.partial(_moe_kernel, rows=rows, group=group, nsub=nsub),
        grid_spec=grid_spec,
        out_shape=jax.ShapeDtypeStruct((t, d), F32),
        compiler_params=_params("parallel", "arbitrary"),
        name="moe",
    )(counts, nact, hb, pos, comb, wg, bg, wu, bu, wd, bd)


def _ln2_kernel(h_ref, ff_ref, g_ref, b_ref, y_ref, *, alpha):
    y_ref[...] = _layer_norm(alpha * h_ref[...] + ff_ref[...], g_ref[...], b_ref[...])


def _ln2(h, ff, g2, b2, alpha, tm, ff_block0):
    t, d = h.shape
    const = lambda a: pl.BlockSpec(a.shape, lambda i: (0,) * a.ndim)
    return pl.pallas_call(
        functools.partial(_ln2_kernel, alpha=alpha),
        grid=(t // tm,),
        in_specs=[pl.BlockSpec((tm, d), lambda i: (i, 0)),
                  pl.BlockSpec((tm, d), lambda i: (ff_block0 + i, 0)), const(g2), const(b2)],
        out_specs=pl.BlockSpec((tm, d), lambda i: (i, 0)),
        out_shape=jax.ShapeDtypeStruct((t, d), F32),
        compiler_params=_params("parallel"),
        name="ln2",
    )(h, ff, g2, b2)


def _proj_sample_kernel(x_ref, w_ref, z_ref):
    z_ref[...] = jnp.dot(x_ref[...], w_ref[...], precision=HIGHEST, preferred_element_type=F32)


def _proj_sample(xs, w, chunk):
    n, d = xs.shape
    width = w.shape[1]
    return pl.pallas_call(
        _proj_sample_kernel,
        grid=(width // chunk,),
        in_specs=[pl.BlockSpec((n, d), lambda j: (0, 0)), pl.BlockSpec((d, chunk), lambda j: (0, j))],
        out_specs=pl.BlockSpec((n, chunk), lambda j: (0, j)),
        out_shape=jax.ShapeDtypeStruct((n, width), F32),
        compiler_params=_params("parallel"),
        name="proj_sample",
    )(xs, w)


def _paged_attn_kernel(pt_ref, z_ref, ra_ref, rb_ref, rc_ref, lam_ref, g_ref, *rest,
                       ppb, lam_init):
    k_refs = rest[:ppb]
    v_refs = rest[ppb:2 * ppb]
    kout_ref, vout_ref, a_ref, q_scr, qs_scr, m_scr, l_scr, acc_scr = rest[2 * ppb:]
    j = pl.program_id(1)
    nj = pl.num_programs(1)
    nrow = 2 * H_A
    prow = PAGE_SIZE * H_A
    ra, rb, rc = ra_ref[...], rb_ref[...], rc_ref[...]
    lane = lax.broadcasted_iota(jnp.int32, (1, HEAD_W), 1)

    def per_row(t):
        return jnp.concatenate([t[:, (r // 2) * HEAD_W:(r // 2 + 1) * HEAD_W] for r in range(nrow)], axis=0)

    @pl.when(j == 0)
    def _():
        z = z_ref[0]
        rows = []
        for h in range(H_A):
            qh = _rope(z[:, h * HEAD_W:(h + 1) * HEAD_W], ra, rb, rc) * (D_HA ** -0.5)
            rows += [jnp.where(lane < D_HA, qh, 0.0), jnp.where(lane >= D_HA, qh, 0.0)]
        q = jnp.concatenate(rows, axis=0)
        q_scr[...] = q
        qs_scr[...] = _stack_split(q)
        m_scr[...] = jnp.full_like(m_scr, -jnp.inf)
        l_scr[...] = jnp.zeros_like(l_scr)
        acc_scr[...] = jnp.zeros_like(acc_scr)

    qs = qs_scr[...]
    parts = []
    for kr in k_refs:
        k_hi, k_lo = _split_bf16(kr[0])
        t = (lax.dot_general(qs, k_hi, NT_DIMS, preferred_element_type=F32)
             + lax.dot_general(qs, k_lo, NT_DIMS, preferred_element_type=F32))
        parts.append(_fold_split(t))
    s = jnp.concatenate(parts, axis=1)
    r_head = lax.broadcasted_iota(jnp.int32, s.shape, 0) // 2
    c_head = lax.broadcasted_iota(jnp.int32, s.shape, 1) % H_A
    s = jnp.where(r_head == c_head, s, -jnp.inf)
    m_old = m_scr[...]
    m_new = jnp.maximum(m_old, jnp.max(s, axis=1, keepdims=True))
    alpha = jnp.exp(m_old - m_new)
    p = jnp.exp(s - m_new)
    l_scr[...] = alpha * l_scr[...] + jnp.sum(p, axis=1, keepdims=True)
    ps = _stack_split(p)
    pv = None
    for i, vr in enumerate(v_refs):
        v_hi, v_lo = _split_bf16(vr[0])
        pi = ps[:, i * prow:(i + 1) * prow]
        t = jnp.dot(pi, v_hi, preferred_element_type=F32) + jnp.dot(pi, v_lo, preferred_element_type=F32)
        pv = t if pv is None else pv + t
    acc_scr[...] = alpha * acc_scr[...] + _fold_split(pv)
    m_scr[...] = m_new

    @pl.when(j == nj - 1)
    def _():
        z = z_ref[0]
        k_new = jnp.concatenate([_rope(z[:, QA_W + h * HEAD_W:QA_W + (h + 1) * HEAD_W], ra, rb, rc)
                                 for h in range(H_A)], axis=1)
        v_new = z[:, 2 * QA_W:3 * QA_W]
        kout_ref[0] = k_new
        vout_ref[0] = v_new
        s_self = jnp.sum(q_scr[...] * per_row(k_new), axis=1, keepdims=True)
        m_old2 = m_scr[...]
        m_fin = jnp.maximum(m_old2, s_self)
        a2 = jnp.exp(m_old2 - m_fin)
        p_self = jnp.exp(s_self - m_fin)
        l_fin = a2 * l_scr[...] + p_self
        o_all = (a2 * acc_scr[...] + p_self * per_row(v_new)) / l_fin
        lam = _diff_lambda(lam_ref, lam_init)
        outs = []
        for h in range(H_A):
            o = o_all[2 * h:2 * h + 1] - lam * o_all[2 * h + 1:2 * h + 2]
            outs.append(o * lax.rsqrt(jnp.mean(o * o, axis=1, keepdims=True) + LN_EPS)
                        * g_ref[...] * (1.0 - lam_init))
        a_ref[0] = jnp.concatenate(outs, axis=1)


def _paged_attn(page_table, z3, tabs, lam_p, subln_g, ck, cv, ppb, lam_init):
    nb, npages = page_table.shape
    width = z3.shape[2]
    nj = npages // ppb
    pt = page_table.reshape(-1)
    prow = PAGE_SIZE * H_A
    nrow = 2 * H_A
    const = lambda a: pl.BlockSpec(a.shape, lambda b, j, p: (0,) * a.ndim)
    page = lambda i: pl.BlockSpec((1, prow, HEAD_W),
                                  lambda b, j, p: (p[b * npages + j * ppb + i], 0, 0))
    seq3 = lambda w: pl.BlockSpec((1, 1, w), lambda b, j, p: (b, 0, 0))
    grid_spec = pltpu.PrefetchScalarGridSpec(
        num_scalar_prefetch=1,
        grid=(nb, nj),
        in_specs=[seq3(width), const(tabs[0]), const(tabs[1]), const(tabs[2]), const(lam_p),
                  const(subln_g)] + [page(i) for i in range(ppb)] + [page(i) for i in range(ppb)],
        out_specs=[seq3(QA_W), seq3(QA_W), seq3(QA_W)],
        scratch_shapes=[pltpu.VMEM((nrow, HEAD_W), F32), pltpu.VMEM((2 * nrow, HEAD_W), BF16),
                        pltpu.VMEM((nrow, 1), F32), pltpu.VMEM((nrow, 1), F32),
                        pltpu.VMEM((nrow, HEAD_W), F32)],
    )
    sds = jax.ShapeDtypeStruct((nb, 1, QA_W), F32)
    return pl.pallas_call(
        functools.partial(_paged_attn_kernel, ppb=ppb, lam_init=lam_init),
        grid_spec=grid_spec,
        out_shape=[sds, sds, sds],
        compiler_params=_params("parallel", "arbitrary"),
        name="paged_attn",
    )(pt, z3, *tabs, lam_p, subln_g, *([ck] * ppb), *([cv] * ppb))


def _mlstm_step_kernel(gates_ref, m0_ref, big_ref, bfg_ref, qc_ref, kc_ref, v_ref, og_ref, c0_ref,
                       n0_ref, g_ref, min_ref, c_ref, n_ref, m_ref):
    b = pl.program_id(0)
    outs = []
    for h in range(H_M):
        sl = slice(h * DV_M, (h + 1) * DV_M)
        ig = _softcap(jnp.full((1, LANES), gates_ref[b, h] + big_ref[h], F32))
        lf = _log_sigmoid(_softcap(jnp.full((1, LANES), gates_ref[b, H_M + h] + bfg_ref[h], F32)))
        m0 = jnp.full((1, LANES), m0_ref[b, h], F32)
        m_inter = lf + m0
        m_t = jnp.maximum(m_inter, ig)
        w_inter = jnp.exp(m_inter - m_t)
        w_new = jnp.exp(ig - m_t)
        qc = qc_ref[0, h]
        kc = kc_ref[0, h] * (DK_M ** -0.5)
        vr = v_ref[0][:, sl]
        c0 = c0_ref[0, h]
        n0 = n0_ref[0, h]
        qk = jnp.sum(qc * kc, axis=0, keepdims=True)
        s = qk * w_new
        num = w_inter * jnp.sum(qc * c0, axis=0, keepdims=True) + s * vr
        den = w_inter * jnp.sum(qc * n0, axis=0, keepdims=True) + s
        hout = num / jnp.maximum(jnp.abs(den), jnp.exp(-m_t))
        mu = jnp.mean(hout, axis=1, keepdims=True)
        hc = hout - mu
        var = jnp.mean(hc * hc, axis=1, keepdims=True)
        outs.append(hc * lax.rsqrt(var + LN_EPS) * g_ref[:, sl] * _sigmoid(og_ref[0][:, sl]))
        c_ref[0, h] = w_inter * c0 + (w_new * kc) * vr
        n_ref[0, h] = w_inter[:, 0:1] * n0 + w_new[:, 0:1] * kc
        m_ref[0, h:h + 1, :] = m_t
    min_ref[0] = jnp.concatenate(outs, axis=1)


def _mlstm_step(gates, m0, big, bfg, qc, kc, v3, og3, c0, n0c, mh_g):
    nb = c0.shape[0]
    smem = pl.BlockSpec(memory_space=pltpu.SMEM)
    per = lambda a: pl.BlockSpec((1,) + a.shape[1:], lambda b: (b,) + (0,) * (a.ndim - 1))
    const = lambda a: pl.BlockSpec(a.shape, lambda b: (0,) * a.ndim)
    return pl.pallas_call(
        _mlstm_step_kernel,
        grid=(nb,),
        in_specs=[smem, smem, smem, smem, per(qc), per(kc), per(v3), per(og3), per(c0), per(n0c),
                  const(mh_g)],
        out_specs=[per(v3), per(c0), per(n0c), pl.BlockSpec((1, H_M, LANES), lambda b: (b, 0, 0))],
        out_shape=[jax.ShapeDtypeStruct(v3.shape, F32), jax.ShapeDtypeStruct(c0.shape, F32),
                   jax.ShapeDtypeStruct(n0c.shape, F32),
                   jax.ShapeDtypeStruct((nb, H_M, LANES), F32)],
        compiler_params=_params("parallel"),
        name="mlstm_step",
    )(gates, m0, big, bfg, qc, kc, v3, og3, c0, n0c, mh_g)


def _pick_tile(n, target):
    t = min(n, target)
    while n % t:
        t //= 2
    return t


def _layer(l, depth, x_p, x_s, cache_k, cache_v, c0, n0, m0, page_table, w):
    (w_in, b_ig, b_fg, lq1, lk1, lq2, lk2, subln_g, mh_g, w_ba, w_bm, w_o, ln1_g, ln1_b,
     w_router, b_router, w_gate, b_gate, w_up, b_up, w_down, b_down, ln2_g, ln2_b) = w
    alpha = (2.0 * depth) ** 0.25
    lam_init = 0.8 - 0.6 * math.exp(-0.3 * l)
    bp, seq, d = x_p.shape
    ns = x_s.shape[0]
    t = bp * seq
    ne = w_router.shape[1]
    past = page_table.shape[1] * PAGE_SIZE

    w_a = w_in[:, :W_A]
    w_gt = w_in[:, W_A:W_A + N_GATE]
    w_r = w_in[:, W_A + N_GATE:]
    w_gc = jnp.pad(w_gt, ((0, 0), (0, LANES - N_GATE)))
    lam_p = jnp.stack([lq1, lk1, lq2, lk2])
    sub_g = subln_g.reshape(1, HEAD_W)
    mh_g2 = mh_g.reshape(1, VM_W)
    bcol = jnp.pad(jnp.concatenate([b_ig, b_fg]), (0, LANES - N_GATE)).reshape(1, LANES)
    brow = jnp.concatenate([b_ig, b_fg]).reshape(N_GATE, 1)
    g1, b1 = ln1_g.reshape(1, d), ln1_b.reshape(1, d)
    g2, b2 = ln2_g.reshape(1, d), ln2_b.reshape(1, d)
    wrt = w_router.T
    brt = b_router.reshape(ne, 1)

    x2 = x_p.reshape(t, d)
    tm = _pick_tile(seq, 256)
    tabs_p = _rope_tables(jnp.arange(seq, dtype=jnp.int32))
    (qa, k_f, k_b, v_f, v_b, qm, km, vm, gcol, grow, og, ga, gm) = _proj_prompt(
        x2, w_a.astype(BF16), w_gc.astype(BF16), w_gt.T.astype(BF16), w_r.astype(BF16), tabs_p, seq, tm)
    a_in, (wg_b, wu_b, wd_b) = _attn_prompt(qa, k_b, v_b, lam_p, sub_g, bp, seq, lam_init,
                                            (w_gate, w_up, w_down))
    m_in, cn_p, m_p = _mlstm_prompt(qm, km, vm, gcol, grow, bcol, brow, og, mh_g2, bp, seq,
                                    _pick_tile(seq, 256))
    h_p, hb_p, comb_p, mask_p = _mix(
        a_in, m_in, ga, gm, x2, w_ba.astype(BF16), w_bm.astype(BF16), w_o.astype(BF16), g1, b1, wrt, brt,
        alpha, False, tm)

    xs2 = x_s.reshape(ns, d)
    chunk = 7 * LANES
    width = -(-(w_in.shape[1] + LANES - N_GATE) // chunk) * chunk
    w_cat = jnp.concatenate([w_a, w_gc, w_r], axis=1)
    w_cat = jnp.pad(w_cat, ((0, 0), (0, width - w_cat.shape[1])))
    z_s = _proj_sample(xs2, w_cat, chunk)
    tabs_s = _rope_tables(jnp.full((1,), past, jnp.int32))
    ppb = _pick_tile(page_table.shape[1], 32)
    n_pool = cache_k.shape[1]
    k_s, v_s, a_s = _paged_attn(page_table + l * n_pool, z_s.reshape(ns, 1, width), tabs_s, lam_p, sub_g,
                                cache_k.reshape(-1, PAGE_SIZE * H_A, HEAD_W),
                                cache_v.reshape(-1, PAGE_SIZE * H_A, HEAD_W), ppb, lam_init)
    o = 3 * QA_W
    qc = z_s[:, o:o + QM_W].reshape(ns, H_M, DK_M, 1)
    kc = z_s[:, o + QM_W:o + 2 * QM_W].reshape(ns, H_M, DK_M, 1)
    v3 = z_s[:, o + 2 * QM_W:W_A].reshape(ns, 1, VM_W)
    gates = z_s[:, W_A:W_A + N_GATE]
    r0 = W_A + LANES
    og3 = z_s[:, r0:r0 + VM_W].reshape(ns, 1, VM_W)
    ga_s = z_s[:, r0 + VM_W:r0 + VM_W + d]
    gm_s = z_s[:, r0 + VM_W + d:r0 + VM_W + 2 * d]
    m_in_s, c_s, n_s, m_s = _mlstm_step(gates, m0, b_ig, b_fg, qc, kc, v3, og3, c0,
                                        n0.reshape(ns, H_M, DK_M, 1), mh_g2)
    h_s, hb_s, comb_s, mask_s = _mix(
        a_s.reshape(ns, QA_W), m_in_s.reshape(ns, VM_W), ga_s, gm_s, xs2, w_ba, w_bm, w_o, g1, b1,
        wrt, brt, alpha, True, ns)

    sgroup = MOE_GROUP * MOE_NSUB
    t_pad = -(-(t + ns) // sgroup) * sgroup
    padt = lambda p, s: jnp.pad(jnp.concatenate([p, s], axis=1), ((0, 0), (0, t_pad - t - ns)))
    hb_all = jnp.pad(jnp.concatenate([hb_p, hb_s], axis=0), ((0, t_pad - t - ns), (0, 0)))
    comb_all = padt(comb_p, comb_s)
    pos_all, cnt = _route(padt(mask_p, mask_s), MOE_GROUP)
    counts = cnt[:, :, 0].astype(jnp.int32)
    used = (jnp.sum(counts, axis=1) > 0).reshape(-1, MOE_NSUB)
    nact = jnp.max(jnp.where(used, jnp.arange(1, MOE_NSUB + 1, dtype=jnp.int32), 0), axis=1)
    ff = _moe(counts.reshape(-1), nact, hb_all, pos_all, comb_all,
              wg_b, b_gate.reshape(ne, 1, -1), wu_b, b_up.reshape(ne, 1, -1),
              wd_b, b_down.reshape(ne, 1, -1), MOE_GROUP, MOE_NSUB, MOE_ROWS)
    y_p = _ln2(h_p, ff, g2, b2, alpha, tm, 0)
    y_s = _ln2(h_s, ff, g2, b2, alpha, ns, t // ns)

    cw = cn_p[:, :, :DV_M].reshape(bp, H_M, DK_M, DV_M)
    nw = cn_p[:, :, DV_M].reshape(bp, H_M, DK_M)
    outs_p = (y_p.reshape(bp, seq, d), k_f.reshape(bp, seq, H_A, HEAD_W), v_f.reshape(bp, seq, H_A, HEAD_W),
              cw, nw, m_p[:, 0, :H_M])
    outs_s = (y_s.reshape(ns, 1, d), k_s.reshape(ns, 1, H_A, HEAD_W), v_s.reshape(ns, 1, H_A, HEAD_W),
              c_s, n_s.reshape(ns, H_M, DK_M), m_s[:, :, 0])
    return outs_p, outs_s


def kernel(x_prompt, x_sample, cache_k, cache_v, state_c, state_n, state_m, page_table, w_in, b_igate, b_fgate, lambda_q1, lambda_k1, lambda_q2, lambda_k2, subln_g, mh_norm_g, w_ba, w_bm, w_o, ln1_g, ln1_b, w_router, b_router, w_gate, b_gate, w_up, b_up, w_down, b_down, ln2_g, ln2_b):
    depth = w_in.shape[0]
    assert x_sample.shape[1] == 1, "the sample pass handles one new token per sequence"
    weights = (w_in, b_igate, b_fgate, lambda_q1, lambda_k1, lambda_q2, lambda_k2, subln_g, mh_norm_g,
               w_ba, w_bm, w_o, ln1_g, ln1_b, w_router, b_router, w_gate, b_gate, w_up, b_up,
               w_down, b_down, ln2_g, ln2_b)
    y_p, y_s = x_prompt, x_sample
    acc_p = [[] for _ in range(5)]
    acc_s = [[] for _ in range(5)]
    for l in range(depth):
        outs_p, outs_s = _layer(l, depth, y_p, y_s, cache_k, cache_v, state_c[l], state_n[l],
                                state_m[l], page_table, tuple(a[l] for a in weights))
        y_p, y_s = outs_p[0], outs_s[0]
        for i in range(5):
            acc_p[i].append(outs_p[1 + i])
            acc_s[i].append(outs_s[1 + i])
    return (y_p, y_s, *(jnp.stack(a) for a in acc_p), *(jnp.stack(a) for a in acc_s))
```

```python
import functools
import math

import jax
import jax.numpy as jnp
from jax import lax
from jax.experimental import pallas as pl
from jax.experimental.pallas import tpu as pltpu

F32 = jnp.float32
BF16 = jnp.bfloat16
HIGHEST = lax.Precision.HIGHEST

H_A = 4
D_HA = 64
ROT_DIM = D_HA // 4
ROPE_THETA = 500000.0
H_M = 4
DK_M = 64
DV_M = 128
GATE_SOFTCAP = 15.0
N_EXPERTS = 32
TOP_K = 4
SWIGLU_LIMIT = 7.0
SWIGLU_ALPHA = 1.702
LN_EPS = 1e-5
PAGE_SIZE = 128

QA_W = H_A * 2 * D_HA
QM_W = H_M * DK_M
VM_W = H_M * DV_M
HEAD_W = 2 * D_HA
N_GATE = 2 * H_M
W_A = 3 * QA_W + 2 * QM_W + VM_W

LANES = 128
VMEM_LIMIT = 56 * 1024 * 1024
SIDE_SLAB_BYTES = 2 * 1024 * 1024

ATTN_TQ = 256
ATTN_TK = 512
ATTN_HEADS_PER_STEP = 4
MLSTM_SEQS_PER_STEP = 2
MLSTM_CHUNK = 256
MOE_ROWS = 128
MOE_GROUP = 7 * LANES
MOE_NSUB = 2
MOE_SCATTER_EXPERTS = 8

NT_DIMS = (((1,), (1,)), ((), ()))
TN_DIMS = (((0,), (0,)), ((), ()))


def _params(*sem):
    return pltpu.CompilerParams(dimension_semantics=sem, vmem_limit_bytes=VMEM_LIMIT)


def _softcap(x):
    return GATE_SOFTCAP * jnp.tanh(x / GATE_SOFTCAP)


def _log_sigmoid(x):
    return jnp.minimum(x, 0.0) - jnp.log1p(jnp.exp(-jnp.abs(x)))


def _sigmoid(x):
    return 1.0 / (1.0 + jnp.exp(-x))


def _split_bf16(x):
    hi = x.astype(BF16)
    lo = (x - hi.astype(F32)).astype(BF16)
    return hi, lo


def _stack_split(x):
    hi, lo = _split_bf16(x)
    return jnp.concatenate([hi, lo], axis=0)


def _fold_split(t):
    n = t.shape[0] // 2
    return t[:n] + t[n:]


def _layer_norm(x, g, b):
    mu = jnp.mean(x, axis=-1, keepdims=True)
    xc = x - mu
    var = jnp.mean(xc * xc, axis=-1, keepdims=True)
    return xc * lax.rsqrt(var + LN_EPS) * g + b


def _rope(t, ra, rb, rc):
    return t * ra + pltpu.roll(t, 8, 1) * rb + pltpu.roll(t, HEAD_W - 8, 1) * rc


def _rope_tables(pos):
    inv = ROPE_THETA ** (-jnp.arange(0, ROT_DIM, 2, dtype=F32) / ROT_DIM)
    ang = pos.astype(F32)[:, None] * inv[None, :]
    cos, sin = jnp.cos(ang), jnp.sin(ang)
    n = pos.shape[0]
    half = ROT_DIM // 2
    rest = D_HA - ROT_DIM
    a = jnp.concatenate([cos, cos, jnp.ones((n, rest), F32)], axis=1)
    b = jnp.concatenate([jnp.zeros((n, half), F32), sin, jnp.zeros((n, rest), F32)], axis=1)
    c = jnp.concatenate([-sin, jnp.zeros((n, half + rest), F32)], axis=1)
    return tuple(jnp.tile(t, (1, 2)) for t in (a, b, c))


def _diff_lambda(lam_ref, lam_init):
    lp = lam_ref[...]
    s1 = jnp.sum(lp[0:1] * lp[1:2], axis=1, keepdims=True)
    s2 = jnp.sum(lp[2:3] * lp[3:4], axis=1, keepdims=True)
    return jnp.exp(s1) - jnp.exp(s2) + lam_init


def _proj_prompt_kernel(x_ref, wa_ref, wgc_ref, wgr_ref, wr_ref, ra_ref, rb_ref, rc_ref,
                        q_ref, kf_ref, kb_ref, vf_ref, vb_ref, qm_ref, km_ref, vm_ref,
                        gcol_ref, grow_ref, og_ref, ga_ref, gm_ref):
    x = x_ref[...].astype(BF16)
    tm = x.shape[0]
    za = jnp.dot(x, wa_ref[...], preferred_element_type=F32)
    ra, rb, rc = ra_ref[...], rb_ref[...], rc_ref[...]
    for h in range(H_A):
        sl = slice(h * HEAD_W, (h + 1) * HEAD_W)
        qh = _rope(za[:, sl], ra, rb, rc)
        q_ref[:, sl] = (qh * (D_HA ** -0.5)).astype(BF16)
        kh = _rope(za[:, QA_W + h * HEAD_W:QA_W + (h + 1) * HEAD_W], ra, rb, rc)
        kf_ref[pl.ds(h, tm, stride=H_A), :] = kh
        kb_ref[:, sl] = kh.astype(BF16)
        vf_ref[pl.ds(h, tm, stride=H_A), :] = za[:, 2 * QA_W + h * HEAD_W:2 * QA_W + (h + 1) * HEAD_W]
    vb_ref[...] = za[:, 2 * QA_W:3 * QA_W].astype(BF16)
    o = 3 * QA_W
    qm_ref[...] = za[:, o:o + QM_W].astype(BF16)
    km_ref[...] = (za[:, o + QM_W:o + 2 * QM_W] * (DK_M ** -0.5)).astype(BF16)
    vm_ref[...] = za[:, o + 2 * QM_W:].astype(BF16)
    gcol_ref[...] = jnp.dot(x, wgc_ref[...], preferred_element_type=F32)
    grow_ref[...] = lax.dot_general(wgr_ref[...], x, NT_DIMS, preferred_element_type=F32)
    zr = jnp.dot(x, wr_ref[...], preferred_element_type=F32)
    d = ga_ref.shape[1]
    og_ref[...] = zr[:, :VM_W].astype(BF16)
    ga_ref[...] = zr[:, VM_W:VM_W + d].astype(BF16)
    gm_ref[...] = zr[:, VM_W + d:].astype(BF16)


def _proj_prompt(x2, wa, wgc, wgr, wr, tabs, seq, tm):
    t, d = x2.shape
    nrep = seq // tm
    row = lambda w: pl.BlockSpec((tm, w), lambda i: (i, 0))
    full = lambda a: pl.BlockSpec(a.shape, lambda i: (0,) * a.ndim)
    tab = pl.BlockSpec((tm, HEAD_W), lambda i: (i % nrep, 0))
    sds = lambda w, dt: jax.ShapeDtypeStruct((t, w), dt)
    kv_spec = pl.BlockSpec((tm * H_A, HEAD_W), lambda i: (i, 0))
    kv_sds = jax.ShapeDtypeStruct((t * H_A, HEAD_W), F32)
    return pl.pallas_call(
        _proj_prompt_kernel,
        grid=(t // tm,),
        in_specs=[row(d), full(wa), full(wgc), full(wgr), full(wr), tab, tab, tab],
        out_specs=[row(QA_W), kv_spec, row(QA_W), kv_spec, row(QA_W),
                   row(QM_W), row(QM_W), row(VM_W), row(LANES),
                   pl.BlockSpec((N_GATE, tm), lambda i: (0, i)),
                   row(VM_W), row(d), row(d)],
        out_shape=[sds(QA_W, BF16), kv_sds, sds(QA_W, BF16), kv_sds, sds(QA_W, BF16),
                   sds(QM_W, BF16), sds(QM_W, BF16), sds(VM_W, BF16), sds(LANES, F32),
                   jax.ShapeDtypeStruct((N_GATE, t), F32),
                   sds(VM_W, BF16), sds(d, BF16), sds(d, BF16)],
        compiler_params=_params("parallel"),
        name="proj_prompt",
    )(x2, wa, wgc, wgr, wr, *tabs)


def _attn_prompt_kernel(q_ref, k_ref, v_ref, lam_ref, g_ref, *rest, tq, tk, hps, lam_init):
    nside = (len(rest) - 1) // 2
    o_ref = rest[nside]
    for src, dst in zip(rest[:nside], rest[nside + 1:]):
        dst[...] = src[...].astype(dst.dtype)
    qi = pl.program_id(2)
    nfull = (qi * tq) // tk
    lane = lax.broadcasted_iota(jnp.int32, (1, HEAD_W), 1)
    r = lax.broadcasted_iota(jnp.int32, (tq, tk), 0) + qi * tq
    c = lax.broadcasted_iota(jnp.int32, (tq, tk), 1) + nfull * tk
    causal = jnp.concatenate([c <= r, c <= r], axis=0)
    ones_blk = jnp.broadcast_to(jnp.where(lane == 0, 1.0, 0.0).astype(BF16), (tk, HEAD_W))
    qqs = []
    for hh in range(hps):
        q = q_ref[:, hh * HEAD_W:(hh + 1) * HEAD_W]
        zero = jnp.zeros_like(q)
        qqs.append(jnp.concatenate([jnp.where(lane < D_HA, q, zero), jnp.where(lane >= D_HA, q, zero)],
                                   axis=0))

    def step(j, carry, masked):
        off = pl.multiple_of(j * tk, tk)
        out = []
        for hh in range(hps):
            m, acc = carry[hh]
            sl = slice(hh * HEAD_W, (hh + 1) * HEAD_W)
            k = k_ref[pl.ds(off, tk), sl]
            vext = jnp.concatenate([v_ref[pl.ds(off, tk), sl], ones_blk], axis=1)
            s = lax.dot_general(qqs[hh], k, NT_DIMS, preferred_element_type=F32)
            if masked:
                s = jnp.where(causal, s, -jnp.inf)
            m_new = jnp.maximum(m, jnp.max(s, axis=1, keepdims=True))
            p = jnp.exp(s - m_new)
            acc = jnp.exp(m - m_new) * acc + jnp.dot(p.astype(BF16), vext, preferred_element_type=F32)
            out.append((m_new, acc))
        return tuple(out)

    init = tuple((jnp.full((2 * tq, 1), -jnp.inf, F32), jnp.zeros((2 * tq, 2 * HEAD_W), F32))
                 for _ in range(hps))
    carry = lax.fori_loop(0, nfull, lambda j, cr: step(j, cr, False), init)
    carry = step(nfull, carry, True)
    lam = _diff_lambda(lam_ref, lam_init)
    for hh in range(hps):
        acc = carry[hh][1]
        o1 = acc[:tq, :HEAD_W] / acc[:tq, HEAD_W:HEAD_W + 1]
        o2 = acc[tq:, :HEAD_W] / acc[tq:, HEAD_W:HEAD_W + 1]
        o = o1 - lam * o2
        y = o * lax.rsqrt(jnp.mean(o * o, axis=1, keepdims=True) + LN_EPS) * g_ref[...] * (1.0 - lam_init)
        o_ref[:, hh * HEAD_W:(hh + 1) * HEAD_W] = y.astype(o_ref.dtype)


def _attn_prompt(q, k, v, lam_p, subln_g, batch, seq, lam_init, side):
    t = q.shape[0]
    tk = _pick_tile(seq, ATTN_TK)
    tq = _pick_tile(tk, ATTN_TQ)
    nq = seq // tq
    hps = ATTN_HEADS_PER_STEP
    nh = H_A // hps
    w = hps * HEAD_W
    steps = batch * nh * nq
    side2 = [a.reshape(-1, a.shape[-1]) for a in side]
    slab = [a.shape[0] // steps for a in side2]
    if any(a.shape[0] % steps or s % 16 or s * a.shape[1] * 4 > SIDE_SLAB_BYTES for a, s in zip(side2, slab)):
        side2, slab = [], []
    side_specs = [pl.BlockSpec((s, a.shape[1]), lambda b, h, i: ((b * nh + h) * nq + i, 0))
                  for a, s in zip(side2, slab)]
    outs = pl.pallas_call(
        functools.partial(_attn_prompt_kernel, tq=tq, tk=tk, hps=hps, lam_init=lam_init),
        grid=(batch, nh, nq),
        in_specs=[pl.BlockSpec((tq, w), lambda b, h, i: (b * nq + i, h)),
                  pl.BlockSpec((seq, w), lambda b, h, i: (b, h)),
                  pl.BlockSpec((seq, w), lambda b, h, i: (b, h)),
                  pl.BlockSpec(lam_p.shape, lambda b, h, i: (0, 0)),
                  pl.BlockSpec((1, HEAD_W), lambda b, h, i: (0, 0))] + side_specs,
        out_specs=[pl.BlockSpec((tq, w), lambda b, h, i: (b * nq + i, h))] + side_specs,
        out_shape=[jax.ShapeDtypeStruct((t, QA_W), BF16)]
        + [jax.ShapeDtypeStruct(a.shape, BF16) for a in side2],
        compiler_params=_params("parallel", "parallel", "parallel"),
        name="attn_prompt",
    )(q, k, v, lam_p, subln_g, *side2)
    if side2:
        return outs[0], [o.reshape(a.shape) for o, a in zip(outs[1:], side)]
    return outs[0], [a.astype(BF16) for a in side]


def _mlstm_prompt_kernel(q_ref, k_ref, v_ref, gcol_ref, grow_ref, bcol_ref, brow_ref, og_ref, g_ref,
                         min_ref, cn_ref, m_ref, *, lc, bps):
    ci = pl.program_id(1)

    @pl.when(ci == 0)
    def _():
        cn_ref[...] = jnp.zeros_like(cn_ref)
        m_ref[...] = jnp.zeros_like(m_ref)

    for i in range(bps):
        _mlstm_chunk(q_ref.at[0, i], k_ref.at[0, i], v_ref.at[0, i], gcol_ref.at[0, i], grow_ref[:, 0, i, :],
                     bcol_ref, brow_ref, og_ref.at[0, i], g_ref, min_ref.at[0, i], cn_ref.at[0, i],
                     m_ref.at[0, i], lc)


def _mlstm_chunk(q_ref, k_ref, v_ref, gcol_ref, grow, bcol_ref, brow_ref, og_ref, g_ref,
                 min_ref, cn_ref, m_ref, lc):
    gc = _softcap(gcol_ref[...] + bcol_ref[...])
    lfc = _log_sigmoid(gc)
    gr = _softcap(grow + brow_ref[...])
    lfr = _log_sigmoid(gr)
    ri = lax.broadcasted_iota(jnp.int32, (lc, lc), 0)
    cj = lax.broadcasted_iota(jnp.int32, (lc, lc), 1)
    tril = cj <= ri
    b_col = jnp.dot(tril.astype(F32), lfc, precision=HIGHEST, preferred_element_type=F32)
    b_row = jnp.dot(lfr, (ri <= cj).astype(F32), precision=HIGHEST, preferred_element_type=F32)

    cn = cn_ref[...]
    cn_bf = cn.astype(BF16)
    q = q_ref[...]
    k = k_ref[...]
    v = v_ref[...]
    m_all = m_ref[...]
    lane_q = lax.broadcasted_iota(jnp.int32, (1, QM_W), 1) // DK_M
    row_q = lax.broadcasted_iota(jnp.int32, (QM_W, 1), 0) // DK_M
    lane_v = lax.broadcasted_iota(jnp.int32, (1, LANES), 1)
    ones_blk = jnp.broadcast_to(jnp.where(lane_v == 0, 1.0, 0.0).astype(BF16), (lc, LANES))
    new_cn = jnp.zeros_like(cn)
    new_m = m_all
    for h in range(H_M):
        sl = slice(h * DV_M, (h + 1) * DV_M)
        qh = jnp.where(lane_q == h, q, jnp.zeros_like(q))
        kh = jnp.where(lane_q == h, k, jnp.zeros_like(k))
        b_c = b_col[:, H_M + h:H_M + h + 1]
        ig_c = gc[:, h:h + 1]
        b_r = b_row[H_M + h:H_M + h + 1, :]
        ig_r = gr[h:h + 1, :]
        m_prev = m_all[:, h:h + 1]
        dmat = jnp.where(tril, b_c - b_r + ig_r, -jnp.inf)
        m_inter = b_c + m_prev
        m_t = jnp.maximum(m_inter, jnp.max(dmat, axis=1, keepdims=True))
        w_inter = jnp.exp(m_inter - m_t)
        s = lax.dot_general(qh, kh, NT_DIMS, preferred_element_type=F32) * jnp.exp(dmat - m_t)
        vext = jnp.concatenate([v[:, sl], ones_blk], axis=1)
        nd = (w_inter * jnp.dot(qh, cn_bf, preferred_element_type=F32)
              + jnp.dot(s.astype(BF16), vext, preferred_element_type=F32))
        num = nd[:, :DV_M]
        den = nd[:, DV_M:DV_M + 1]
        hout = num / jnp.maximum(jnp.abs(den), jnp.exp(-m_t))
        mu = jnp.mean(hout, axis=1, keepdims=True)
        hc = hout - mu
        var = jnp.mean(hc * hc, axis=1, keepdims=True)
        y = hc * lax.rsqrt(var + LN_EPS) * g_ref[:, sl] * _sigmoid(og_ref[:, sl].astype(F32))
        min_ref[:, sl] = y.astype(min_ref.dtype)
        m_new = m_t[lc - 1:lc, :]
        b_last = b_c[lc - 1:lc, :]
        decay = jnp.exp(b_last + m_prev - m_new)
        wk = jnp.exp(b_last - b_c + ig_c - m_new)
        upd = lax.dot_general(kh, (wk * vext.astype(F32)).astype(BF16), TN_DIMS,
                              preferred_element_type=F32)
        new_cn = new_cn + jnp.where(row_q == h, decay * cn, 0.0) + upd
        new_m = jnp.where(lane_v == h, m_new, new_m)
    cn_ref[...] = new_cn
    m_ref[...] = new_m


def _mlstm_prompt(qm, km, vm, gcol, grow, bcol, brow, og, mh_g, batch, seq, lc):
    t = qm.shape[0]
    bps = MLSTM_SEQS_PER_STEP if batch % MLSTM_SEQS_PER_STEP == 0 else 1
    nb = batch // bps
    row = lambda w: pl.BlockSpec((1, bps, lc, w), lambda b, c: (b, 0, c, 0))
    const = lambda a: pl.BlockSpec(a.shape, lambda b, c: (0,) * a.ndim)
    seqs = lambda a: a.reshape(nb, bps, seq, a.shape[1])
    cw = 2 * LANES
    m_in, cn, m = pl.pallas_call(
        functools.partial(_mlstm_prompt_kernel, lc=lc, bps=bps),
        grid=(nb, seq // lc),
        in_specs=[row(QM_W), row(QM_W), row(VM_W), row(LANES),
                  pl.BlockSpec((N_GATE, 1, bps, lc), lambda b, c: (0, b, 0, c)),
                  const(bcol), const(brow), row(VM_W), const(mh_g)],
        out_specs=[row(VM_W),
                   pl.BlockSpec((1, bps, QM_W, cw), lambda b, c: (b, 0, 0, 0)),
                   pl.BlockSpec((1, bps, 1, LANES), lambda b, c: (b, 0, 0, 0))],
        out_shape=[jax.ShapeDtypeStruct((nb, bps, seq, VM_W), BF16),
                   jax.ShapeDtypeStruct((nb, bps, QM_W, cw), F32),
                   jax.ShapeDtypeStruct((nb, bps, 1, LANES), F32)],
        compiler_params=_params("parallel", "arbitrary"),
        name="mlstm_prompt",
    )(seqs(qm), seqs(km), seqs(vm), seqs(gcol), grow.reshape(N_GATE, nb, bps, seq), bcol, brow,
      seqs(og), mh_g)
    return m_in.reshape(t, VM_W), cn.reshape(batch, QM_W, cw), m.reshape(batch, 1, LANES)


def _mix_kernel(a_ref, mi_ref, ga_ref, gm_ref, x_ref, wba_ref, wbm_ref, wo_ref, g1_ref, b1_ref,
                wrt_ref, brt_ref, h_ref, hb_ref, comb_ref, mask_ref, *, alpha, precise):
    def mm(a, w_ref):
        if precise:
            return jnp.dot(a.astype(F32), w_ref[...], precision=HIGHEST, preferred_element_type=F32)
        return jnp.dot(a.astype(BF16), w_ref[...], preferred_element_type=F32)

    a_br = mm(a_ref[...], wba_ref)
    m_br = mm(mi_ref[...], wbm_ref)
    merged = _sigmoid(ga_ref[...].astype(F32)) * a_br + _sigmoid(gm_ref[...].astype(F32)) * m_br
    mix = mm(merged, wo_ref)
    h = _layer_norm(alpha * x_ref[...] + mix, g1_ref[...], b1_ref[...])
    h_ref[...] = h
    hb_ref[...] = h.astype(BF16)

    ne = wrt_ref.shape[0]
    if precise:
        logits = lax.dot_general(wrt_ref[...], h, NT_DIMS, precision=HIGHEST, preferred_element_type=F32)
    else:
        ws = _stack_split(wrt_ref[...])
        h_hi, h_lo = _split_bf16(h)
        logits = (_fold_split(lax.dot_general(ws, h_hi, NT_DIMS, preferred_element_type=F32))
                  + lax.dot_general(ws[:ne], h_lo, NT_DIMS, preferred_element_type=F32))
    logits = logits + brt_ref[...]
    tm = logits.shape[1]
    eidx = lax.broadcasted_iota(jnp.int32, (ne, tm), 0)
    work = logits
    sel = jnp.zeros((ne, tm), jnp.bool_)
    top = None
    for _ in range(TOP_K):
        mx = jnp.max(work, axis=0, keepdims=True)
        top = mx if top is None else top
        first = jnp.min(jnp.where(work == mx, eidx, ne), axis=0, keepdims=True)
        pick = eidx == first
        sel = jnp.logical_or(sel, pick)
        work = jnp.where(pick, -jnp.inf, work)
    ex = jnp.where(sel, jnp.exp(logits - top), 0.0)
    comb_ref[...] = ex / jnp.sum(ex, axis=0, keepdims=True)
    mask_ref[...] = jnp.where(sel, 1.0, 0.0)


def _mix(a_in, m_in, ga, gm, x2, wba, wbm, wo, g1, b1, wrt, brt, alpha, precise, tm, hb_rows=None):
    t, d = x2.shape
    hb_rows = t if hb_rows is None else hb_rows
    ne = wrt.shape[0]
    row = lambda w: pl.BlockSpec((tm, w), lambda i: (i, 0))
    col = pl.BlockSpec((ne, tm), lambda i: (0, i))
    const = lambda a: pl.BlockSpec(a.shape, lambda i: (0,) * a.ndim)
    return pl.pallas_call(
        functools.partial(_mix_kernel, alpha=alpha, precise=precise),
        grid=(t // tm,),
        in_specs=[row(QA_W), row(VM_W), row(d), row(d), row(d), const(wba), const(wbm), const(wo),
                  const(g1), const(b1), const(wrt), const(brt)],
        out_specs=[row(d), row(d), col, col],
        out_shape=[jax.ShapeDtypeStruct((t, d), F32), jax.ShapeDtypeStruct((hb_rows, d), BF16),
                   jax.ShapeDtypeStruct((ne, t), F32), jax.ShapeDtypeStruct((ne, t), F32)],
        compiler_params=_params("parallel"),
        name="mix_sample" if precise else "mix_prompt",
    )(a_in, m_in, ga, gm, x2, wba, wbm, wo, g1, b1, wrt, brt)


def _append_rows_kernel(big_ref, tail_ref, o_ref):
    del big_ref
    o_ref[...] = jnp.zeros_like(o_ref)

    @pl.when(pl.program_id(0) == 0)
    def _():
        o_ref[:tail_ref.shape[0], :] = tail_ref[...]


def _append_rows(big, tail, start):
    rows, d = big.shape
    blk = math.gcd(start, rows - start)
    assert blk % 16 == 0 and blk >= tail.shape[0]
    return pl.pallas_call(
        _append_rows_kernel,
        grid=((rows - start) // blk,),
        in_specs=[pl.BlockSpec(memory_space=pl.ANY), pl.BlockSpec(tail.shape, lambda i: (0, 0))],
        out_specs=pl.BlockSpec((blk, d), lambda i: (start // blk + i, 0)),
        out_shape=jax.ShapeDtypeStruct(big.shape, big.dtype),
        input_output_aliases={0: 0},
        compiler_params=_params("arbitrary"),
        name="append_rows",
    )(big, tail)


def _route_kernel(mask_ref, pos_ref, cnt_ref):
    maskf = mask_ref[...]
    g = maskf.shape[1]
    r = lax.broadcasted_iota(jnp.int32, (g, g), 0)
    c = lax.broadcasted_iota(jnp.int32, (g, g), 1)
    before = jnp.dot(maskf.astype(BF16), (r < c).astype(BF16), preferred_element_type=F32)
    pos_ref[...] = jnp.where(maskf > 0.0, before, -1.0)
    cnt_ref[0] = jnp.broadcast_to(jnp.sum(maskf, axis=1, keepdims=True), cnt_ref.shape[1:])


def _route(mask, group):
    ne, t = mask.shape
    ng = t // group
    return pl.pallas_call(
        _route_kernel,
        grid=(ng,),
        in_specs=[pl.BlockSpec((ne, group), lambda g: (0, g))],
        out_specs=[pl.BlockSpec((ne, group), lambda g: (0, g)),
                   pl.BlockSpec((1, ne, LANES), lambda g: (g, 0, 0))],
        out_shape=[jax.ShapeDtypeStruct((ne, t), F32), jax.ShapeDtypeStruct((ng, ne, LANES), F32)],
        compiler_params=_params("parallel"),
        name="route",
    )(mask)


def _moe_kernel(cnt_ref, nact_ref, h_ref, pos_ref, comb_ref, wg_ref, bg_ref, wu_ref, bu_ref,
                wd_ref, bd_ref, o_ref, y_scr, *, rows, group, nsub):
    sg = pl.program_id(0)
    e = pl.program_id(1)
    ne = pl.num_programs(1)
    n_exp = pos_ref.shape[0]

    @pl.when(e == 0)
    def _():
        o_ref[...] = jnp.zeros_like(o_ref)

    slot0 = lax.broadcasted_iota(jnp.int32, (rows, 1), 0).astype(F32)

    def ffn(hit, tok):
        xg = jnp.dot(jnp.where(hit, 1.0, 0.0).astype(BF16), h_ref[tok, :],
                     preferred_element_type=F32).astype(BF16)
        gate = jnp.minimum(jnp.dot(xg, wg_ref[0], preferred_element_type=F32) + bg_ref[0], SWIGLU_LIMIT)
        up = jnp.clip(jnp.dot(xg, wu_ref[0], preferred_element_type=F32) + bu_ref[0],
                      -SWIGLU_LIMIT, SWIGLU_LIMIT)
        hid = gate * _sigmoid(SWIGLU_ALPHA * gate) * (up + 1.0)
        return (jnp.dot(hid.astype(BF16), wd_ref[0], preferred_element_type=F32) + bd_ref[0]).astype(BF16)

    yrows = pl.ds(pl.multiple_of(e * rows, rows), rows)

    def first_tile(sub):
        tok = slice(sub * group, (sub + 1) * group)
        y_scr[sub, yrows, :] = ffn(pos_ref[pl.ds(e, 1), tok] == slot0, tok)

    nact = nact_ref[sg]

    @pl.when(nact == nsub)
    def _():
        for sub in range(nsub):
            first_tile(sub)

    @pl.when(nact < nsub)
    def _():
        for sub in range(nsub):
            pl.when(sub < nact)(functools.partial(first_tile, sub))

            @pl.when(sub >= nact)
            def _(sub=sub):
                y_scr[sub, yrows, :] = jnp.zeros((rows, y_scr.shape[2]), y_scr.dtype)

    for sub in range(nsub):
        tok = slice(sub * group, (sub + 1) * group)
        n = cnt_ref[(sg * nsub + sub) * ne + e]
        pos = pos_ref[pl.ds(e, 1), tok]
        cw = comb_ref[pl.ds(e, 1), tok]

        def tile(s, carry, tok=tok, pos=pos, cw=cw):
            hit = pos == slot0 + (s * rows).astype(F32)
            gw = jnp.where(hit, cw, 0.0).astype(BF16)
            o_ref[tok, :] += lax.dot_general(gw, ffn(hit, tok), TN_DIMS, preferred_element_type=F32)
            return carry

        lax.fori_loop(1, (n + rows - 1) // rows, tile, 0)

    @pl.when(e == ne - 1)
    def _():
        for sub in range(nsub):
            tok = slice(sub * group, (sub + 1) * group)
            for c0 in range(0, n_exp, MOE_SCATTER_EXPERTS):
                gw = jnp.concatenate(
                    [jnp.where(pos_ref[ee:ee + 1, tok] == slot0, comb_ref[ee:ee + 1, tok], 0.0).astype(BF16)
                     for ee in range(c0, c0 + MOE_SCATTER_EXPERTS)], axis=0)
                ys = y_scr[sub, c0 * rows:(c0 + MOE_SCATTER_EXPERTS) * rows, :]
                o_ref[tok, :] += lax.dot_general(gw, ys, TN_DIMS, preferred_element_type=F32)


def _moe(counts, nact, hb, pos, comb, wg, bg, wu, bu, wd, bd, group, nsub, rows):
    t, d = hb.shape
    ne = wg.shape[0]
    sgroup = group * nsub
    tok = pl.BlockSpec((ne, sgroup), lambda g, e, c, a: (0, g))
    wspec = lambda w: pl.BlockSpec((1,) + w.shape[1:], lambda g, e, c, a: (e, 0, 0))
    grid_spec = pltpu.PrefetchScalarGridSpec(
        num_scalar_prefetch=2,
        grid=(t // sgroup, ne),
        in_specs=[pl.BlockSpec((sgroup, d), lambda g, e, c, a: (g, 0)), tok, tok,
                  wspec(wg), wspec(bg), wspec(wu), wspec(bu), wspec(wd), wspec(bd)],
        out_specs=pl.BlockSpec((sgroup, d), lambda g, e, c, a: (g, 0)),
        scratch_shapes=[pltpu.VMEM((nsub, ne * rows, d), BF16)],
    )
    return pl.pallas_call(
        functools.partial(_moe_kernel, rows=rows, group=group, nsub=nsub),
        grid_spec=grid_spec,
        out_shape=jax.ShapeDtypeStruct((t, d), F32),
        compiler_params=_params("parallel", "arbitrary"),
        name="moe",
    )(counts, nact, hb, pos, comb, wg, bg, wu, bu, wd, bd)


def _ln2_kernel(h_ref, ff_ref, g_ref, b_ref, y_ref, *, alpha):
    y_ref[...] = _layer_norm(alpha * h_ref[...] + ff_ref[...], g_ref[...], b_ref[...])


def _ln2(h, ff, g2, b2, alpha, tm, ff_block0):
    t, d = h.shape
    const = lambda a: pl.BlockSpec(a.shape, lambda i: (0,) * a.ndim)
    return pl.pallas_call(
        functools.partial(_ln2_kernel, alpha=alpha),
        grid=(t // tm,),
        in_specs=[pl.BlockSpec((tm, d), lambda i: (i, 0)),
                  pl.BlockSpec((tm, d), lambda i: (ff_block0 + i, 0)), const(g2), const(b2)],
        out_specs=pl.BlockSpec((tm, d), lambda i: (i, 0)),
        out_shape=jax.ShapeDtypeStruct((t, d), F32),
        compiler_params=_params("parallel"),
        name="ln2",
    )(h, ff, g2, b2)


def _proj_sample_kernel(x_ref, w_ref, z_ref):
    z_ref[...] = jnp.dot(x_ref[...], w_ref[...], precision=HIGHEST, preferred_element_type=F32)


def _proj_sample(xs, w, chunk):
    n, d = xs.shape
    width = w.shape[1]
    return pl.pallas_call(
        _proj_sample_kernel,
        grid=(width // chunk,),
        in_specs=[pl.BlockSpec((n, d), lambda j: (0, 0)), pl.BlockSpec((d, chunk), lambda j: (0, j))],
        out_specs=pl.BlockSpec((n, chunk), lambda j: (0, j)),
        out_shape=jax.ShapeDtypeStruct((n, width), F32),
        compiler_params=_params("parallel"),
        name="proj_sample",
    )(xs, w)


def _paged_attn_kernel(pt_ref, z_ref, ra_ref, rb_ref, rc_ref, lam_ref, g_ref, *rest,
                       ppb, lam_init):
    k_refs = rest[:ppb]
    v_refs = rest[ppb:2 * ppb]
    kout_ref, vout_ref, a_ref, q_scr, qs_scr, m_scr, l_scr, acc_scr = rest[2 * ppb:]
    j = pl.program_id(1)
    nj = pl.num_programs(1)
    nrow = 2 * H_A
    prow = PAGE_SIZE * H_A
    ra, rb, rc = ra_ref[...], rb_ref[...], rc_ref[...]
    lane = lax.broadcasted_iota(jnp.int32, (1, HEAD_W), 1)

    def per_row(t):
        return jnp.concatenate([t[:, (r // 2) * HEAD_W:(r // 2 + 1) * HEAD_W] for r in range(nrow)], axis=0)

    @pl.when(j == 0)
    def _():
        z = z_ref[0]
        rows = []
        for h in range(H_A):
            qh = _rope(z[:, h * HEAD_W:(h + 1) * HEAD_W], ra, rb, rc) * (D_HA ** -0.5)
            rows += [jnp.where(lane < D_HA, qh, 0.0), jnp.where(lane >= D_HA, qh, 0.0)]
        q = jnp.concatenate(rows, axis=0)
        q_scr[...] = q
        qs_scr[...] = _stack_split(q)
        m_scr[...] = jnp.full_like(m_scr, -jnp.inf)
        l_scr[...] = jnp.zeros_like(l_scr)
        acc_scr[...] = jnp.zeros_like(acc_scr)

    qs = qs_scr[...]
    parts = []
    for kr in k_refs:
        k_hi, k_lo = _split_bf16(kr[0])
        t = (lax.dot_general(qs, k_hi, NT_DIMS, preferred_element_type=F32)
             + lax.dot_general(qs, k_lo, NT_DIMS, preferred_element_type=F32))
        parts.append(_fold_split(t))
    s = jnp.concatenate(parts, axis=1)
    r_head = lax.broadcasted_iota(jnp.int32, s.shape, 0) // 2
    c_head = lax.broadcasted_iota(jnp.int32, s.shape, 1) % H_A
    s = jnp.where(r_head == c_head, s, -jnp.inf)
    m_old = m_scr[...]
    m_new = jnp.maximum(m_old, jnp.max(s, axis=1, keepdims=True))
    alpha = jnp.exp(m_old - m_new)
    p = jnp.exp(s - m_new)
    l_scr[...] = alpha * l_scr[...] + jnp.sum(p, axis=1, keepdims=True)
    ps = _stack_split(p)
    pv = None
    for i, vr in enumerate(v_refs):
        v_hi, v_lo = _split_bf16(vr[0])
        pi = ps[:, i * prow:(i + 1) * prow]
        t = jnp.dot(pi, v_hi, preferred_element_type=F32) + jnp.dot(pi, v_lo, preferred_element_type=F32)
        pv = t if pv is None else pv + t
    acc_scr[...] = alpha * acc_scr[...] + _fold_split(pv)
    m_scr[...] = m_new

    @pl.when(j == nj - 1)
    def _():
        z = z_ref[0]
        k_new = jnp.concatenate([_rope(z[:, QA_W + h * HEAD_W:QA_W + (h + 1) * HEAD_W], ra, rb, rc)
                                 for h in range(H_A)], axis=1)
        v_new = z[:, 2 * QA_W:3 * QA_W]
        kout_ref[0] = k_new
        vout_ref[0] = v_new
        s_self = jnp.sum(q_scr[...] * per_row(k_new), axis=1, keepdims=True)
        m_old2 = m_scr[...]
        m_fin = jnp.maximum(m_old2, s_self)
        a2 = jnp.exp(m_old2 - m_fin)
        p_self = jnp.exp(s_self - m_fin)
        l_fin = a2 * l_scr[...] + p_self
        o_all = (a2 * acc_scr[...] + p_self * per_row(v_new)) / l_fin
        lam = _diff_lambda(lam_ref, lam_init)
        outs = []
        for h in range(H_A):
            o = o_all[2 * h:2 * h + 1] - lam * o_all[2 * h + 1:2 * h + 2]
            outs.append(o * lax.rsqrt(jnp.mean(o * o, axis=1, keepdims=True) + LN_EPS)
                        * g_ref[...] * (1.0 - lam_init))
        a_ref[0] = jnp.concatenate(outs, axis=1)


def _paged_attn(page_table, z3, tabs, lam_p, subln_g, ck, cv, ppb, lam_init):
    nb, npages = page_table.shape
    width = z3.shape[2]
    nj = npages // ppb
    pt = page_table.reshape(-1)
    prow = PAGE_SIZE * H_A
    nrow = 2 * H_A
    const = lambda a: pl.BlockSpec(a.shape, lambda b, j, p: (0,) * a.ndim)
    page = lambda i: pl.BlockSpec((1, prow, HEAD_W),
                                  lambda b, j, p: (p[b * npages + j * ppb + i], 0, 0))
    seq3 = lambda w: pl.BlockSpec((1, 1, w), lambda b, j, p: (b, 0, 0))
    grid_spec = pltpu.PrefetchScalarGridSpec(
        num_scalar_prefetch=1,
        grid=(nb, nj),
        in_specs=[seq3(width), const(tabs[0]), const(tabs[1]), const(tabs[2]), const(lam_p),
                  const(subln_g)] + [page(i) for i in range(ppb)] + [page(i) for i in range(ppb)],
        out_specs=[seq3(QA_W), seq3(QA_W), seq3(QA_W)],
        scratch_shapes=[pltpu.VMEM((nrow, HEAD_W), F32), pltpu.VMEM((2 * nrow, HEAD_W), BF16),
                        pltpu.VMEM((nrow, 1), F32), pltpu.VMEM((nrow, 1), F32),
                        pltpu.VMEM((nrow, HEAD_W), F32)],
    )
    sds = jax.ShapeDtypeStruct((nb, 1, QA_W), F32)
    return pl.pallas_call(
        functools.partial(_paged_attn_kernel, ppb=ppb, lam_init=lam_init),
        grid_spec=grid_spec,
        out_shape=[sds, sds, sds],
        compiler_params=_params("parallel", "arbitrary"),
        name="paged_attn",
    )(pt, z3, *tabs, lam_p, subln_g, *([ck] * ppb), *([cv] * ppb))


def _mlstm_step_kernel(gates_ref, m0_ref, big_ref, bfg_ref, qc_ref, kc_ref, v_ref, og_ref, c0_ref,
                       n0_ref, g_ref, min_ref, c_ref, n_ref, m_ref):
    spb = c0_ref.shape[0]
    for i in range(spb):
        _mlstm_step_one(pl.program_id(0) * spb + i, i, gates_ref, m0_ref, big_ref, bfg_ref, qc_ref, kc_ref,
                        v_ref, og_ref, c0_ref, n0_ref, g_ref, min_ref, c_ref, n_ref, m_ref)


def _mlstm_step_one(b, i, gates_ref, m0_ref, big_ref, bfg_ref, qc_ref, kc_ref, v_ref, og_ref, c0_ref,
                    n0_ref, g_ref, min_ref, c_ref, n_ref, m_ref):
    outs = []
    for h in range(H_M):
        sl = slice(h * DV_M, (h + 1) * DV_M)
        ig = _softcap(jnp.full((1, LANES), gates_ref[b, h] + big_ref[h], F32))
        lf = _log_sigmoid(_softcap(jnp.full((1, LANES), gates_ref[b, H_M + h] + bfg_ref[h], F32)))
        m0 = jnp.full((1, LANES), m0_ref[b, h], F32)
        m_inter = lf + m0
        m_t = jnp.maximum(m_inter, ig)
        w_inter = jnp.exp(m_inter - m_t)
        w_new = jnp.exp(ig - m_t)
        qc = qc_ref[i, h]
        kc = kc_ref[i, h] * (DK_M ** -0.5)
        vr = v_ref[i][:, sl]
        c0 = c0_ref[i, h]
        n0 = n0_ref[i, h]
        qk = jnp.sum(qc * kc, axis=0, keepdims=True)
        s = qk * w_new
        num = w_inter * jnp.sum(qc * c0, axis=0, keepdims=True) + s * vr
        den = w_inter * jnp.sum(qc * n0, axis=0, keepdims=True) + s
        hout = num / jnp.maximum(jnp.abs(den), jnp.exp(-m_t))
        mu = jnp.mean(hout, axis=1, keepdims=True)
        hc = hout - mu
        var = jnp.mean(hc * hc, axis=1, keepdims=True)
        outs.append(hc * lax.rsqrt(var + LN_EPS) * g_ref[:, sl] * _sigmoid(og_ref[i][:, sl]))
        c_ref[i, h] = w_inter * c0 + (w_new * kc) * vr
        n_ref[i, h] = w_inter[:, 0:1] * n0 + w_new[:, 0:1] * kc
        m_ref[i, h:h + 1, :] = m_t
    min_ref[i] = jnp.concatenate(outs, axis=1)


def _mlstm_step(gates, m0, big, bfg, qc, kc, v3, og3, c0, n0c, mh_g):
    nb = c0.shape[0]
    spb = _pick_tile(nb, 8)
    smem = pl.BlockSpec(memory_space=pltpu.SMEM)
    per = lambda a: pl.BlockSpec((spb,) + a.shape[1:], lambda b: (b,) + (0,) * (a.ndim - 1))
    const = lambda a: pl.BlockSpec(a.shape, lambda b: (0,) * a.ndim)
    return pl.pallas_call(
        _mlstm_step_kernel,
        grid=(nb // spb,),
        in_specs=[smem, smem, smem, smem, per(qc), per(kc), per(v3), per(og3), per(c0), per(n0c),
                  const(mh_g)],
        out_specs=[per(v3), per(c0), per(n0c), pl.BlockSpec((spb, H_M, LANES), lambda b: (b, 0, 0))],
        out_shape=[jax.ShapeDtypeStruct(v3.shape, F32), jax.ShapeDtypeStruct(c0.shape, F32),
                   jax.ShapeDtypeStruct(n0c.shape, F32),
                   jax.ShapeDtypeStruct((nb, H_M, LANES), F32)],
        compiler_params=_params("parallel"),
        name="mlstm_step",
    )(gates, m0, big, bfg, qc, kc, v3, og3, c0, n0c, mh_g)


def _pick_tile(n, target):
    t = min(n, target)
    while n % t:
        t //= 2
    return t


def _layer(l, depth, x_p, x_s, cache_k, cache_v, c0, n0, m0, page_table, w):
    (w_in, b_ig, b_fg, lq1, lk1, lq2, lk2, subln_g, mh_g, w_ba, w_bm, w_o, ln1_g, ln1_b,
     w_router, b_router, w_gate, b_gate, w_up, b_up, w_down, b_down, ln2_g, ln2_b) = w
    alpha = (2.0 * depth) ** 0.25
    lam_init = 0.8 - 0.6 * math.exp(-0.3 * l)
    bp, seq, d = x_p.shape
    ns = x_s.shape[0]
    t = bp * seq
    ne = w_router.shape[1]
    past = page_table.shape[1] * PAGE_SIZE

    w_a = w_in[:, :W_A]
    w_gt = w_in[:, W_A:W_A + N_GATE]
    w_r = w_in[:, W_A + N_GATE:]
    w_gc = jnp.pad(w_gt, ((0, 0), (0, LANES - N_GATE)))
    lam_p = jnp.stack([lq1, lk1, lq2, lk2])
    sub_g = subln_g.reshape(1, HEAD_W)
    mh_g2 = mh_g.reshape(1, VM_W)
    bcol = jnp.pad(jnp.concatenate([b_ig, b_fg]), (0, LANES - N_GATE)).reshape(1, LANES)
    brow = jnp.concatenate([b_ig, b_fg]).reshape(N_GATE, 1)
    g1, b1 = ln1_g.reshape(1, d), ln1_b.reshape(1, d)
    g2, b2 = ln2_g.reshape(1, d), ln2_b.reshape(1, d)
    wrt = w_router.T
    brt = b_router.reshape(ne, 1)

    x2 = x_p.reshape(t, d)
    tm = _pick_tile(seq, 256)
    tabs_p = _rope_tables(jnp.arange(seq, dtype=jnp.int32))
    (qa, k_f, k_b, v_f, v_b, qm, km, vm, gcol, grow, og, ga, gm) = _proj_prompt(
        x2, w_a.astype(BF16), w_gc.astype(BF16), w_gt.T.astype(BF16), w_r.astype(BF16), tabs_p, seq, tm)
    a_in, (wg_b, wu_b, wd_b) = _attn_prompt(qa, k_b, v_b, lam_p, sub_g, bp, seq, lam_init,
                                            (w_gate, w_up, w_down))
    m_in, cn_p, m_p = _mlstm_prompt(qm, km, vm, gcol, grow, bcol, brow, og, mh_g2, bp, seq,
                                    _pick_tile(seq, MLSTM_CHUNK))
    sgroup = MOE_GROUP * MOE_NSUB
    t_pad = -(-(t + ns) // sgroup) * sgroup
    h_p, hb_p, comb_p, mask_p = _mix(
        a_in, m_in, ga, gm, x2, w_ba.astype(BF16), w_bm.astype(BF16), w_o.astype(BF16), g1, b1, wrt, brt,
        alpha, False, tm, hb_rows=t_pad)

    xs2 = x_s.reshape(ns, d)
    chunk = 7 * LANES
    width = -(-w_in.shape[1] // chunk) * chunk
    z_s = _proj_sample(xs2, jnp.pad(w_in, ((0, 0), (0, width - w_in.shape[1]))), chunk)
    tabs_s = _rope_tables(jnp.full((1,), past, jnp.int32))
    ppb = _pick_tile(page_table.shape[1], 32)
    n_pool = cache_k.shape[1]
    k_s, v_s, a_s = _paged_attn(page_table + l * n_pool, z_s.reshape(ns, 1, width), tabs_s, lam_p, sub_g,
                                cache_k.reshape(-1, PAGE_SIZE * H_A, HEAD_W),
                                cache_v.reshape(-1, PAGE_SIZE * H_A, HEAD_W), ppb, lam_init)
    o = 3 * QA_W
    qc = z_s[:, o:o + QM_W].reshape(ns, H_M, DK_M, 1)
    kc = z_s[:, o + QM_W:o + 2 * QM_W].reshape(ns, H_M, DK_M, 1)
    v3 = z_s[:, o + 2 * QM_W:W_A].reshape(ns, 1, VM_W)
    gates = z_s[:, W_A:W_A + N_GATE]
    r0 = W_A + N_GATE
    og3 = z_s[:, r0:r0 + VM_W].reshape(ns, 1, VM_W)
    ga_s = z_s[:, r0 + VM_W:r0 + VM_W + d]
    gm_s = z_s[:, r0 + VM_W + d:r0 + VM_W + 2 * d]
    m_in_s, c_s, n_s, m_s = _mlstm_step(gates, m0, b_ig, b_fg, qc, kc, v3, og3, c0,
                                        n0.reshape(ns, H_M, DK_M, 1), mh_g2)
    h_s, hb_s, comb_s, mask_s = _mix(
        a_s.reshape(ns, QA_W), m_in_s.reshape(ns, VM_W), ga_s, gm_s, xs2, w_ba, w_bm, w_o, g1, b1,
        wrt, brt, alpha, True, ns)

    padt = lambda p, s: jnp.pad(jnp.concatenate([p, s], axis=1), ((0, 0), (0, t_pad - t - ns)))
    hb_all = _append_rows(hb_p, hb_s, t)
    comb_all = padt(comb_p, comb_s)
    pos_all, cnt = _route(padt(mask_p, mask_s), MOE_GROUP)
    counts = cnt[:, :, 0].astype(jnp.int32)
    used = (jnp.sum(counts, axis=1) > 0).reshape(-1, MOE_NSUB)
    nact = jnp.max(jnp.where(used, jnp.arange(1, MOE_NSUB + 1, dtype=jnp.int32), 0), axis=1)
    ff = _moe(counts.reshape(-1), nact, hb_all, pos_all, comb_all,
              wg_b, b_gate.reshape(ne, 1, -1), wu_b, b_up.reshape(ne, 1, -1),
              wd_b, b_down.reshape(ne, 1, -1), MOE_GROUP, MOE_NSUB, MOE_ROWS)
    y_p = _ln2(h_p, ff, g2, b2, alpha, tm, 0)
    y_s = _ln2(h_s, ff, g2, b2, alpha, ns, t // ns)

    cw = cn_p[:, :, :DV_M].reshape(bp, H_M, DK_M, DV_M)
    nw = cn_p[:, :, DV_M].reshape(bp, H_M, DK_M)
    outs_p = (y_p.reshape(bp, seq, d), k_f.reshape(bp, seq, H_A, HEAD_W), v_f.reshape(bp, seq, H_A, HEAD_W),
              cw, nw, m_p[:, 0, :H_M])
    outs_s = (y_s.reshape(ns, 1, d), k_s.reshape(ns, 1, H_A, HEAD_W), v_s.reshape(ns, 1, H_A, HEAD_W),
              c_s, n_s.reshape(ns, H_M, DK_M), m_s[:, :, 0])
    return outs_p, outs_s


def kernel(x_prompt, x_sample, cache_k, cache_v, state_c, state_n, state_m, page_table, w_in, b_igate, b_fgate, lambda_q1, lambda_k1, lambda_q2, lambda_k2, subln_g, mh_norm_g, w_ba, w_bm, w_o, ln1_g, ln1_b, w_router, b_router, w_gate, b_gate, w_up, b_up, w_down, b_down, ln2_g, ln2_b):
    depth = w_in.shape[0]
    assert x_sample.shape[1] == 1, "the sample pass handles one new token per sequence"
    weights = (w_in, b_igate, b_fgate, lambda_q1, lambda_k1, lambda_q2, lambda_k2, subln_g, mh_norm_g,
               w_ba, w_bm, w_o, ln1_g, ln1_b, w_router, b_router, w_gate, b_gate, w_up, b_up,
               w_down, b_down, ln2_g, ln2_b)
    y_p, y_s = x_prompt, x_sample
    acc_p = [[] for _ in range(5)]
    acc_s = [[] for _ in range(5)]
    for l in range(depth):
        outs_p, outs_s = _layer(l, depth, y_p, y_s, cache_k, cache_v, state_c[l], state_n[l],
                                state_m[l], page_table, tuple(a[l] for a in weights))
        y_p, y_s = outs_p[0], outs_s[0]
        for i in range(5):
            acc_p[i].append(outs_p[1 + i])
            acc_s[i].append(outs_s[1 + i])
    return (y_p, y_s, *(jnp.stack(a) for a in acc_p), *(jnp.stack(a) for a in acc_s))
```

```python
import functools
import math

import jax
import jax.numpy as jnp
from jax import lax
from jax.experimental import pallas as pl
from jax.experimental.pallas import tpu as pltpu

F32 = jnp.float32
BF16 = jnp.bfloat16
HIGHEST = lax.Precision.HIGHEST

H_A = 4
D_HA = 64
ROT_DIM = D_HA // 4
ROPE_THETA = 500000.0
H_M = 4
DK_M = 64
DV_M = 128
GATE_SOFTCAP = 15.0
N_EXPERTS = 32
TOP_K = 4
SWIGLU_LIMIT = 7.0
SWIGLU_ALPHA = 1.702
LN_EPS = 1e-5
PAGE_SIZE = 128

QA_W = H_A * 2 * D_HA
QM_W = H_M * DK_M
VM_W = H_M * DV_M
HEAD_W = 2 * D_HA
N_GATE = 2 * H_M
W_A = 3 * QA_W + 2 * QM_W + VM_W

LANES = 128
VMEM_LIMIT = 56 * 1024 * 1024
SIDE_SLAB_BYTES = 2 * 1024 * 1024

ATTN_TQ = 256
ATTN_TK = 512
ATTN_HEADS_PER_STEP = 4
MIX_TM = 512
MLSTM_SEQS_PER_STEP = 2
MLSTM_CHUNK = 256
MOE_ROWS = 128
MOE_GROUP = 7 * LANES
MOE_NSUB = 2
MOE_SCATTER_EXPERTS = 8

NT_DIMS = (((1,), (1,)), ((), ()))
TN_DIMS = (((0,), (0,)), ((), ()))


def _params(*sem):
    return pltpu.CompilerParams(dimension_semantics=sem, vmem_limit_bytes=VMEM_LIMIT)


def _softcap(x):
    return GATE_SOFTCAP * jnp.tanh(x / GATE_SOFTCAP)


def _log_sigmoid(x):
    return jnp.minimum(x, 0.0) - jnp.log1p(jnp.exp(-jnp.abs(x)))


def _sigmoid(x):
    return 1.0 / (1.0 + jnp.exp(-x))


def _split_bf16(x):
    hi = x.astype(BF16)
    lo = (x - hi.astype(F32)).astype(BF16)
    return hi, lo


def _stack_split(x):
    hi, lo = _split_bf16(x)
    return jnp.concatenate([hi, lo], axis=0)


def _fold_split(t):
    n = t.shape[0] // 2
    return t[:n] + t[n:]


def _layer_norm(x, g, b):
    mu = jnp.mean(x, axis=-1, keepdims=True)
    xc = x - mu
    var = jnp.mean(xc * xc, axis=-1, keepdims=True)
    return xc * lax.rsqrt(var + LN_EPS) * g + b


def _rope(t, ra, rb, rc):
    return t * ra + pltpu.roll(t, 8, 1) * rb + pltpu.roll(t, HEAD_W - 8, 1) * rc


def _rope_tables(pos):
    inv = ROPE_THETA ** (-jnp.arange(0, ROT_DIM, 2, dtype=F32) / ROT_DIM)
    ang = pos.astype(F32)[:, None] * inv[None, :]
    cos, sin = jnp.cos(ang), jnp.sin(ang)
    n = pos.shape[0]
    half = ROT_DIM // 2
    rest = D_HA - ROT_DIM
    a = jnp.concatenate([cos, cos, jnp.ones((n, rest), F32)], axis=1)
    b = jnp.concatenate([jnp.zeros((n, half), F32), sin, jnp.zeros((n, rest), F32)], axis=1)
    c = jnp.concatenate([-sin, jnp.zeros((n, half + rest), F32)], axis=1)
    return tuple(jnp.tile(t, (1, 2)) for t in (a, b, c))


def _diff_lambda(lam_ref, lam_init):
    lp = lam_ref[...]
    s1 = jnp.sum(lp[0:1] * lp[1:2], axis=1, keepdims=True)
    s2 = jnp.sum(lp[2:3] * lp[3:4], axis=1, keepdims=True)
    return jnp.exp(s1) - jnp.exp(s2) + lam_init


def _proj_prompt_kernel(x_ref, wa_ref, wgc_ref, wgr_ref, wr_ref, ra_ref, rb_ref, rc_ref,
                        q_ref, kf_ref, kb_ref, vf_ref, vb_ref, qm_ref, km_ref, vm_ref,
                        gcol_ref, grow_ref, og_ref, ga_ref, gm_ref):
    x = x_ref[...].astype(BF16)
    tm = x.shape[0]
    za = jnp.dot(x, wa_ref[...], preferred_element_type=F32)
    ra, rb, rc = ra_ref[...], rb_ref[...], rc_ref[...]
    for h in range(H_A):
        sl = slice(h * HEAD_W, (h + 1) * HEAD_W)
        qh = _rope(za[:, sl], ra, rb, rc)
        q_ref[:, sl] = (qh * (D_HA ** -0.5)).astype(BF16)
        kh = _rope(za[:, QA_W + h * HEAD_W:QA_W + (h + 1) * HEAD_W], ra, rb, rc)
        kf_ref[pl.ds(h, tm, stride=H_A), :] = kh
        kb_ref[:, sl] = kh.astype(BF16)
        vf_ref[pl.ds(h, tm, stride=H_A), :] = za[:, 2 * QA_W + h * HEAD_W:2 * QA_W + (h + 1) * HEAD_W]
    vb_ref[...] = za[:, 2 * QA_W:3 * QA_W].astype(BF16)
    o = 3 * QA_W
    qm_ref[...] = za[:, o:o + QM_W].astype(BF16)
    km_ref[...] = (za[:, o + QM_W:o + 2 * QM_W] * (DK_M ** -0.5)).astype(BF16)
    vm_ref[...] = za[:, o + 2 * QM_W:].astype(BF16)
    gcol_ref[...] = jnp.dot(x, wgc_ref[...], preferred_element_type=F32)
    grow_ref[...] = lax.dot_general(wgr_ref[...], x, NT_DIMS, preferred_element_type=F32)
    zr = jnp.dot(x, wr_ref[...], preferred_element_type=F32)
    d = ga_ref.shape[1]
    og_ref[...] = zr[:, :VM_W].astype(BF16)
    ga_ref[...] = zr[:, VM_W:VM_W + d].astype(BF16)
    gm_ref[...] = zr[:, VM_W + d:].astype(BF16)


def _proj_prompt(x2, wa, wgc, wgr, wr, tabs, seq, tm):
    t, d = x2.shape
    nrep = seq // tm
    row = lambda w: pl.BlockSpec((tm, w), lambda i: (i, 0))
    full = lambda a: pl.BlockSpec(a.shape, lambda i: (0,) * a.ndim)
    tab = pl.BlockSpec((tm, HEAD_W), lambda i: (i % nrep, 0))
    sds = lambda w, dt: jax.ShapeDtypeStruct((t, w), dt)
    kv_spec = pl.BlockSpec((tm * H_A, HEAD_W), lambda i: (i, 0))
    kv_sds = jax.ShapeDtypeStruct((t * H_A, HEAD_W), F32)
    return pl.pallas_call(
        _proj_prompt_kernel,
        grid=(t // tm,),
        in_specs=[row(d), full(wa), full(wgc), full(wgr), full(wr), tab, tab, tab],
        out_specs=[row(QA_W), kv_spec, row(QA_W), kv_spec, row(QA_W),
                   row(QM_W), row(QM_W), row(VM_W), row(LANES),
                   pl.BlockSpec((N_GATE, tm), lambda i: (0, i)),
                   row(VM_W), row(d), row(d)],
        out_shape=[sds(QA_W, BF16), kv_sds, sds(QA_W, BF16), kv_sds, sds(QA_W, BF16),
                   sds(QM_W, BF16), sds(QM_W, BF16), sds(VM_W, BF16), sds(LANES, F32),
                   jax.ShapeDtypeStruct((N_GATE, t), F32),
                   sds(VM_W, BF16), sds(d, BF16), sds(d, BF16)],
        compiler_params=_params("parallel"),
        name="proj_prompt",
    )(x2, wa, wgc, wgr, wr, *tabs)


def _attn_prompt_kernel(q_ref, k_ref, v_ref, lam_ref, g_ref, *rest, tq, tk, hps, lam_init):
    nside = (len(rest) - 1) // 2
    o_ref = rest[nside]
    for src, dst in zip(rest[:nside], rest[nside + 1:]):
        dst[...] = src[...].astype(dst.dtype)
    qi = pl.program_id(2)
    nfull = (qi * tq) // tk
    lane = lax.broadcasted_iota(jnp.int32, (1, HEAD_W), 1)
    r = lax.broadcasted_iota(jnp.int32, (tq, tk), 0) + qi * tq
    c = lax.broadcasted_iota(jnp.int32, (tq, tk), 1) + nfull * tk
    causal = jnp.concatenate([c <= r, c <= r], axis=0)
    ones_blk = jnp.broadcast_to(jnp.where(lane == 0, 1.0, 0.0).astype(BF16), (tk, HEAD_W))
    qqs = []
    for hh in range(hps):
        q = q_ref[:, hh * HEAD_W:(hh + 1) * HEAD_W]
        zero = jnp.zeros_like(q)
        qqs.append(jnp.concatenate([jnp.where(lane < D_HA, q, zero), jnp.where(lane >= D_HA, q, zero)],
                                   axis=0))

    def step(j, carry, masked):
        off = pl.multiple_of(j * tk, tk)
        out = []
        for hh in range(hps):
            m, acc = carry[hh]
            sl = slice(hh * HEAD_W, (hh + 1) * HEAD_W)
            k = k_ref[pl.ds(off, tk), sl]
            vext = jnp.concatenate([v_ref[pl.ds(off, tk), sl], ones_blk], axis=1)
            s = lax.dot_general(qqs[hh], k, NT_DIMS, preferred_element_type=F32)
            if masked:
                s = jnp.where(causal, s, -jnp.inf)
            m_new = jnp.maximum(m, jnp.max(s, axis=1, keepdims=True))
            p = jnp.exp(s - m_new)
            acc = jnp.exp(m - m_new) * acc + jnp.dot(p.astype(BF16), vext, preferred_element_type=F32)
            out.append((m_new, acc))
        return tuple(out)

    init = tuple((jnp.full((2 * tq, 1), -jnp.inf, F32), jnp.zeros((2 * tq, 2 * HEAD_W), F32))
                 for _ in range(hps))
    carry = lax.fori_loop(0, nfull, lambda j, cr: step(j, cr, False), init)
    carry = step(nfull, carry, True)
    lam = _diff_lambda(lam_ref, lam_init)
    for hh in range(hps):
        acc = carry[hh][1]
        o1 = acc[:tq, :HEAD_W] / acc[:tq, HEAD_W:HEAD_W + 1]
        o2 = acc[tq:, :HEAD_W] / acc[tq:, HEAD_W:HEAD_W + 1]
        o = o1 - lam * o2
        y = o * lax.rsqrt(jnp.mean(o * o, axis=1, keepdims=True) + LN_EPS) * g_ref[...] * (1.0 - lam_init)
        o_ref[:, hh * HEAD_W:(hh + 1) * HEAD_W] = y.astype(o_ref.dtype)


def _attn_prompt(q, k, v, lam_p, subln_g, batch, seq, lam_init, side):
    t = q.shape[0]
    tk = _pick_tile(seq, ATTN_TK)
    tq = _pick_tile(tk, ATTN_TQ)
    nq = seq // tq
    hps = ATTN_HEADS_PER_STEP
    nh = H_A // hps
    w = hps * HEAD_W
    steps = batch * nh * nq
    side2 = [a.reshape(-1, a.shape[-1]) for a in side]
    slab = [a.shape[0] // steps for a in side2]
    if any(a.shape[0] % steps or s % 16 or s * a.shape[1] * 4 > SIDE_SLAB_BYTES for a, s in zip(side2, slab)):
        side2, slab = [], []
    side_specs = [pl.BlockSpec((s, a.shape[1]), lambda b, h, i: ((b * nh + h) * nq + i, 0))
                  for a, s in zip(side2, slab)]
    outs = pl.pallas_call(
        functools.partial(_attn_prompt_kernel, tq=tq, tk=tk, hps=hps, lam_init=lam_init),
        grid=(batch, nh, nq),
        in_specs=[pl.BlockSpec((tq, w), lambda b, h, i: (b * nq + i, h)),
                  pl.BlockSpec((seq, w), lambda b, h, i: (b, h)),
                  pl.BlockSpec((seq, w), lambda b, h, i: (b, h)),
                  pl.BlockSpec(lam_p.shape, lambda b, h, i: (0, 0)),
                  pl.BlockSpec((1, HEAD_W), lambda b, h, i: (0, 0))] + side_specs,
        out_specs=[pl.BlockSpec((tq, w), lambda b, h, i: (b * nq + i, h))] + side_specs,
        out_shape=[jax.ShapeDtypeStruct((t, QA_W), BF16)]
        + [jax.ShapeDtypeStruct(a.shape, BF16) for a in side2],
        compiler_params=_params("parallel", "parallel", "parallel"),
        name="attn_prompt",
    )(q, k, v, lam_p, subln_g, *side2)
    if side2:
        return outs[0], [o.reshape(a.shape) for o, a in zip(outs[1:], side)]
    return outs[0], [a.astype(BF16) for a in side]


def _mlstm_prompt_kernel(q_ref, k_ref, v_ref, gcol_ref, grow_ref, bcol_ref, brow_ref, og_ref, g_ref,
                         min_ref, cn_ref, m_ref, *, lc, bps):
    ci = pl.program_id(1)

    @pl.when(ci == 0)
    def _():
        cn_ref[...] = jnp.zeros_like(cn_ref)
        m_ref[...] = jnp.zeros_like(m_ref)

    for i in range(bps):
        _mlstm_chunk(q_ref.at[0, i], k_ref.at[0, i], v_ref.at[0, i], gcol_ref.at[0, i], grow_ref[:, 0, i, :],
                     bcol_ref, brow_ref, og_ref.at[0, i], g_ref, min_ref.at[0, i], cn_ref.at[0, i],
                     m_ref.at[0, i], lc)


def _mlstm_chunk(q_ref, k_ref, v_ref, gcol_ref, grow, bcol_ref, brow_ref, og_ref, g_ref,
                 min_ref, cn_ref, m_ref, lc):
    gc = _softcap(gcol_ref[...] + bcol_ref[...])
    lfc = _log_sigmoid(gc)
    gr = _softcap(grow + brow_ref[...])
    lfr = _log_sigmoid(gr)
    ri = lax.broadcasted_iota(jnp.int32, (lc, lc), 0)
    cj = lax.broadcasted_iota(jnp.int32, (lc, lc), 1)
    tril = cj <= ri
    b_col = jnp.dot(tril.astype(F32), lfc, precision=HIGHEST, preferred_element_type=F32)
    b_row = jnp.dot(lfr, (ri <= cj).astype(F32), precision=HIGHEST, preferred_element_type=F32)

    cn = cn_ref[...]
    cn_bf = cn.astype(BF16)
    q = q_ref[...]
    k = k_ref[...]
    v = v_ref[...]
    m_all = m_ref[...]
    lane_q = lax.broadcasted_iota(jnp.int32, (1, QM_W), 1) // DK_M
    row_q = lax.broadcasted_iota(jnp.int32, (QM_W, 1), 0) // DK_M
    lane_v = lax.broadcasted_iota(jnp.int32, (1, LANES), 1)
    ones_blk = jnp.broadcast_to(jnp.where(lane_v == 0, 1.0, 0.0).astype(BF16), (lc, LANES))
    new_cn = jnp.zeros_like(cn)
    new_m = m_all
    for h in range(H_M):
        sl = slice(h * DV_M, (h + 1) * DV_M)
        qh = jnp.where(lane_q == h, q, jnp.zeros_like(q))
        kh = jnp.where(lane_q == h, k, jnp.zeros_like(k))
        b_c = b_col[:, H_M + h:H_M + h + 1]
        ig_c = gc[:, h:h + 1]
        b_r = b_row[H_M + h:H_M + h + 1, :]
        ig_r = gr[h:h + 1, :]
        m_prev = m_all[:, h:h + 1]
        dmat = jnp.where(tril, b_c - b_r + ig_r, -jnp.inf)
        m_inter = b_c + m_prev
        m_t = jnp.maximum(m_inter, jnp.max(dmat, axis=1, keepdims=True))
        w_inter = jnp.exp(m_inter - m_t)
        s = lax.dot_general(qh, kh, NT_DIMS, preferred_element_type=F32) * jnp.exp(dmat - m_t)
        vext = jnp.concatenate([v[:, sl], ones_blk], axis=1)
        nd = (w_inter * jnp.dot(qh, cn_bf, preferred_element_type=F32)
              + jnp.dot(s.astype(BF16), vext, preferred_element_type=F32))
        num = nd[:, :DV_M]
        den = nd[:, DV_M:DV_M + 1]
        hout = num / jnp.maximum(jnp.abs(den), jnp.exp(-m_t))
        mu = jnp.mean(hout, axis=1, keepdims=True)
        hc = hout - mu
        var = jnp.mean(hc * hc, axis=1, keepdims=True)
        y = hc * lax.rsqrt(var + LN_EPS) * g_ref[:, sl] * _sigmoid(og_ref[:, sl].astype(F32))
        min_ref[:, sl] = y.astype(min_ref.dtype)
        m_new = m_t[lc - 1:lc, :]
        b_last = b_c[lc - 1:lc, :]
        decay = jnp.exp(b_last + m_prev - m_new)
        wk = jnp.exp(b_last - b_c + ig_c - m_new)
        upd = lax.dot_general(kh, (wk * vext.astype(F32)).astype(BF16), TN_DIMS,
                              preferred_element_type=F32)
        new_cn = new_cn + jnp.where(row_q == h, decay * cn, 0.0) + upd
        new_m = jnp.where(lane_v == h, m_new, new_m)
    cn_ref[...] = new_cn
    m_ref[...] = new_m


def _mlstm_prompt(qm, km, vm, gcol, grow, bcol, brow, og, mh_g, batch, seq, lc):
    t = qm.shape[0]
    bps = MLSTM_SEQS_PER_STEP if batch % MLSTM_SEQS_PER_STEP == 0 else 1
    nb = batch // bps
    row = lambda w: pl.BlockSpec((1, bps, lc, w), lambda b, c: (b, 0, c, 0))
    const = lambda a: pl.BlockSpec(a.shape, lambda b, c: (0,) * a.ndim)
    seqs = lambda a: a.reshape(nb, bps, seq, a.shape[1])
    cw = 2 * LANES
    m_in, cn, m = pl.pallas_call(
        functools.partial(_mlstm_prompt_kernel, lc=lc, bps=bps),
        grid=(nb, seq // lc),
        in_specs=[row(QM_W), row(QM_W), row(VM_W), row(LANES),
                  pl.BlockSpec((N_GATE, 1, bps, lc), lambda b, c: (0, b, 0, c)),
                  const(bcol), const(brow), row(VM_W), const(mh_g)],
        out_specs=[row(VM_W),
                   pl.BlockSpec((1, bps, QM_W, cw), lambda b, c: (b, 0, 0, 0)),
                   pl.BlockSpec((1, bps, 1, LANES), lambda b, c: (b, 0, 0, 0))],
        out_shape=[jax.ShapeDtypeStruct((nb, bps, seq, VM_W), BF16),
                   jax.ShapeDtypeStruct((nb, bps, QM_W, cw), F32),
                   jax.ShapeDtypeStruct((nb, bps, 1, LANES), F32)],
        compiler_params=_params("parallel", "arbitrary"),
        name="mlstm_prompt",
    )(seqs(qm), seqs(km), seqs(vm), seqs(gcol), grow.reshape(N_GATE, nb, bps, seq), bcol, brow,
      seqs(og), mh_g)
    return m_in.reshape(t, VM_W), cn.reshape(batch, QM_W, cw), m.reshape(batch, 1, LANES)


def _mix_kernel(a_ref, mi_ref, ga_ref, gm_ref, x_ref, wba_ref, wbm_ref, wo_ref, g1_ref, b1_ref,
                wrt_ref, brt_ref, h_ref, hb_ref, comb_ref, mask_ref, *, alpha, precise):
    def mm(a, w_ref):
        if precise:
            return jnp.dot(a.astype(F32), w_ref[...], precision=HIGHEST, preferred_element_type=F32)
        return jnp.dot(a.astype(BF16), w_ref[...], preferred_element_type=F32)

    a_br = mm(a_ref[...], wba_ref)
    m_br = mm(mi_ref[...], wbm_ref)
    merged = _sigmoid(ga_ref[...].astype(F32)) * a_br + _sigmoid(gm_ref[...].astype(F32)) * m_br
    mix = mm(merged, wo_ref)
    h = _layer_norm(alpha * x_ref[...] + mix, g1_ref[...], b1_ref[...])
    h_ref[...] = h
    hb_ref[...] = h.astype(BF16)

    ne = wrt_ref.shape[0]
    if precise:
        logits = lax.dot_general(wrt_ref[...], h, NT_DIMS, precision=HIGHEST, preferred_element_type=F32)
    else:
        ws = _stack_split(wrt_ref[...])
        h_hi, h_lo = _split_bf16(h)
        logits = (_fold_split(lax.dot_general(ws, h_hi, NT_DIMS, preferred_element_type=F32))
                  + lax.dot_general(ws[:ne], h_lo, NT_DIMS, preferred_element_type=F32))
    logits = logits + brt_ref[...]
    tm = logits.shape[1]
    eidx = lax.broadcasted_iota(jnp.int32, (ne, tm), 0)
    work = logits
    sel = jnp.zeros((ne, tm), jnp.bool_)
    top = None
    for _ in range(TOP_K):
        mx = jnp.max(work, axis=0, keepdims=True)
        top = mx if top is None else top
        first = jnp.min(jnp.where(work == mx, eidx, ne), axis=0, keepdims=True)
        pick = eidx == first
        sel = jnp.logical_or(sel, pick)
        work = jnp.where(pick, -jnp.inf, work)
    ex = jnp.where(sel, jnp.exp(logits - top), 0.0)
    comb_ref[...] = ex / jnp.sum(ex, axis=0, keepdims=True)
    mask_ref[...] = jnp.where(sel, 1.0, 0.0)


def _mix(a_in, m_in, ga, gm, x2, wba, wbm, wo, g1, b1, wrt, brt, alpha, precise, tm, hb_rows=None):
    t, d = x2.shape
    hb_rows = t if hb_rows is None else hb_rows
    ne = wrt.shape[0]
    row = lambda w: pl.BlockSpec((tm, w), lambda i: (i, 0))
    col = pl.BlockSpec((ne, tm), lambda i: (0, i))
    const = lambda a: pl.BlockSpec(a.shape, lambda i: (0,) * a.ndim)
    return pl.pallas_call(
        functools.partial(_mix_kernel, alpha=alpha, precise=precise),
        grid=(t // tm,),
        in_specs=[row(QA_W), row(VM_W), row(d), row(d), row(d), const(wba), const(wbm), const(wo),
                  const(g1), const(b1), const(wrt), const(brt)],
        out_specs=[row(d), row(d), col, col],
        out_shape=[jax.ShapeDtypeStruct((t, d), F32), jax.ShapeDtypeStruct((hb_rows, d), BF16),
                   jax.ShapeDtypeStruct((ne, t), F32), jax.ShapeDtypeStruct((ne, t), F32)],
        compiler_params=_params("parallel"),
        name="mix_sample" if precise else "mix_prompt",
    )(a_in, m_in, ga, gm, x2, wba, wbm, wo, g1, b1, wrt, brt)


def _append_rows_kernel(big_ref, tail_ref, o_ref):
    del big_ref
    o_ref[...] = jnp.zeros_like(o_ref)

    @pl.when(pl.program_id(0) == 0)
    def _():
        o_ref[:tail_ref.shape[0], :] = tail_ref[...]


def _append_rows(big, tail, start):
    rows, d = big.shape
    blk = math.gcd(start, rows - start)
    assert blk % 16 == 0 and blk >= tail.shape[0]
    return pl.pallas_call(
        _append_rows_kernel,
        grid=((rows - start) // blk,),
        in_specs=[pl.BlockSpec(memory_space=pl.ANY), pl.BlockSpec(tail.shape, lambda i: (0, 0))],
        out_specs=pl.BlockSpec((blk, d), lambda i: (start // blk + i, 0)),
        out_shape=jax.ShapeDtypeStruct(big.shape, big.dtype),
        input_output_aliases={0: 0},
        compiler_params=_params("arbitrary"),
        name="append_rows",
    )(big, tail)


def _route_kernel(mask_ref, pos_ref, cnt_ref):
    maskf = mask_ref[...]
    g = maskf.shape[1]
    r = lax.broadcasted_iota(jnp.int32, (g, g), 0)
    c = lax.broadcasted_iota(jnp.int32, (g, g), 1)
    before = jnp.dot(maskf.astype(BF16), (r < c).astype(BF16), preferred_element_type=F32)
    pos_ref[...] = jnp.where(maskf > 0.0, before, -1.0)
    cnt_ref[0] = jnp.broadcast_to(jnp.sum(maskf, axis=1, keepdims=True), cnt_ref.shape[1:])


def _route(mask, group):
    ne, t = mask.shape
    ng = t // group
    return pl.pallas_call(
        _route_kernel,
        grid=(ng,),
        in_specs=[pl.BlockSpec((ne, group), lambda g: (0, g))],
        out_specs=[pl.BlockSpec((ne, group), lambda g: (0, g)),
                   pl.BlockSpec((1, ne, LANES), lambda g: (g, 0, 0))],
        out_shape=[jax.ShapeDtypeStruct((ne, t), F32), jax.ShapeDtypeStruct((ng, ne, LANES), F32)],
        compiler_params=_params("parallel"),
        name="route",
    )(mask)


def _moe_kernel(cnt_ref, nact_ref, h_ref, pos_ref, comb_ref, wg_ref, bg_ref, wu_ref, bu_ref,
                wd_ref, bd_ref, o_ref, y_scr, *, rows, group, nsub):
    sg = pl.program_id(0)
    e = pl.program_id(1)
    ne = pl.num_programs(1)
    n_exp = pos_ref.shape[0]

    @pl.when(e == 0)
    def _():
        o_ref[...] = jnp.zeros_like(o_ref)

    slot0 = lax.broadcasted_iota(jnp.int32, (rows, 1), 0).astype(F32)

    def gather(hit, tok):
        return jnp.dot(jnp.where(hit, 1.0, 0.0).astype(BF16), h_ref[tok, :],
                       preferred_element_type=F32).astype(BF16)

    def ffn(hit, tok):
        return ffn_rows(gather(hit, tok))

    def ffn_rows(xg):
        gate = jnp.minimum(jnp.dot(xg, wg_ref[0], preferred_element_type=F32) + bg_ref[0], SWIGLU_LIMIT)
        up = jnp.clip(jnp.dot(xg, wu_ref[0], preferred_element_type=F32) + bu_ref[0],
                      -SWIGLU_LIMIT, SWIGLU_LIMIT)
        hid = gate * _sigmoid(SWIGLU_ALPHA * gate) * (up + 1.0)
        return (jnp.dot(hid.astype(BF16), wd_ref[0], preferred_element_type=F32) + bd_ref[0]).astype(BF16)

    yrows = pl.ds(pl.multiple_of(e * rows, rows), rows)

    def first_tile(sub):
        tok = slice(sub * group, (sub + 1) * group)
        y_scr[sub, yrows, :] = ffn(pos_ref[pl.ds(e, 1), tok] == slot0, tok)

    nact = nact_ref[sg]

    @pl.when(nact == nsub)
    def _():
        xg = jnp.concatenate(
            [gather(pos_ref[pl.ds(e, 1), sub * group:(sub + 1) * group] == slot0,
                    slice(sub * group, (sub + 1) * group)) for sub in range(nsub)], axis=0)
        y = ffn_rows(xg)
        for sub in range(nsub):
            y_scr[sub, yrows, :] = y[sub * rows:(sub + 1) * rows]

    @pl.when(nact < nsub)
    def _():
        for sub in range(nsub):
            pl.when(sub < nact)(functools.partial(first_tile, sub))

            @pl.when(sub >= nact)
            def _(sub=sub):
                y_scr[sub, yrows, :] = jnp.zeros((rows, y_scr.shape[2]), y_scr.dtype)

    for sub in range(nsub):
        tok = slice(sub * group, (sub + 1) * group)
        n = cnt_ref[(sg * nsub + sub) * ne + e]
        pos = pos_ref[pl.ds(e, 1), tok]
        cw = comb_ref[pl.ds(e, 1), tok]

        def tile(s, carry, tok=tok, pos=pos, cw=cw):
            hit = pos == slot0 + (s * rows).astype(F32)
            gw = jnp.where(hit, cw, 0.0).astype(BF16)
            o_ref[tok, :] += lax.dot_general(gw, ffn(hit, tok), TN_DIMS, preferred_element_type=F32)
            return carry

        lax.fori_loop(1, (n + rows - 1) // rows, tile, 0)

    @pl.when(e == ne - 1)
    def _():
        for sub in range(nsub):
            tok = slice(sub * group, (sub + 1) * group)
            for c0 in range(0, n_exp, MOE_SCATTER_EXPERTS):
                gw = jnp.concatenate(
                    [jnp.where(pos_ref[ee:ee + 1, tok] == slot0, comb_ref[ee:ee + 1, tok], 0.0).astype(BF16)
                     for ee in range(c0, c0 + MOE_SCATTER_EXPERTS)], axis=0)
                ys = y_scr[sub, c0 * rows:(c0 + MOE_SCATTER_EXPERTS) * rows, :]
                o_ref[tok, :] += lax.dot_general(gw, ys, TN_DIMS, preferred_element_type=F32)


def _moe(counts, nact, hb, pos, comb, wg, bg, wu, bu, wd, bd, group, nsub, rows):
    t, d = hb.shape
    ne = wg.shape[0]
    sgroup = group * nsub
    tok = pl.BlockSpec((ne, sgroup), lambda g, e, c, a: (0, g))
    wspec = lambda w: pl.BlockSpec((1,) + w.shape[1:], lambda g, e, c, a: (e, 0, 0))
    grid_spec = pltpu.PrefetchScalarGridSpec(
        num_scalar_prefetch=2,
        grid=(t // sgroup, ne),
        in_specs=[pl.BlockSpec((sgroup, d), lambda g, e, c, a: (g, 0)), tok, tok,
                  wspec(wg), wspec(bg), wspec(wu), wspec(bu), wspec(wd), wspec(bd)],
        out_specs=pl.BlockSpec((sgroup, d), lambda g, e, c, a: (g, 0)),
        scratch_shapes=[pltpu.VMEM((nsub, ne * rows, d), BF16)],
    )
    return pl.pallas_call(
        functools.partial(_moe_kernel, rows=rows, group=group, nsub=nsub),
        grid_spec=grid_spec,
        out_shape=jax.ShapeDtypeStruct((t, d), F32),
        compiler_params=_params("parallel", "arbitrary"),
        name="moe",
    )(counts, nact, hb, pos, comb, wg, bg, wu, bu, wd, bd)


def _ln2_kernel(h_ref, ff_ref, g_ref, b_ref, y_ref, *, alpha):
    y_ref[...] = _layer_norm(alpha * h_ref[...] + ff_ref[...], g_ref[...], b_ref[...])


def _ln2(h, ff, g2, b2, alpha, tm, ff_block0):
    t, d = h.shape
    const = lambda a: pl.BlockSpec(a.shape, lambda i: (0,) * a.ndim)
    return pl.pallas_call(
        functools.partial(_ln2_kernel, alpha=alpha),
        grid=(t // tm,),
        in_specs=[pl.BlockSpec((tm, d), lambda i: (i, 0)),
                  pl.BlockSpec((tm, d), lambda i: (ff_block0 + i, 0)), const(g2), const(b2)],
        out_specs=pl.BlockSpec((tm, d), lambda i: (i, 0)),
        out_shape=jax.ShapeDtypeStruct((t, d), F32),
        compiler_params=_params("parallel"),
        name="ln2",
    )(h, ff, g2, b2)


def _proj_sample_kernel(x_ref, w_ref, z_ref):
    z_ref[...] = jnp.dot(x_ref[...], w_ref[...], precision=HIGHEST, preferred_element_type=F32)


def _proj_sample(xs, w, chunk):
    n, d = xs.shape
    width = w.shape[1]
    return pl.pallas_call(
        _proj_sample_kernel,
        grid=(width // chunk,),
        in_specs=[pl.BlockSpec((n, d), lambda j: (0, 0)), pl.BlockSpec((d, chunk), lambda j: (0, j))],
        out_specs=pl.BlockSpec((n, chunk), lambda j: (0, j)),
        out_shape=jax.ShapeDtypeStruct((n, width), F32),
        compiler_params=_params("parallel"),
        name="proj_sample",
    )(xs, w)


def _paged_attn_kernel(pt_ref, z_ref, ra_ref, rb_ref, rc_ref, lam_ref, g_ref, *rest,
                       ppb, lam_init):
    k_refs = rest[:ppb]
    v_refs = rest[ppb:2 * ppb]
    kout_ref, vout_ref, a_ref, q_scr, qs_scr, m_scr, l_scr, acc_scr = rest[2 * ppb:]
    j = pl.program_id(1)
    nj = pl.num_programs(1)
    nrow = 2 * H_A
    prow = PAGE_SIZE * H_A
    ra, rb, rc = ra_ref[...], rb_ref[...], rc_ref[...]
    lane = lax.broadcasted_iota(jnp.int32, (1, HEAD_W), 1)

    def per_row(t):
        return jnp.concatenate([t[:, (r // 2) * HEAD_W:(r // 2 + 1) * HEAD_W] for r in range(nrow)], axis=0)

    @pl.when(j == 0)
    def _():
        z = z_ref[0]
        rows = []
        for h in range(H_A):
            qh = _rope(z[:, h * HEAD_W:(h + 1) * HEAD_W], ra, rb, rc) * (D_HA ** -0.5)
            rows += [jnp.where(lane < D_HA, qh, 0.0), jnp.where(lane >= D_HA, qh, 0.0)]
        q = jnp.concatenate(rows, axis=0)
        q_scr[...] = q
        qs_scr[...] = _stack_split(q)
        m_scr[...] = jnp.full_like(m_scr, -jnp.inf)
        l_scr[...] = jnp.zeros_like(l_scr)
        acc_scr[...] = jnp.zeros_like(acc_scr)

    qs = qs_scr[...]
    parts = []
    for kr in k_refs:
        k_hi, k_lo = _split_bf16(kr[0])
        t = (lax.dot_general(qs, k_hi, NT_DIMS, preferred_element_type=F32)
             + lax.dot_general(qs, k_lo, NT_DIMS, preferred_element_type=F32))
        parts.append(_fold_split(t))
    s = jnp.concatenate(parts, axis=1)
    r_head = lax.broadcasted_iota(jnp.int32, s.shape, 0) // 2
    c_head = lax.broadcasted_iota(jnp.int32, s.shape, 1) % H_A
    s = jnp.where(r_head == c_head, s, -jnp.inf)
    m_old = m_scr[...]
    m_new = jnp.maximum(m_old, jnp.max(s, axis=1, keepdims=True))
    alpha = jnp.exp(m_old - m_new)
    p = jnp.exp(s - m_new)
    l_scr[...] = alpha * l_scr[...] + jnp.sum(p, axis=1, keepdims=True)
    ps = _stack_split(p)
    pv = None
    for i, vr in enumerate(v_refs):
        v_hi, v_lo = _split_bf16(vr[0])
        pi = ps[:, i * prow:(i + 1) * prow]
        t = jnp.dot(pi, v_hi, preferred_element_type=F32) + jnp.dot(pi, v_lo, preferred_element_type=F32)
        pv = t if pv is None else pv + t
    acc_scr[...] = alpha * acc_scr[...] + _fold_split(pv)
    m_scr[...] = m_new

    @pl.when(j == nj - 1)
    def _():
        z = z_ref[0]
        k_new = jnp.concatenate([_rope(z[:, QA_W + h * HEAD_W:QA_W + (h + 1) * HEAD_W], ra, rb, rc)
                                 for h in range(H_A)], axis=1)
        v_new = z[:, 2 * QA_W:3 * QA_W]
        kout_ref[0] = k_new
        vout_ref[0] = v_new
        s_self = jnp.sum(q_scr[...] * per_row(k_new), axis=1, keepdims=True)
        m_old2 = m_scr[...]
        m_fin = jnp.maximum(m_old2, s_self)
        a2 = jnp.exp(m_old2 - m_fin)
        p_self = jnp.exp(s_self - m_fin)
        l_fin = a2 * l_scr[...] + p_self
        o_all = (a2 * acc_scr[...] + p_self * per_row(v_new)) / l_fin
        lam = _diff_lambda(lam_ref, lam_init)
        outs = []
        for h in range(H_A):
            o = o_all[2 * h:2 * h + 1] - lam * o_all[2 * h + 1:2 * h + 2]
            outs.append(o * lax.rsqrt(jnp.mean(o * o, axis=1, keepdims=True) + LN_EPS)
                        * g_ref[...] * (1.0 - lam_init))
        a_ref[0] = jnp.concatenate(outs, axis=1)


def _paged_attn(page_table, z3, tabs, lam_p, subln_g, ck, cv, ppb, lam_init):
    nb, npages = page_table.shape
    width = z3.shape[2]
    nj = npages // ppb
    pt = page_table.reshape(-1)
    prow = PAGE_SIZE * H_A
    nrow = 2 * H_A
    const = lambda a: pl.BlockSpec(a.shape, lambda b, j, p: (0,) * a.ndim)
    page = lambda i: pl.BlockSpec((1, prow, HEAD_W),
                                  lambda b, j, p: (p[b * npages + j * ppb + i], 0, 0))
    seq3 = lambda w: pl.BlockSpec((1, 1, w), lambda b, j, p: (b, 0, 0))
    grid_spec = pltpu.PrefetchScalarGridSpec(
        num_scalar_prefetch=1,
        grid=(nb, nj),
        in_specs=[seq3(width), const(tabs[0]), const(tabs[1]), const(tabs[2]), const(lam_p),
                  const(subln_g)] + [page(i) for i in range(ppb)] + [page(i) for i in range(ppb)],
        out_specs=[seq3(QA_W), seq3(QA_W), seq3(QA_W)],
        scratch_shapes=[pltpu.VMEM((nrow, HEAD_W), F32), pltpu.VMEM((2 * nrow, HEAD_W), BF16),
                        pltpu.VMEM((nrow, 1), F32), pltpu.VMEM((nrow, 1), F32),
                        pltpu.VMEM((nrow, HEAD_W), F32)],
    )
    sds = jax.ShapeDtypeStruct((nb, 1, QA_W), F32)
    return pl.pallas_call(
        functools.partial(_paged_attn_kernel, ppb=ppb, lam_init=lam_init),
        grid_spec=grid_spec,
        out_shape=[sds, sds, sds],
        compiler_params=_params("parallel", "arbitrary"),
        name="paged_attn",
    )(pt, z3, *tabs, lam_p, subln_g, *([ck] * ppb), *([cv] * ppb))


def _mlstm_step_kernel(gates_ref, m0_ref, big_ref, bfg_ref, qc_ref, kc_ref, v_ref, og_ref, c0_ref,
                       n0_ref, g_ref, min_ref, c_ref, n_ref, m_ref):
    spb = c0_ref.shape[0]
    for i in range(spb):
        _mlstm_step_one(pl.program_id(0) * spb + i, i, gates_ref, m0_ref, big_ref, bfg_ref, qc_ref, kc_ref,
                        v_ref, og_ref, c0_ref, n0_ref, g_ref, min_ref, c_ref, n_ref, m_ref)


def _mlstm_step_one(b, i, gates_ref, m0_ref, big_ref, bfg_ref, qc_ref, kc_ref, v_ref, og_ref, c0_ref,
                    n0_ref, g_ref, min_ref, c_ref, n_ref, m_ref):
    outs = []
    for h in range(H_M):
        sl = slice(h * DV_M, (h + 1) * DV_M)
        ig = _softcap(jnp.full((1, LANES), gates_ref[b, h] + big_ref[h], F32))
        lf = _log_sigmoid(_softcap(jnp.full((1, LANES), gates_ref[b, H_M + h] + bfg_ref[h], F32)))
        m0 = jnp.full((1, LANES), m0_ref[b, h], F32)
        m_inter = lf + m0
        m_t = jnp.maximum(m_inter, ig)
        w_inter = jnp.exp(m_inter - m_t)
        w_new = jnp.exp(ig - m_t)
        qc = qc_ref[i, h]
        kc = kc_ref[i, h] * (DK_M ** -0.5)
        vr = v_ref[i][:, sl]
        c0 = c0_ref[i, h]
        n0 = n0_ref[i, h]
        qk = jnp.sum(qc * kc, axis=0, keepdims=True)
        s = qk * w_new
        num = w_inter * jnp.sum(qc * c0, axis=0, keepdims=True) + s * vr
        den = w_inter * jnp.sum(qc * n0, axis=0, keepdims=True) + s
        hout = num / jnp.maximum(jnp.abs(den), jnp.exp(-m_t))
        mu = jnp.mean(hout, axis=1, keepdims=True)
        hc = hout - mu
        var = jnp.mean(hc * hc, axis=1, keepdims=True)
        outs.append(hc * lax.rsqrt(var + LN_EPS) * g_ref[:, sl] * _sigmoid(og_ref[i][:, sl]))
        c_ref[i, h] = w_inter * c0 + (w_new * kc) * vr
        n_ref[i, h] = w_inter[:, 0:1] * n0 + w_new[:, 0:1] * kc
        m_ref[i, h:h + 1, :] = m_t
    min_ref[i] = jnp.concatenate(outs, axis=1)


def _mlstm_step(gates, m0, big, bfg, qc, kc, v3, og3, c0, n0c, mh_g):
    nb = c0.shape[0]
    spb = _pick_tile(nb, 8)
    smem = pl.BlockSpec(memory_space=pltpu.SMEM)
    per = lambda a: pl.BlockSpec((spb,) + a.shape[1:], lambda b: (b,) + (0,) * (a.ndim - 1))
    const = lambda a: pl.BlockSpec(a.shape, lambda b: (0,) * a.ndim)
    return pl.pallas_call(
        _mlstm_step_kernel,
        grid=(nb // spb,),
        in_specs=[smem, smem, smem, smem, per(qc), per(kc), per(v3), per(og3), per(c0), per(n0c),
                  const(mh_g)],
        out_specs=[per(v3), per(c0), per(n0c), pl.BlockSpec((spb, H_M, LANES), lambda b: (b, 0, 0))],
        out_shape=[jax.ShapeDtypeStruct(v3.shape, F32), jax.ShapeDtypeStruct(c0.shape, F32),
                   jax.ShapeDtypeStruct(n0c.shape, F32),
                   jax.ShapeDtypeStruct((nb, H_M, LANES), F32)],
        compiler_params=_params("parallel"),
        name="mlstm_step",
    )(gates, m0, big, bfg, qc, kc, v3, og3, c0, n0c, mh_g)


def _pick_tile(n, target):
    t = min(n, target)
    while n % t:
        t //= 2
    return t


def _layer(l, depth, x_p, x_s, cache_k, cache_v, c0, n0, m0, page_table, w):
    (w_in, b_ig, b_fg, lq1, lk1, lq2, lk2, subln_g, mh_g, w_ba, w_bm, w_o, ln1_g, ln1_b,
     w_router, b_router, w_gate, b_gate, w_up, b_up, w_down, b_down, ln2_g, ln2_b) = w
    alpha = (2.0 * depth) ** 0.25
    lam_init = 0.8 - 0.6 * math.exp(-0.3 * l)
    bp, seq, d = x_p.shape
    ns = x_s.shape[0]
    t = bp * seq
    ne = w_router.shape[1]
    past = page_table.shape[1] * PAGE_SIZE

    w_a = w_in[:, :W_A]
    w_gt = w_in[:, W_A:W_A + N_GATE]
    w_r = w_in[:, W_A + N_GATE:]
    w_gc = jnp.pad(w_gt, ((0, 0), (0, LANES - N_GATE)))
    lam_p = jnp.stack([lq1, lk1, lq2, lk2])
    sub_g = subln_g.reshape(1, HEAD_W)
    mh_g2 = mh_g.reshape(1, VM_W)
    bcol = jnp.pad(jnp.concatenate([b_ig, b_fg]), (0, LANES - N_GATE)).reshape(1, LANES)
    brow = jnp.concatenate([b_ig, b_fg]).reshape(N_GATE, 1)
    g1, b1 = ln1_g.reshape(1, d), ln1_b.reshape(1, d)
    g2, b2 = ln2_g.reshape(1, d), ln2_b.reshape(1, d)
    wrt = w_router.T
    brt = b_router.reshape(ne, 1)

    x2 = x_p.reshape(t, d)
    tm = _pick_tile(seq, 256)
    tabs_p = _rope_tables(jnp.arange(seq, dtype=jnp.int32))
    (qa, k_f, k_b, v_f, v_b, qm, km, vm, gcol, grow, og, ga, gm) = _proj_prompt(
        x2, w_a.astype(BF16), w_gc.astype(BF16), w_gt.T.astype(BF16), w_r.astype(BF16), tabs_p, seq, tm)
    a_in, (wg_b, wu_b, wd_b) = _attn_prompt(qa, k_b, v_b, lam_p, sub_g, bp, seq, lam_init,
                                            (w_gate, w_up, w_down))
    m_in, cn_p, m_p = _mlstm_prompt(qm, km, vm, gcol, grow, bcol, brow, og, mh_g2, bp, seq,
                                    _pick_tile(seq, MLSTM_CHUNK))
    sgroup = MOE_GROUP * MOE_NSUB
    t_pad = -(-(t + ns) // sgroup) * sgroup
    h_p, hb_p, comb_p, mask_p = _mix(
        a_in, m_in, ga, gm, x2, w_ba.astype(BF16), w_bm.astype(BF16), w_o.astype(BF16), g1, b1, wrt, brt,
        alpha, False, _pick_tile(seq, MIX_TM), hb_rows=t_pad)

    xs2 = x_s.reshape(ns, d)
    chunk = 7 * LANES
    width = -(-w_in.shape[1] // chunk) * chunk
    z_s = _proj_sample(xs2, jnp.pad(w_in, ((0, 0), (0, width - w_in.shape[1]))), chunk)
    tabs_s = _rope_tables(jnp.full((1,), past, jnp.int32))
    ppb = _pick_tile(page_table.shape[1], 32)
    n_pool = cache_k.shape[1]
    k_s, v_s, a_s = _paged_attn(page_table + l * n_pool, z_s.reshape(ns, 1, width), tabs_s, lam_p, sub_g,
                                cache_k.reshape(-1, PAGE_SIZE * H_A, HEAD_W),
                                cache_v.reshape(-1, PAGE_SIZE * H_A, HEAD_W), ppb, lam_init)
    o = 3 * QA_W
    qc = z_s[:, o:o + QM_W].reshape(ns, H_M, DK_M, 1)
    kc = z_s[:, o + QM_W:o + 2 * QM_W].reshape(ns, H_M, DK_M, 1)
    v3 = z_s[:, o + 2 * QM_W:W_A].reshape(ns, 1, VM_W)
    gates = z_s[:, W_A:W_A + N_GATE]
    r0 = W_A + N_GATE
    og3 = z_s[:, r0:r0 + VM_W].reshape(ns, 1, VM_W)
    ga_s = z_s[:, r0 + VM_W:r0 + VM_W + d]
    gm_s = z_s[:, r0 + VM_W + d:r0 + VM_W + 2 * d]
    m_in_s, c_s, n_s, m_s = _mlstm_step(gates, m0, b_ig, b_fg, qc, kc, v3, og3, c0,
                                        n0.reshape(ns, H_M, DK_M, 1), mh_g2)
    h_s, hb_s, comb_s, mask_s = _mix(
        a_s.reshape(ns, QA_W), m_in_s.reshape(ns, VM_W), ga_s, gm_s, xs2, w_ba, w_bm, w_o, g1, b1,
        wrt, brt, alpha, True, ns)

    padt = lambda p, s: jnp.pad(jnp.concatenate([p, s], axis=1), ((0, 0), (0, t_pad - t - ns)))
    hb_all = _append_rows(hb_p, hb_s, t)
    comb_all = padt(comb_p, comb_s)
    pos_all, cnt = _route(padt(mask_p, mask_s), MOE_GROUP)
    counts = cnt[:, :, 0].astype(jnp.int32)
    used = (jnp.sum(counts, axis=1) > 0).reshape(-1, MOE_NSUB)
    nact = jnp.max(jnp.where(used, jnp.arange(1, MOE_NSUB + 1, dtype=jnp.int32), 0), axis=1)
    ff = _moe(counts.reshape(-1), nact, hb_all, pos_all, comb_all,
              wg_b, b_gate.reshape(ne, 1, -1), wu_b, b_up.reshape(ne, 1, -1),
              wd_b, b_down.reshape(ne, 1, -1), MOE_GROUP, MOE_NSUB, MOE_ROWS)
    y_p = _ln2(h_p, ff, g2, b2, alpha, tm, 0)
    y_s = _ln2(h_s, ff, g2, b2, alpha, ns, t // ns)

    cw = cn_p[:, :, :DV_M].reshape(bp, H_M, DK_M, DV_M)
    nw = cn_p[:, :, DV_M].reshape(bp, H_M, DK_M)
    outs_p = (y_p.reshape(bp, seq, d), k_f.reshape(bp, seq, H_A, HEAD_W), v_f.reshape(bp, seq, H_A, HEAD_W),
              cw, nw, m_p[:, 0, :H_M])
    outs_s = (y_s.reshape(ns, 1, d), k_s.reshape(ns, 1, H_A, HEAD_W), v_s.reshape(ns, 1, H_A, HEAD_W),
              c_s, n_s.reshape(ns, H_M, DK_M), m_s[:, :, 0])
    return outs_p, outs_s


def kernel(x_prompt, x_sample, cache_k, cache_v, state_c, state_n, state_m, page_table, w_in, b_igate, b_fgate, lambda_q1, lambda_k1, lambda_q2, lambda_k2, subln_g, mh_norm_g, w_ba, w_bm, w_o, ln1_g, ln1_b, w_router, b_router, w_gate, b_gate, w_up, b_up, w_down, b_down, ln2_g, ln2_b):
    depth = w_in.shape[0]
    assert x_sample.shape[1] == 1, "the sample pass handles one new token per sequence"
    weights = (w_in, b_igate, b_fgate, lambda_q1, lambda_k1, lambda_q2, lambda_k2, subln_g, mh_norm_g,
               w_ba, w_bm, w_o, ln1_g, ln1_b, w_router, b_router, w_gate, b_gate, w_up, b_up,
               w_down, b_down, ln2_g, ln2_b)
    y_p, y_s = x_prompt, x_sample
    acc_p = [[] for _ in range(5)]
    acc_s = [[] for _ in range(5)]
    for l in range(depth):
        outs_p, outs_s = _layer(l, depth, y_p, y_s, cache_k, cache_v, state_c[l], state_n[l],
                                state_m[l], page_table, tuple(a[l] for a in weights))
        y_p, y_s = outs_p[0], outs_s[0]
        for i in range(5):
            acc_p[i].append(outs_p[1 + i])
            acc_s[i].append(outs_s[1 + i])
    return (y_p, y_s, *(jnp.stack(a) for a in acc_p), *(jnp.stack(a) for a in acc_s))
```

```python
import functools
import math

import jax
import jax.numpy as jnp
from jax import lax
from jax.experimental import pallas as pl
from jax.experimental.pallas import tpu as pltpu

F32 = jnp.float32
BF16 = jnp.bfloat16
HIGHEST = lax.Precision.HIGHEST

H_A = 4
D_HA = 64
ROT_DIM = D_HA // 4
ROPE_THETA = 500000.0
H_M = 4
DK_M = 64
DV_M = 128
GATE_SOFTCAP = 15.0
N_EXPERTS = 32
TOP_K = 4
SWIGLU_LIMIT = 7.0
SWIGLU_ALPHA = 1.702
LN_EPS = 1e-5
PAGE_SIZE = 128

QA_W = H_A * 2 * D_HA
QM_W = H_M * DK_M
VM_W = H_M * DV_M
HEAD_W = 2 * D_HA
N_GATE = 2 * H_M
W_A = 3 * QA_W + 2 * QM_W + VM_W

LANES = 128
VMEM_LIMIT = 56 * 1024 * 1024
SIDE_SLAB_BYTES = 2 * 1024 * 1024

ATTN_TQ = 256
ATTN_TK = 512
ATTN_HEADS_PER_STEP = 4
PROJ_TM = 512
MIX_TM = 1024
MLSTM_SEQS_PER_STEP = 2
MLSTM_CHUNK = 256
MOE_ROWS = 128
MOE_GROUP = 7 * LANES
MOE_NSUB = 2
MOE_SLAB_EXPERTS = N_EXPERTS
MOE_SCATTER_EXPERTS = 8

NT_DIMS = (((1,), (1,)), ((), ()))
TN_DIMS = (((0,), (0,)), ((), ()))


def _params(*sem):
    return pltpu.CompilerParams(dimension_semantics=sem, vmem_limit_bytes=VMEM_LIMIT)


def _softcap(x):
    return GATE_SOFTCAP * jnp.tanh(x / GATE_SOFTCAP)


def _log_sigmoid(x):
    return jnp.minimum(x, 0.0) - jnp.log1p(jnp.exp(-jnp.abs(x)))


def _sigmoid(x):
    return 1.0 / (1.0 + jnp.exp(-x))


def _split_bf16(x):
    hi = x.astype(BF16)
    lo = (x - hi.astype(F32)).astype(BF16)
    return hi, lo


def _stack_split(x):
    hi, lo = _split_bf16(x)
    return jnp.concatenate([hi, lo], axis=0)


def _fold_split(t):
    n = t.shape[0] // 2
    return t[:n] + t[n:]


def _layer_norm(x, g, b):
    mu = jnp.mean(x, axis=-1, keepdims=True)
    xc = x - mu
    var = jnp.mean(xc * xc, axis=-1, keepdims=True)
    return xc * lax.rsqrt(var + LN_EPS) * g + b


def _rope(t, ra, rb, rc):
    return t * ra + pltpu.roll(t, 8, 1) * rb + pltpu.roll(t, HEAD_W - 8, 1) * rc


def _rope_tables(pos):
    inv = ROPE_THETA ** (-jnp.arange(0, ROT_DIM, 2, dtype=F32) / ROT_DIM)
    ang = pos.astype(F32)[:, None] * inv[None, :]
    cos, sin = jnp.cos(ang), jnp.sin(ang)
    n = pos.shape[0]
    half = ROT_DIM // 2
    rest = D_HA - ROT_DIM
    a = jnp.concatenate([cos, cos, jnp.ones((n, rest), F32)], axis=1)
    b = jnp.concatenate([jnp.zeros((n, half), F32), sin, jnp.zeros((n, rest), F32)], axis=1)
    c = jnp.concatenate([-sin, jnp.zeros((n, half + rest), F32)], axis=1)
    return tuple(jnp.tile(t, (1, 2)) for t in (a, b, c))


def _diff_lambda(lam_ref, lam_init):
    lp = lam_ref[...]
    s1 = jnp.sum(lp[0:1] * lp[1:2], axis=1, keepdims=True)
    s2 = jnp.sum(lp[2:3] * lp[3:4], axis=1, keepdims=True)
    return jnp.exp(s1) - jnp.exp(s2) + lam_init


def _proj_prompt_kernel(x_ref, wa_ref, wgc_ref, wgr_ref, wr_ref, ra_ref, rb_ref, rc_ref,
                        q_ref, kf_ref, kb_ref, vf_ref, vb_ref, qm_ref, km_ref, vm_ref,
                        gcol_ref, grow_ref, og_ref, ga_ref, gm_ref):
    x = x_ref[...].astype(BF16)
    tm = x.shape[0]
    za = jnp.dot(x, wa_ref[...], preferred_element_type=F32)
    ra, rb, rc = ra_ref[...], rb_ref[...], rc_ref[...]
    for h in range(H_A):
        sl = slice(h * HEAD_W, (h + 1) * HEAD_W)
        qh = _rope(za[:, sl], ra, rb, rc)
        q_ref[:, sl] = (qh * (D_HA ** -0.5)).astype(BF16)
        kh = _rope(za[:, QA_W + h * HEAD_W:QA_W + (h + 1) * HEAD_W], ra, rb, rc)
        kf_ref[pl.ds(h, tm, stride=H_A), :] = kh
        kb_ref[:, sl] = kh.astype(BF16)
        vf_ref[pl.ds(h, tm, stride=H_A), :] = za[:, 2 * QA_W + h * HEAD_W:2 * QA_W + (h + 1) * HEAD_W]
    vb_ref[...] = za[:, 2 * QA_W:3 * QA_W].astype(BF16)
    o = 3 * QA_W
    qm_ref[...] = za[:, o:o + QM_W].astype(BF16)
    km_ref[...] = (za[:, o + QM_W:o + 2 * QM_W] * (DK_M ** -0.5)).astype(BF16)
    vm_ref[...] = za[:, o + 2 * QM_W:].astype(BF16)
    gcol_ref[...] = jnp.dot(x, wgc_ref[...], preferred_element_type=F32)
    grow_ref[...] = lax.dot_general(wgr_ref[...], x, NT_DIMS, preferred_element_type=F32)
    zr = jnp.dot(x, wr_ref[...], preferred_element_type=F32)
    d = ga_ref.shape[1]
    og_ref[...] = zr[:, :VM_W].astype(BF16)
    ga_ref[...] = zr[:, VM_W:VM_W + d].astype(BF16)
    gm_ref[...] = zr[:, VM_W + d:].astype(BF16)


def _proj_prompt(x2, wa, wgc, wgr, wr, tabs, seq, tm):
    t, d = x2.shape
    nrep = seq // tm
    row = lambda w: pl.BlockSpec((tm, w), lambda i: (i, 0))
    full = lambda a: pl.BlockSpec(a.shape, lambda i: (0,) * a.ndim, pipeline_mode=pl.Buffered(1))
    tab = pl.BlockSpec((tm, HEAD_W), lambda i: (i % nrep, 0))
    sds = lambda w, dt: jax.ShapeDtypeStruct((t, w), dt)
    kv_spec = pl.BlockSpec((tm * H_A, HEAD_W), lambda i: (i, 0))
    kv_sds = jax.ShapeDtypeStruct((t * H_A, HEAD_W), F32)
    return pl.pallas_call(
        _proj_prompt_kernel,
        grid=(t // tm,),
        in_specs=[row(d), full(wa), full(wgc), full(wgr), full(wr), tab, tab, tab],
        out_specs=[row(QA_W), kv_spec, row(QA_W), kv_spec, row(QA_W),
                   row(QM_W), row(QM_W), row(VM_W), row(LANES),
                   pl.BlockSpec((N_GATE, tm), lambda i: (0, i)),
                   row(VM_W), row(d), row(d)],
        out_shape=[sds(QA_W, BF16), kv_sds, sds(QA_W, BF16), kv_sds, sds(QA_W, BF16),
                   sds(QM_W, BF16), sds(QM_W, BF16), sds(VM_W, BF16), sds(LANES, F32),
                   jax.ShapeDtypeStruct((N_GATE, t), F32),
                   sds(VM_W, BF16), sds(d, BF16), sds(d, BF16)],
        compiler_params=_params("parallel"),
        name="proj_prompt",
    )(x2, wa, wgc, wgr, wr, *tabs)


def _attn_prompt_kernel(q_ref, k_ref, v_ref, lam_ref, g_ref, *rest, tq, tk, hps, lam_init):
    nside = (len(rest) - 1) // 2
    o_ref = rest[nside]
    for src, dst in zip(rest[:nside], rest[nside + 1:]):
        dst[...] = src[...].astype(dst.dtype)
    qi = pl.program_id(2)
    nfull = (qi * tq) // tk
    lane = lax.broadcasted_iota(jnp.int32, (1, HEAD_W), 1)
    r = lax.broadcasted_iota(jnp.int32, (tq, tk), 0) + qi * tq
    c = lax.broadcasted_iota(jnp.int32, (tq, tk), 1) + nfull * tk
    causal = jnp.concatenate([c <= r, c <= r], axis=0)
    ones_blk = jnp.broadcast_to(jnp.where(lane == 0, 1.0, 0.0).astype(BF16), (tk, HEAD_W))
    qqs = []
    for hh in range(hps):
        q = q_ref[:, hh * HEAD_W:(hh + 1) * HEAD_W]
        zero = jnp.zeros_like(q)
        qqs.append(jnp.concatenate([jnp.where(lane < D_HA, q, zero), jnp.where(lane >= D_HA, q, zero)],
                                   axis=0))

    def step(j, carry, masked):
        off = pl.multiple_of(j * tk, tk)
        out = []
        for hh in range(hps):
            m, acc = carry[hh]
            sl = slice(hh * HEAD_W, (hh + 1) * HEAD_W)
            k = k_ref[pl.ds(off, tk), sl]
            vext = jnp.concatenate([v_ref[pl.ds(off, tk), sl], ones_blk], axis=1)
            s = lax.dot_general(qqs[hh], k, NT_DIMS, preferred_element_type=F32)
            if masked:
                s = jnp.where(causal, s, -jnp.inf)
            m_new = jnp.maximum(m, jnp.max(s, axis=1, keepdims=True))
            p = jnp.exp(s - m_new)
            acc = jnp.exp(m - m_new) * acc + jnp.dot(p.astype(BF16), vext, preferred_element_type=F32)
            out.append((m_new, acc))
        return tuple(out)

    init = tuple((jnp.full((2 * tq, 1), -jnp.inf, F32), jnp.zeros((2 * tq, 2 * HEAD_W), F32))
                 for _ in range(hps))
    carry = lax.fori_loop(0, nfull, lambda j, cr: step(j, cr, False), init)
    carry = step(nfull, carry, True)
    lam = _diff_lambda(lam_ref, lam_init)
    for hh in range(hps):
        acc = carry[hh][1]
        o1 = acc[:tq, :HEAD_W] / acc[:tq, HEAD_W:HEAD_W + 1]
        o2 = acc[tq:, :HEAD_W] / acc[tq:, HEAD_W:HEAD_W + 1]
        o = o1 - lam * o2
        y = o * lax.rsqrt(jnp.mean(o * o, axis=1, keepdims=True) + LN_EPS) * g_ref[...] * (1.0 - lam_init)
        o_ref[:, hh * HEAD_W:(hh + 1) * HEAD_W] = y.astype(o_ref.dtype)


def _attn_prompt(q, k, v, lam_p, subln_g, batch, seq, lam_init, side):
    t = q.shape[0]
    tk = _pick_tile(seq, ATTN_TK)
    tq = _pick_tile(tk, ATTN_TQ)
    nq = seq // tq
    hps = ATTN_HEADS_PER_STEP
    nh = H_A // hps
    w = hps * HEAD_W
    steps = batch * nh * nq
    side2 = [a.reshape(-1, a.shape[-1]) for a in side]
    slab = [a.shape[0] // steps for a in side2]
    if any(a.shape[0] % steps or s % 16 or s * a.shape[1] * 4 > SIDE_SLAB_BYTES for a, s in zip(side2, slab)):
        side2, slab = [], []
    side_specs = [pl.BlockSpec((s, a.shape[1]), lambda b, h, i: ((b * nh + h) * nq + i, 0))
                  for a, s in zip(side2, slab)]
    outs = pl.pallas_call(
        functools.partial(_attn_prompt_kernel, tq=tq, tk=tk, hps=hps, lam_init=lam_init),
        grid=(batch, nh, nq),
        in_specs=[pl.BlockSpec((tq, w), lambda b, h, i: (b * nq + i, h)),
                  pl.BlockSpec((seq, w), lambda b, h, i: (b, h)),
                  pl.BlockSpec((seq, w), lambda b, h, i: (b, h)),
                  pl.BlockSpec(lam_p.shape, lambda b, h, i: (0, 0)),
                  pl.BlockSpec((1, HEAD_W), lambda b, h, i: (0, 0))] + side_specs,
        out_specs=[pl.BlockSpec((tq, w), lambda b, h, i: (b * nq + i, h))] + side_specs,
        out_shape=[jax.ShapeDtypeStruct((t, QA_W), BF16)]
        + [jax.ShapeDtypeStruct(a.shape, BF16) for a in side2],
        compiler_params=_params("parallel", "parallel", "parallel"),
        name="attn_prompt",
    )(q, k, v, lam_p, subln_g, *side2)
    if side2:
        return outs[0], [o.reshape(a.shape) for o, a in zip(outs[1:], side)]
    return outs[0], [a.astype(BF16) for a in side]


def _mlstm_prompt_kernel(q_ref, k_ref, v_ref, gcol_ref, grow_ref, bcol_ref, brow_ref, og_ref, g_ref,
                         min_ref, cn_ref, m_ref, *, lc, bps):
    ci = pl.program_id(1)

    @pl.when(ci == 0)
    def _():
        cn_ref[...] = jnp.zeros_like(cn_ref)
        m_ref[...] = jnp.zeros_like(m_ref)

    for i in range(bps):
        _mlstm_chunk(q_ref.at[0, i], k_ref.at[0, i], v_ref.at[0, i], gcol_ref.at[0, i], grow_ref[:, 0, i, :],
                     bcol_ref, brow_ref, og_ref.at[0, i], g_ref, min_ref.at[0, i], cn_ref.at[0, i],
                     m_ref.at[0, i], lc)


def _mlstm_chunk(q_ref, k_ref, v_ref, gcol_ref, grow, bcol_ref, brow_ref, og_ref, g_ref,
                 min_ref, cn_ref, m_ref, lc):
    gc = _softcap(gcol_ref[...] + bcol_ref[...])
    lfc = _log_sigmoid(gc)
    gr = _softcap(grow + brow_ref[...])
    lfr = _log_sigmoid(gr)
    ri = lax.broadcasted_iota(jnp.int32, (lc, lc), 0)
    cj = lax.broadcasted_iota(jnp.int32, (lc, lc), 1)
    tril = cj <= ri
    b_col = jnp.dot(tril.astype(F32), lfc, precision=HIGHEST, preferred_element_type=F32)
    b_row = jnp.dot(lfr, (ri <= cj).astype(F32), precision=HIGHEST, preferred_element_type=F32)

    cn = cn_ref[...]
    cn_bf = cn.astype(BF16)
    q = q_ref[...]
    k = k_ref[...]
    v = v_ref[...]
    m_all = m_ref[...]
    lane_q = lax.broadcasted_iota(jnp.int32, (1, QM_W), 1) // DK_M
    row_q = lax.broadcasted_iota(jnp.int32, (QM_W, 1), 0) // DK_M
    lane_v = lax.broadcasted_iota(jnp.int32, (1, LANES), 1)
    ones_blk = jnp.broadcast_to(jnp.where(lane_v == 0, 1.0, 0.0).astype(BF16), (lc, LANES))
    new_cn = jnp.zeros_like(cn)
    new_m = m_all
    for h in range(H_M):
        sl = slice(h * DV_M, (h + 1) * DV_M)
        qh = jnp.where(lane_q == h, q, jnp.zeros_like(q))
        kh = jnp.where(lane_q == h, k, jnp.zeros_like(k))
        b_c = b_col[:, H_M + h:H_M + h + 1]
        ig_c = gc[:, h:h + 1]
        b_r = b_row[H_M + h:H_M + h + 1, :]
        ig_r = gr[h:h + 1, :]
        m_prev = m_all[:, h:h + 1]
        dmat = jnp.where(tril, b_c - b_r + ig_r, -jnp.inf)
        m_inter = b_c + m_prev
        m_t = jnp.maximum(m_inter, jnp.max(dmat, axis=1, keepdims=True))
        w_inter = jnp.exp(m_inter - m_t)
        s = lax.dot_general(qh, kh, NT_DIMS, preferred_element_type=F32) * jnp.exp(dmat - m_t)
        vext = jnp.concatenate([v[:, sl], ones_blk], axis=1)
        nd = (w_inter * jnp.dot(qh, cn_bf, preferred_element_type=F32)
              + jnp.dot(s.astype(BF16), vext, preferred_element_type=F32))
        num = nd[:, :DV_M]
        den = nd[:, DV_M:DV_M + 1]
        hout = num / jnp.maximum(jnp.abs(den), jnp.exp(-m_t))
        mu = jnp.mean(hout, axis=1, keepdims=True)
        hc = hout - mu
        var = jnp.mean(hc * hc, axis=1, keepdims=True)
        y = hc * lax.rsqrt(var + LN_EPS) * g_ref[:, sl] * _sigmoid(og_ref[:, sl].astype(F32))
        min_ref[:, sl] = y.astype(min_ref.dtype)
        m_new = m_t[lc - 1:lc, :]
        b_last = b_c[lc - 1:lc, :]
        decay = jnp.exp(b_last + m_prev - m_new)
        wk = jnp.exp(b_last - b_c + ig_c - m_new)
        upd = lax.dot_general(kh, (wk * vext.astype(F32)).astype(BF16), TN_DIMS,
                              preferred_element_type=F32)
        new_cn = new_cn + jnp.where(row_q == h, decay * cn, 0.0) + upd
        new_m = jnp.where(lane_v == h, m_new, new_m)
    cn_ref[...] = new_cn
    m_ref[...] = new_m


def _mlstm_prompt(qm, km, vm, gcol, grow, bcol, brow, og, mh_g, batch, seq, lc):
    t = qm.shape[0]
    bps = MLSTM_SEQS_PER_STEP if batch % MLSTM_SEQS_PER_STEP == 0 else 1
    nb = batch // bps
    row = lambda w: pl.BlockSpec((1, bps, lc, w), lambda b, c: (b, 0, c, 0))
    const = lambda a: pl.BlockSpec(a.shape, lambda b, c: (0,) * a.ndim)
    seqs = lambda a: a.reshape(nb, bps, seq, a.shape[1])
    cw = 2 * LANES
    m_in, cn, m = pl.pallas_call(
        functools.partial(_mlstm_prompt_kernel, lc=lc, bps=bps),
        grid=(nb, seq // lc),
        in_specs=[row(QM_W), row(QM_W), row(VM_W), row(LANES),
                  pl.BlockSpec((N_GATE, 1, bps, lc), lambda b, c: (0, b, 0, c)),
                  const(bcol), const(brow), row(VM_W), const(mh_g)],
        out_specs=[row(VM_W),
                   pl.BlockSpec((1, bps, QM_W, cw), lambda b, c: (b, 0, 0, 0)),
                   pl.BlockSpec((1, bps, 1, LANES), lambda b, c: (b, 0, 0, 0))],
        out_shape=[jax.ShapeDtypeStruct((nb, bps, seq, VM_W), BF16),
                   jax.ShapeDtypeStruct((nb, bps, QM_W, cw), F32),
                   jax.ShapeDtypeStruct((nb, bps, 1, LANES), F32)],
        compiler_params=_params("parallel", "arbitrary"),
        name="mlstm_prompt",
    )(seqs(qm), seqs(km), seqs(vm), seqs(gcol), grow.reshape(N_GATE, nb, bps, seq), bcol, brow,
      seqs(og), mh_g)
    return m_in.reshape(t, VM_W), cn.reshape(batch, QM_W, cw), m.reshape(batch, 1, LANES)


def _mix_kernel(a_ref, mi_ref, ga_ref, gm_ref, x_ref, wba_ref, wbm_ref, wo_ref, g1_ref, b1_ref,
                wrt_ref, brt_ref, h_ref, hb_ref, comb_ref, mask_ref, *, alpha, precise):
    def mm(a, w_ref):
        if precise:
            return jnp.dot(a.astype(F32), w_ref[...], precision=HIGHEST, preferred_element_type=F32)
        return jnp.dot(a.astype(BF16), w_ref[...], preferred_element_type=F32)

    a_br = mm(a_ref[...], wba_ref)
    m_br = mm(mi_ref[...], wbm_ref)
    merged = _sigmoid(ga_ref[...].astype(F32)) * a_br + _sigmoid(gm_ref[...].astype(F32)) * m_br
    mix = mm(merged, wo_ref)
    h = _layer_norm(alpha * x_ref[...] + mix, g1_ref[...], b1_ref[...])
    h_ref[...] = h
    hb_ref[...] = h.astype(BF16)

    ne = wrt_ref.shape[0]
    if precise:
        logits = lax.dot_general(wrt_ref[...], h, NT_DIMS, precision=HIGHEST, preferred_element_type=F32)
    else:
        ws = _stack_split(wrt_ref[...])
        h_hi, h_lo = _split_bf16(h)
        logits = (_fold_split(lax.dot_general(ws, h_hi, NT_DIMS, preferred_element_type=F32))
                  + lax.dot_general(ws[:ne], h_lo, NT_DIMS, preferred_element_type=F32))
    logits = logits + brt_ref[...]
    tm = logits.shape[1]
    eidx = lax.broadcasted_iota(jnp.int32, (ne, tm), 0)
    work = logits
    sel = jnp.zeros((ne, tm), jnp.bool_)
    top = None
    for _ in range(TOP_K):
        mx = jnp.max(work, axis=0, keepdims=True)
        top = mx if top is None else top
        first = jnp.min(jnp.where(work == mx, eidx, ne), axis=0, keepdims=True)
        pick = eidx == first
        sel = jnp.logical_or(sel, pick)
        work = jnp.where(pick, -jnp.inf, work)
    ex = jnp.where(sel, jnp.exp(logits - top), 0.0)
    comb_ref[...] = ex / jnp.sum(ex, axis=0, keepdims=True)
    mask_ref[...] = jnp.where(sel, 1.0, 0.0)


def _mix(a_in, m_in, ga, gm, x2, wba, wbm, wo, g1, b1, wrt, brt, alpha, precise, tm, hb_rows=None):
    t, d = x2.shape
    hb_rows = t if hb_rows is None else hb_rows
    ne = wrt.shape[0]
    row = lambda w: pl.BlockSpec((tm, w), lambda i: (i, 0))
    col = pl.BlockSpec((ne, tm), lambda i: (0, i))
    const = lambda a: pl.BlockSpec(a.shape, lambda i: (0,) * a.ndim)
    return pl.pallas_call(
        functools.partial(_mix_kernel, alpha=alpha, precise=precise),
        grid=(t // tm,),
        in_specs=[row(QA_W), row(VM_W), row(d), row(d), row(d), const(wba), const(wbm), const(wo),
                  const(g1), const(b1), const(wrt), const(brt)],
        out_specs=[row(d), row(d), col, col],
        out_shape=[jax.ShapeDtypeStruct((t, d), F32), jax.ShapeDtypeStruct((hb_rows, d), BF16),
                   jax.ShapeDtypeStruct((ne, t), F32), jax.ShapeDtypeStruct((ne, t), F32)],
        compiler_params=_params("parallel"),
        name="mix_sample" if precise else "mix_prompt",
    )(a_in, m_in, ga, gm, x2, wba, wbm, wo, g1, b1, wrt, brt)


def _append_rows_kernel(big_ref, tail_ref, o_ref):
    del big_ref
    o_ref[...] = jnp.zeros_like(o_ref)

    @pl.when(pl.program_id(0) == 0)
    def _():
        o_ref[:tail_ref.shape[0], :] = tail_ref[...]


def _append_rows(big, tail, start):
    rows, d = big.shape
    blk = math.gcd(start, rows - start)
    assert blk % 16 == 0 and blk >= tail.shape[0]
    return pl.pallas_call(
        _append_rows_kernel,
        grid=((rows - start) // blk,),
        in_specs=[pl.BlockSpec(memory_space=pl.ANY), pl.BlockSpec(tail.shape, lambda i: (0, 0))],
        out_specs=pl.BlockSpec((blk, d), lambda i: (start // blk + i, 0)),
        out_shape=jax.ShapeDtypeStruct(big.shape, big.dtype),
        input_output_aliases={0: 0},
        compiler_params=_params("arbitrary"),
        name="append_rows",
    )(big, tail)


def _route_kernel(mask_ref, pos_ref, cnt_ref):
    maskf = mask_ref[...]
    g = maskf.shape[1]
    r = lax.broadcasted_iota(jnp.int32, (g, g), 0)
    c = lax.broadcasted_iota(jnp.int32, (g, g), 1)
    before = jnp.dot(maskf.astype(BF16), (r < c).astype(BF16), preferred_element_type=F32)
    pos_ref[...] = jnp.where(maskf > 0.0, before, -1.0)
    cnt_ref[0] = jnp.broadcast_to(jnp.sum(maskf, axis=1, keepdims=True), cnt_ref.shape[1:])


def _route(mask, group):
    ne, t = mask.shape
    ng = t // group
    return pl.pallas_call(
        _route_kernel,
        grid=(ng,),
        in_specs=[pl.BlockSpec((ne, group), lambda g: (0, g))],
        out_specs=[pl.BlockSpec((ne, group), lambda g: (0, g)),
                   pl.BlockSpec((1, ne, LANES), lambda g: (g, 0, 0))],
        out_shape=[jax.ShapeDtypeStruct((ne, t), F32), jax.ShapeDtypeStruct((ng, ne, LANES), F32)],
        compiler_params=_params("parallel"),
        name="route",
    )(mask)


def _moe_kernel(cnt_ref, nact_ref, h_ref, pos_ref, comb_ref, wg_ref, bg_ref, wu_ref, bu_ref,
                wd_ref, bd_ref, o_ref, y_scr, *, rows, group, nsub):
    sg = pl.program_id(0)
    e = pl.program_id(1)
    ne = pl.num_programs(1)
    n_exp = pos_ref.shape[0]
    slab = y_scr.shape[1] // rows

    @pl.when(e == 0)
    def _():
        o_ref[...] = jnp.zeros_like(o_ref)

    slot0 = lax.broadcasted_iota(jnp.int32, (rows, 1), 0).astype(F32)

    def gather(hit, tok):
        return jnp.dot(jnp.where(hit, 1.0, 0.0).astype(BF16), h_ref[tok, :],
                       preferred_element_type=F32).astype(BF16)

    def ffn(hit, tok):
        return ffn_rows(gather(hit, tok))

    def ffn_rows(xg):
        gate = jnp.minimum(jnp.dot(xg, wg_ref[0], preferred_element_type=F32) + bg_ref[0], SWIGLU_LIMIT)
        up = jnp.clip(jnp.dot(xg, wu_ref[0], preferred_element_type=F32) + bu_ref[0],
                      -SWIGLU_LIMIT, SWIGLU_LIMIT)
        hid = gate * _sigmoid(SWIGLU_ALPHA * gate) * (up + 1.0)
        return (jnp.dot(hid.astype(BF16), wd_ref[0], preferred_element_type=F32) + bd_ref[0]).astype(BF16)

    yrows = pl.ds(pl.multiple_of(lax.rem(e, slab) * rows, rows), rows)

    def first_tile(sub):
        tok = slice(sub * group, (sub + 1) * group)
        y_scr[sub, yrows, :] = ffn(pos_ref[pl.ds(e, 1), tok] == slot0, tok)

    nact = nact_ref[sg]

    @pl.when(nact == nsub)
    def _():
        xg = jnp.concatenate(
            [gather(pos_ref[pl.ds(e, 1), sub * group:(sub + 1) * group] == slot0,
                    slice(sub * group, (sub + 1) * group)) for sub in range(nsub)], axis=0)
        y = ffn_rows(xg)
        for sub in range(nsub):
            y_scr[sub, yrows, :] = y[sub * rows:(sub + 1) * rows]

    @pl.when(nact < nsub)
    def _():
        for sub in range(nsub):
            pl.when(sub < nact)(functools.partial(first_tile, sub))

            @pl.when(sub >= nact)
            def _(sub=sub):
                y_scr[sub, yrows, :] = jnp.zeros((rows, y_scr.shape[2]), y_scr.dtype)

    for sub in range(nsub):
        tok = slice(sub * group, (sub + 1) * group)
        n = cnt_ref[(sg * nsub + sub) * ne + e]
        pos = pos_ref[pl.ds(e, 1), tok]
        cw = comb_ref[pl.ds(e, 1), tok]

        def tile(s, carry, tok=tok, pos=pos, cw=cw):
            hit = pos == slot0 + (s * rows).astype(F32)
            gw = jnp.where(hit, cw, 0.0).astype(BF16)
            o_ref[tok, :] += lax.dot_general(gw, ffn(hit, tok), TN_DIMS, preferred_element_type=F32)
            return carry

        lax.fori_loop(1, (n + rows - 1) // rows, tile, 0)

    def scatter_slab(e0):
        for sub in range(nsub):
            tok = slice(sub * group, (sub + 1) * group)
            for c0 in range(0, slab, MOE_SCATTER_EXPERTS):
                gw = jnp.concatenate(
                    [jnp.where(pos_ref[ee:ee + 1, tok] == slot0, comb_ref[ee:ee + 1, tok], 0.0).astype(BF16)
                     for ee in range(e0 + c0, e0 + c0 + MOE_SCATTER_EXPERTS)], axis=0)
                ys = y_scr[sub, c0 * rows:(c0 + MOE_SCATTER_EXPERTS) * rows, :]
                o_ref[tok, :] += lax.dot_general(gw, ys, TN_DIMS, preferred_element_type=F32)

    for e0 in range(0, n_exp, slab):
        pl.when(e == e0 + slab - 1)(functools.partial(scatter_slab, e0))


def _moe(counts, nact, hb, pos, comb, wg, bg, wu, bu, wd, bd, group, nsub, rows):
    t, d = hb.shape
    ne = wg.shape[0]
    sgroup = group * nsub
    tok = pl.BlockSpec((ne, sgroup), lambda g, e, c, a: (0, g))
    wspec = lambda w: pl.BlockSpec((1,) + w.shape[1:], lambda g, e, c, a: (e, 0, 0))
    grid_spec = pltpu.PrefetchScalarGridSpec(
        num_scalar_prefetch=2,
        grid=(t // sgroup, ne),
        in_specs=[pl.BlockSpec((sgroup, d), lambda g, e, c, a: (g, 0)),
                  tok, tok, wspec(wg), wspec(bg), wspec(wu), wspec(bu), wspec(wd), wspec(bd)],
        out_specs=pl.BlockSpec((sgroup, d), lambda g, e, c, a: (g, 0)),
        scratch_shapes=[pltpu.VMEM((nsub, MOE_SLAB_EXPERTS * rows, d), BF16)],
    )
    return pl.pallas_call(
        functools.partial(_moe_kernel, rows=rows, group=group, nsub=nsub),
        grid_spec=grid_spec,
        out_shape=jax.ShapeDtypeStruct((t, d), F32),
        compiler_params=_params("parallel", "arbitrary"),
        name="moe",
    )(counts, nact, hb, pos, comb, wg, bg, wu, bu, wd, bd)


def _ln2_kernel(h_ref, ff_ref, g_ref, b_ref, y_ref, *, alpha):
    y_ref[...] = _layer_norm(alpha * h_ref[...] + ff_ref[...], g_ref[...], b_ref[...])


def _ln2(h, ff, g2, b2, alpha, tm, ff_block0):
    t, d = h.shape
    const = lambda a: pl.BlockSpec(a.shape, lambda i: (0,) * a.ndim)
    return pl.pallas_call(
        functools.partial(_ln2_kernel, alpha=alpha),
        grid=(t // tm,),
        in_specs=[pl.BlockSpec((tm, d), lambda i: (i, 0)),
                  pl.BlockSpec((tm, d), lambda i: (ff_block0 + i, 0)), const(g2), const(b2)],
        out_specs=pl.BlockSpec((tm, d), lambda i: (i, 0)),
        out_shape=jax.ShapeDtypeStruct((t, d), F32),
        compiler_params=_params("parallel"),
        name="ln2",
    )(h, ff, g2, b2)


def _proj_sample_kernel(x_ref, w_ref, z_ref):
    z_ref[...] = jnp.dot(x_ref[...], w_ref[...], precision=HIGHEST, preferred_element_type=F32)


def _proj_sample(xs, w, chunk):
    n, d = xs.shape
    width = w.shape[1]
    return pl.pallas_call(
        _proj_sample_kernel,
        grid=(width // chunk,),
        in_specs=[pl.BlockSpec((n, d), lambda j: (0, 0)), pl.BlockSpec((d, chunk), lambda j: (0, j))],
        out_specs=pl.BlockSpec((n, chunk), lambda j: (0, j)),
        out_shape=jax.ShapeDtypeStruct((n, width), F32),
        compiler_params=_params("parallel"),
        name="proj_sample",
    )(xs, w)


def _paged_attn_kernel(pt_ref, z_ref, ra_ref, rb_ref, rc_ref, lam_ref, g_ref, *rest,
                       ppb, lam_init):
    k_refs = rest[:ppb]
    v_refs = rest[ppb:2 * ppb]
    kout_ref, vout_ref, a_ref, q_scr, qs_scr, m_scr, l_scr, acc_scr = rest[2 * ppb:]
    j = pl.program_id(1)
    nj = pl.num_programs(1)
    nrow = 2 * H_A
    prow = PAGE_SIZE * H_A
    ra, rb, rc = ra_ref[...], rb_ref[...], rc_ref[...]
    lane = lax.broadcasted_iota(jnp.int32, (1, HEAD_W), 1)

    def per_row(t):
        return jnp.concatenate([t[:, (r // 2) * HEAD_W:(r // 2 + 1) * HEAD_W] for r in range(nrow)], axis=0)

    @pl.when(j == 0)
    def _():
        z = z_ref[0]
        rows = []
        for h in range(H_A):
            qh = _rope(z[:, h * HEAD_W:(h + 1) * HEAD_W], ra, rb, rc) * (D_HA ** -0.5)
            rows += [jnp.where(lane < D_HA, qh, 0.0), jnp.where(lane >= D_HA, qh, 0.0)]
        q = jnp.concatenate(rows, axis=0)
        q_scr[...] = q
        qs_scr[...] = _stack_split(q)
        m_scr[...] = jnp.full_like(m_scr, -jnp.inf)
        l_scr[...] = jnp.zeros_like(l_scr)
        acc_scr[...] = jnp.zeros_like(acc_scr)

    qs = qs_scr[...]
    parts = []
    for kr in k_refs:
        k_hi, k_lo = _split_bf16(kr[0])
        t = (lax.dot_general(qs, k_hi, NT_DIMS, preferred_element_type=F32)
             + lax.dot_general(qs, k_lo, NT_DIMS, preferred_element_type=F32))
        parts.append(_fold_split(t))
    s = jnp.concatenate(parts, axis=1)
    r_head = lax.broadcasted_iota(jnp.int32, s.shape, 0) // 2
    c_head = lax.broadcasted_iota(jnp.int32, s.shape, 1) % H_A
    s = jnp.where(r_head == c_head, s, -jnp.inf)
    m_old = m_scr[...]
    m_new = jnp.maximum(m_old, jnp.max(s, axis=1, keepdims=True))
    alpha = jnp.exp(m_old - m_new)
    p = jnp.exp(s - m_new)
    l_scr[...] = alpha * l_scr[...] + jnp.sum(p, axis=1, keepdims=True)
    ps = _stack_split(p)
    pv = None
    for i, vr in enumerate(v_refs):
        v_hi, v_lo = _split_bf16(vr[0])
        pi = ps[:, i * prow:(i + 1) * prow]
        t = jnp.dot(pi, v_hi, preferred_element_type=F32) + jnp.dot(pi, v_lo, preferred_element_type=F32)
        pv = t if pv is None else pv + t
    acc_scr[...] = alpha * acc_scr[...] + _fold_split(pv)
    m_scr[...] = m_new

    @pl.when(j == nj - 1)
    def _():
        z = z_ref[0]
        k_new = jnp.concatenate([_rope(z[:, QA_W + h * HEAD_W:QA_W + (h + 1) * HEAD_W], ra, rb, rc)
                                 for h in range(H_A)], axis=1)
        v_new = z[:, 2 * QA_W:3 * QA_W]
        kout_ref[0] = k_new
        vout_ref[0] = v_new
        s_self = jnp.sum(q_scr[...] * per_row(k_new), axis=1, keepdims=True)
        m_old2 = m_scr[...]
        m_fin = jnp.maximum(m_old2, s_self)
        a2 = jnp.exp(m_old2 - m_fin)
        p_self = jnp.exp(s_self - m_fin)
        l_fin = a2 * l_scr[...] + p_self
        o_all = (a2 * acc_scr[...] + p_self * per_row(v_new)) / l_fin
        lam = _diff_lambda(lam_ref, lam_init)
        outs = []
        for h in range(H_A):
            o = o_all[2 * h:2 * h + 1] - lam * o_all[2 * h + 1:2 * h + 2]
            outs.append(o * lax.rsqrt(jnp.mean(o * o, axis=1, keepdims=True) + LN_EPS)
                        * g_ref[...] * (1.0 - lam_init))
        a_ref[0] = jnp.concatenate(outs, axis=1)


def _paged_attn(page_table, z3, tabs, lam_p, subln_g, ck, cv, ppb, lam_init):
    nb, npages = page_table.shape
    width = z3.shape[2]
    nj = npages // ppb
    pt = page_table.reshape(-1)
    prow = PAGE_SIZE * H_A
    nrow = 2 * H_A
    const = lambda a: pl.BlockSpec(a.shape, lambda b, j, p: (0,) * a.ndim)
    page = lambda i: pl.BlockSpec((1, prow, HEAD_W),
                                  lambda b, j, p: (p[b * npages + j * ppb + i], 0, 0))
    seq3 = lambda w: pl.BlockSpec((1, 1, w), lambda b, j, p: (b, 0, 0))
    grid_spec = pltpu.PrefetchScalarGridSpec(
        num_scalar_prefetch=1,
        grid=(nb, nj),
        in_specs=[seq3(width), const(tabs[0]), const(tabs[1]), const(tabs[2]), const(lam_p),
                  const(subln_g)] + [page(i) for i in range(ppb)] + [page(i) for i in range(ppb)],
        out_specs=[seq3(QA_W), seq3(QA_W), seq3(QA_W)],
        scratch_shapes=[pltpu.VMEM((nrow, HEAD_W), F32), pltpu.VMEM((2 * nrow, HEAD_W), BF16),
                        pltpu.VMEM((nrow, 1), F32), pltpu.VMEM((nrow, 1), F32),
                        pltpu.VMEM((nrow, HEAD_W), F32)],
    )
    sds = jax.ShapeDtypeStruct((nb, 1, QA_W), F32)
    return pl.pallas_call(
        functools.partial(_paged_attn_kernel, ppb=ppb, lam_init=lam_init),
        grid_spec=grid_spec,
        out_shape=[sds, sds, sds],
        compiler_params=_params("parallel", "arbitrary"),
        name="paged_attn",
    )(pt, z3, *tabs, lam_p, subln_g, *([ck] * ppb), *([cv] * ppb))


def _mlstm_step_kernel(gates_ref, m0_ref, big_ref, bfg_ref, qc_ref, kc_ref, v_ref, og_ref, c0_ref,
                       n0_ref, g_ref, min_ref, c_ref, n_ref, m_ref):
    spb = c0_ref.shape[0]
    for i in range(spb):
        _mlstm_step_one(pl.program_id(0) * spb + i, i, gates_ref, m0_ref, big_ref, bfg_ref, qc_ref, kc_ref,
                        v_ref, og_ref, c0_ref, n0_ref, g_ref, min_ref, c_ref, n_ref, m_ref)


def _mlstm_step_one(b, i, gates_ref, m0_ref, big_ref, bfg_ref, qc_ref, kc_ref, v_ref, og_ref, c0_ref,
                    n0_ref, g_ref, min_ref, c_ref, n_ref, m_ref):
    outs = []
    for h in range(H_M):
        sl = slice(h * DV_M, (h + 1) * DV_M)
        ig = _softcap(jnp.full((1, LANES), gates_ref[b, h] + big_ref[h], F32))
        lf = _log_sigmoid(_softcap(jnp.full((1, LANES), gates_ref[b, H_M + h] + bfg_ref[h], F32)))
        m0 = jnp.full((1, LANES), m0_ref[b, h], F32)
        m_inter = lf + m0
        m_t = jnp.maximum(m_inter, ig)
        w_inter = jnp.exp(m_inter - m_t)
        w_new = jnp.exp(ig - m_t)
        qc = qc_ref[i, h]
        kc = kc_ref[i, h] * (DK_M ** -0.5)
        vr = v_ref[i][:, sl]
        c0 = c0_ref[i, h]
        n0 = n0_ref[i, h]
        qk = jnp.sum(qc * kc, axis=0, keepdims=True)
        s = qk * w_new
        num = w_inter * jnp.sum(qc * c0, axis=0, keepdims=True) + s * vr
        den = w_inter * jnp.sum(qc * n0, axis=0, keepdims=True) + s
        hout = num / jnp.maximum(jnp.abs(den), jnp.exp(-m_t))
        mu = jnp.mean(hout, axis=1, keepdims=True)
        hc = hout - mu
        var = jnp.mean(hc * hc, axis=1, keepdims=True)
        outs.append(hc * lax.rsqrt(var + LN_EPS) * g_ref[:, sl] * _sigmoid(og_ref[i][:, sl]))
        c_ref[i, h] = w_inter * c0 + (w_new * kc) * vr
        n_ref[i, h] = w_inter[:, 0:1] * n0 + w_new[:, 0:1] * kc
        m_ref[i, h:h + 1, :] = m_t
    min_ref[i] = jnp.concatenate(outs, axis=1)


def _mlstm_step(gates, m0, big, bfg, qc, kc, v3, og3, c0, n0c, mh_g):
    nb = c0.shape[0]
    spb = _pick_tile(nb, 8)
    smem = pl.BlockSpec(memory_space=pltpu.SMEM)
    per = lambda a: pl.BlockSpec((spb,) + a.shape[1:], lambda b: (b,) + (0,) * (a.ndim - 1))
    const = lambda a: pl.BlockSpec(a.shape, lambda b: (0,) * a.ndim)
    return pl.pallas_call(
        _mlstm_step_kernel,
        grid=(nb // spb,),
        in_specs=[smem, smem, smem, smem, per(qc), per(kc), per(v3), per(og3), per(c0), per(n0c),
                  const(mh_g)],
        out_specs=[per(v3), per(c0), per(n0c), pl.BlockSpec((spb, H_M, LANES), lambda b: (b, 0, 0))],
        out_shape=[jax.ShapeDtypeStruct(v3.shape, F32), jax.ShapeDtypeStruct(c0.shape, F32),
                   jax.ShapeDtypeStruct(n0c.shape, F32),
                   jax.ShapeDtypeStruct((nb, H_M, LANES), F32)],
        compiler_params=_params("parallel"),
        name="mlstm_step",
    )(gates, m0, big, bfg, qc, kc, v3, og3, c0, n0c, mh_g)


def _pick_tile(n, target):
    t = min(n, target)
    while n % t:
        t //= 2
    return t


def _layer(l, depth, x_p, x_s, cache_k, cache_v, c0, n0, m0, page_table, w):
    (w_in, b_ig, b_fg, lq1, lk1, lq2, lk2, subln_g, mh_g, w_ba, w_bm, w_o, ln1_g, ln1_b,
     w_router, b_router, w_gate, b_gate, w_up, b_up, w_down, b_down, ln2_g, ln2_b) = w
    alpha = (2.0 * depth) ** 0.25
    lam_init = 0.8 - 0.6 * math.exp(-0.3 * l)
    bp, seq, d = x_p.shape
    ns = x_s.shape[0]
    t = bp * seq
    ne = w_router.shape[1]
    past = page_table.shape[1] * PAGE_SIZE

    w_a = w_in[:, :W_A]
    w_gt = w_in[:, W_A:W_A + N_GATE]
    w_r = w_in[:, W_A + N_GATE:]
    w_gc = jnp.pad(w_gt, ((0, 0), (0, LANES - N_GATE)))
    lam_p = jnp.stack([lq1, lk1, lq2, lk2])
    sub_g = subln_g.reshape(1, HEAD_W)
    mh_g2 = mh_g.reshape(1, VM_W)
    bcol = jnp.pad(jnp.concatenate([b_ig, b_fg]), (0, LANES - N_GATE)).reshape(1, LANES)
    brow = jnp.concatenate([b_ig, b_fg]).reshape(N_GATE, 1)
    g1, b1 = ln1_g.reshape(1, d), ln1_b.reshape(1, d)
    g2, b2 = ln2_g.reshape(1, d), ln2_b.reshape(1, d)
    wrt = w_router.T
    brt = b_router.reshape(ne, 1)

    x2 = x_p.reshape(t, d)
    tm = _pick_tile(seq, PROJ_TM)
    tabs_p = _rope_tables(jnp.arange(seq, dtype=jnp.int32))
    (qa, k_f, k_b, v_f, v_b, qm, km, vm, gcol, grow, og, ga, gm) = _proj_prompt(
        x2, w_a.astype(BF16), w_gc.astype(BF16), w_gt.T.astype(BF16), w_r.astype(BF16), tabs_p, seq, tm)
    a_in, (wg_b, wu_b, wd_b) = _attn_prompt(qa, k_b, v_b, lam_p, sub_g, bp, seq, lam_init,
                                            (w_gate, w_up, w_down))
    m_in, cn_p, m_p = _mlstm_prompt(qm, km, vm, gcol, grow, bcol, brow, og, mh_g2, bp, seq,
                                    _pick_tile(seq, MLSTM_CHUNK))
    sgroup = MOE_GROUP * MOE_NSUB
    t_pad = -(-(t + ns) // sgroup) * sgroup
    h_p, hb_p, comb_p, mask_p = _mix(
        a_in, m_in, ga, gm, x2, w_ba.astype(BF16), w_bm.astype(BF16), w_o.astype(BF16), g1, b1, wrt, brt,
        alpha, False, _pick_tile(seq, MIX_TM), hb_rows=t_pad)

    xs2 = x_s.reshape(ns, d)
    chunk = 7 * LANES
    width = -(-w_in.shape[1] // chunk) * chunk
    z_s = _proj_sample(xs2, jnp.pad(w_in, ((0, 0), (0, width - w_in.shape[1]))), chunk)
    tabs_s = _rope_tables(jnp.full((1,), past, jnp.int32))
    ppb = _pick_tile(page_table.shape[1], 32)
    n_pool = cache_k.shape[1]
    k_s, v_s, a_s = _paged_attn(page_table + l * n_pool, z_s.reshape(ns, 1, width), tabs_s, lam_p, sub_g,
                                cache_k.reshape(-1, PAGE_SIZE * H_A, HEAD_W),
                                cache_v.reshape(-1, PAGE_SIZE * H_A, HEAD_W), ppb, lam_init)
    o = 3 * QA_W
    qc = z_s[:, o:o + QM_W].reshape(ns, H_M, DK_M, 1)
    kc = z_s[:, o + QM_W:o + 2 * QM_W].reshape(ns, H_M, DK_M, 1)
    v3 = z_s[:, o + 2 * QM_W:W_A].reshape(ns, 1, VM_W)
    gates = z_s[:, W_A:W_A + N_GATE]
    r0 = W_A + N_GATE
    og3 = z_s[:, r0:r0 + VM_W].reshape(ns, 1, VM_W)
    ga_s = z_s[:, r0 + VM_W:r0 + VM_W + d]
    gm_s = z_s[:, r0 + VM_W + d:r0 + VM_W + 2 * d]
    m_in_s, c_s, n_s, m_s = _mlstm_step(gates, m0, b_ig, b_fg, qc, kc, v3, og3, c0,
                                        n0.reshape(ns, H_M, DK_M, 1), mh_g2)
    h_s, hb_s, comb_s, mask_s = _mix(
        a_s.reshape(ns, QA_W), m_in_s.reshape(ns, VM_W), ga_s, gm_s, xs2, w_ba, w_bm, w_o, g1, b1,
        wrt, brt, alpha, True, ns)

    padt = lambda p, s: jnp.pad(jnp.concatenate([p, s], axis=1), ((0, 0), (0, t_pad - t - ns)))
    hb_all = _append_rows(hb_p, hb_s, t)
    comb_all = padt(comb_p, comb_s)
    pos_all, cnt = _route(padt(mask_p, mask_s), MOE_GROUP)
    counts = cnt[:, :, 0].astype(jnp.int32)
    used = (jnp.sum(counts, axis=1) > 0).reshape(-1, MOE_NSUB)
    nact = jnp.max(jnp.where(used, jnp.arange(1, MOE_NSUB + 1, dtype=jnp.int32), 0), axis=1)
    ff = _moe(counts.reshape(-1), nact, hb_all, pos_all, comb_all,
              wg_b, b_gate.reshape(ne, 1, -1), wu_b, b_up.reshape(ne, 1, -1),
              wd_b, b_down.reshape(ne, 1, -1), MOE_GROUP, MOE_NSUB, MOE_ROWS)
    y_p = _ln2(h_p, ff, g2, b2, alpha, tm, 0)
    y_s = _ln2(h_s, ff, g2, b2, alpha, ns, t // ns)

    cw = cn_p[:, :, :DV_M].reshape(bp, H_M, DK_M, DV_M)
    nw = cn_p[:, :, DV_M].reshape(bp, H_M, DK_M)
    outs_p = (y_p.reshape(bp, seq, d), k_f.reshape(bp, seq, H_A, HEAD_W), v_f.reshape(bp, seq, H_A, HEAD_W),
              cw, nw, m_p[:, 0, :H_M])
    outs_s = (y_s.reshape(ns, 1, d), k_s.reshape(ns, 1, H_A, HEAD_W), v_s.reshape(ns, 1, H_A, HEAD_W),
              c_s, n_s.reshape(ns, H_M, DK_M), m_s[:, :, 0])
    return outs_p, outs_s


def kernel(x_prompt, x_sample, cache_k, cache_v, state_c, state_n, state_m, page_table, w_in, b_igate, b_fgate, lambda_q1, lambda_k1, lambda_q2, lambda_k2, subln_g, mh_norm_g, w_ba, w_bm, w_o, ln1_g, ln1_b, w_router, b_router, w_gate, b_gate, w_up, b_up, w_down, b_down, ln2_g, ln2_b):
    depth = w_in.shape[0]
    assert x_sample.shape[1] == 1, "the sample pass handles one new token per sequence"
    weights = (w_in, b_igate, b_fgate, lambda_q1, lambda_k1, lambda_q2, lambda_k2, subln_g, mh_norm_g,
               w_ba, w_bm, w_o, ln1_g, ln1_b, w_router, b_router, w_gate, b_gate, w_up, b_up,
               w_down, b_down, ln2_g, ln2_b)
    y_p, y_s = x_prompt, x_sample
    acc_p = [[] for _ in range(5)]
    acc_s = [[] for _ in range(5)]
    for l in range(depth):
        outs_p, outs_s = _layer(l, depth, y_p, y_s, cache_k, cache_v, state_c[l], state_n[l],
                                state_m[l], page_table, tuple(a[l] for a in weights))
        y_p, y_s = outs_p[0], outs_s[0]
        for i in range(5):
            acc_p[i].append(outs_p[1 + i])
            acc_s[i].append(outs_s[1 + i])
    return (y_p, y_s, *(jnp.stack(a) for a in acc_p), *(jnp.stack(a) for a in acc_s))
```

```python
import functools
import math

import jax
import jax.numpy as jnp
from jax import lax
from jax.experimental import pallas as pl
from jax.experimental.pallas import tpu as pltpu

F32 = jnp.float32
BF16 = jnp.bfloat16
HIGHEST = lax.Precision.HIGHEST

H_A = 4
D_HA = 64
ROT_DIM = D_HA // 4
ROPE_THETA = 500000.0
H_M = 4
DK_M = 64
DV_M = 128
GATE_SOFTCAP = 15.0
N_EXPERTS = 32
TOP_K = 4
SWIGLU_LIMIT = 7.0
SWIGLU_ALPHA = 1.702
LN_EPS = 1e-5
PAGE_SIZE = 128

QA_W = H_A * 2 * D_HA
QM_W = H_M * DK_M
VM_W = H_M * DV_M
HEAD_W = 2 * D_HA
N_GATE = 2 * H_M
W_A = 3 * QA_W + 2 * QM_W + VM_W

LANES = 128
VMEM_LIMIT = 56 * 1024 * 1024
SIDE_SLAB_BYTES = 2 * 1024 * 1024

ATTN_TQ = 256
ATTN_TK = 512
ATTN_HEADS_PER_STEP = 4
PROJ_TM = 512
MIX_TM = 1024
MLSTM_SEQS_PER_STEP = 2
MLSTM_CHUNK = 256
MOE_ROWS = 128
MOE_GROUP = 7 * LANES
MOE_NSUB = 2
MOE_SLAB_EXPERTS = N_EXPERTS
MOE_SCATTER_EXPERTS = 8
MOE_GATHER_EXPERTS = 8

NT_DIMS = (((1,), (1,)), ((), ()))
TN_DIMS = (((0,), (0,)), ((), ()))


def _params(*sem):
    return pltpu.CompilerParams(dimension_semantics=sem, vmem_limit_bytes=VMEM_LIMIT)


def _softcap(x):
    return GATE_SOFTCAP * jnp.tanh(x / GATE_SOFTCAP)


def _log_sigmoid(x):
    return jnp.minimum(x, 0.0) - jnp.log1p(jnp.exp(-jnp.abs(x)))


def _sigmoid(x):
    return 1.0 / (1.0 + jnp.exp(-x))


def _split_bf16(x):
    hi = x.astype(BF16)
    lo = (x - hi.astype(F32)).astype(BF16)
    return hi, lo


def _split3_bf16(x):
    hi = x.astype(BF16)
    r = x - hi.astype(F32)
    mid = r.astype(BF16)
    lo = (r - mid.astype(F32)).astype(BF16)
    return hi, mid, lo


def _stack_split(x):
    hi, lo = _split_bf16(x)
    return jnp.concatenate([hi, lo], axis=0)


def _fold_split(t):
    n = t.shape[0] // 2
    return t[:n] + t[n:]


def _layer_norm(x, g, b):
    mu = jnp.mean(x, axis=-1, keepdims=True)
    xc = x - mu
    var = jnp.mean(xc * xc, axis=-1, keepdims=True)
    return xc * lax.rsqrt(var + LN_EPS) * g + b


def _rope(t, ra, rb, rc):
    return t * ra + pltpu.roll(t, 8, 1) * rb + pltpu.roll(t, HEAD_W - 8, 1) * rc


def _rope_tables(pos):
    inv = ROPE_THETA ** (-jnp.arange(0, ROT_DIM, 2, dtype=F32) / ROT_DIM)
    ang = pos.astype(F32)[:, None] * inv[None, :]
    cos, sin = jnp.cos(ang), jnp.sin(ang)
    n = pos.shape[0]
    half = ROT_DIM // 2
    rest = D_HA - ROT_DIM
    a = jnp.concatenate([cos, cos, jnp.ones((n, rest), F32)], axis=1)
    b = jnp.concatenate([jnp.zeros((n, half), F32), sin, jnp.zeros((n, rest), F32)], axis=1)
    c = jnp.concatenate([-sin, jnp.zeros((n, half + rest), F32)], axis=1)
    return tuple(jnp.tile(t, (1, 2)) for t in (a, b, c))


def _diff_lambda(lam_ref, lam_init):
    lp = lam_ref[...]
    s1 = jnp.sum(lp[0:1] * lp[1:2], axis=1, keepdims=True)
    s2 = jnp.sum(lp[2:3] * lp[3:4], axis=1, keepdims=True)
    return jnp.exp(s1) - jnp.exp(s2) + lam_init


def _proj_prompt_kernel(x_ref, wa_ref, wgc_ref, wgr_ref, wr_ref, ra_ref, rb_ref, rc_ref,
                        q_ref, kf_ref, kb_ref, vf_ref, vb_ref, qm_ref, km_ref, vm_ref,
                        gcol_ref, grow_ref, og_ref, ga_ref, gm_ref):
    x = x_ref[...].astype(BF16)
    tm = x.shape[0]
    za = jnp.dot(x, wa_ref[...], preferred_element_type=F32)
    ra, rb, rc = ra_ref[...], rb_ref[...], rc_ref[...]
    for h in range(H_A):
        sl = slice(h * HEAD_W, (h + 1) * HEAD_W)
        qh = _rope(za[:, sl], ra, rb, rc)
        q_ref[:, sl] = (qh * (D_HA ** -0.5)).astype(BF16)
        kh = _rope(za[:, QA_W + h * HEAD_W:QA_W + (h + 1) * HEAD_W], ra, rb, rc)
        kf_ref[pl.ds(h, tm, stride=H_A), :] = kh
        kb_ref[:, sl] = kh.astype(BF16)
        vf_ref[pl.ds(h, tm, stride=H_A), :] = za[:, 2 * QA_W + h * HEAD_W:2 * QA_W + (h + 1) * HEAD_W]
    vb_ref[...] = za[:, 2 * QA_W:3 * QA_W].astype(BF16)
    o = 3 * QA_W
    qm_ref[...] = za[:, o:o + QM_W].astype(BF16)
    km_ref[...] = (za[:, o + QM_W:o + 2 * QM_W] * (DK_M ** -0.5)).astype(BF16)
    vm_ref[...] = za[:, o + 2 * QM_W:].astype(BF16)
    gcol_ref[...] = jnp.dot(x, wgc_ref[...], preferred_element_type=F32)
    grow_ref[...] = lax.dot_general(wgr_ref[...], x, NT_DIMS, preferred_element_type=F32)
    zr = jnp.dot(x, wr_ref[...], preferred_element_type=F32)
    d = ga_ref.shape[1]
    og_ref[...] = zr[:, :VM_W].astype(BF16)
    ga_ref[...] = zr[:, VM_W:VM_W + d].astype(BF16)
    gm_ref[...] = zr[:, VM_W + d:].astype(BF16)


def _proj_prompt(x2, wa, wgc, wgr, wr, tabs, seq, tm):
    t, d = x2.shape
    nrep = seq // tm
    row = lambda w: pl.BlockSpec((tm, w), lambda i: (i, 0))
    full = lambda a: pl.BlockSpec(a.shape, lambda i: (0,) * a.ndim, pipeline_mode=pl.Buffered(1))
    tab = pl.BlockSpec((tm, HEAD_W), lambda i: (i % nrep, 0))
    sds = lambda w, dt: jax.ShapeDtypeStruct((t, w), dt)
    kv_spec = pl.BlockSpec((tm * H_A, HEAD_W), lambda i: (i, 0))
    kv_sds = jax.ShapeDtypeStruct((t * H_A, HEAD_W), F32)
    return pl.pallas_call(
        _proj_prompt_kernel,
        grid=(t // tm,),
        in_specs=[row(d), full(wa), full(wgc), full(wgr), full(wr), tab, tab, tab],
        out_specs=[row(QA_W), kv_spec, row(QA_W), kv_spec, row(QA_W),
                   row(QM_W), row(QM_W), row(VM_W), row(LANES),
                   pl.BlockSpec((N_GATE, tm), lambda i: (0, i)),
                   row(VM_W), row(d), row(d)],
        out_shape=[sds(QA_W, BF16), kv_sds, sds(QA_W, BF16), kv_sds, sds(QA_W, BF16),
                   sds(QM_W, BF16), sds(QM_W, BF16), sds(VM_W, BF16), sds(LANES, F32),
                   jax.ShapeDtypeStruct((N_GATE, t), F32),
                   sds(VM_W, BF16), sds(d, BF16), sds(d, BF16)],
        compiler_params=_params("parallel"),
        name="proj_prompt",
    )(x2, wa, wgc, wgr, wr, *tabs)


def _attn_prompt_kernel(q_ref, k_ref, v_ref, lam_ref, g_ref, *rest, tq, tk, hps, lam_init):
    nside = (len(rest) - 1) // 2
    o_ref = rest[nside]
    for src, dst in zip(rest[:nside], rest[nside + 1:]):
        dst[...] = src[...].astype(dst.dtype)
    qi = pl.program_id(2)
    nfull = (qi * tq) // tk
    lane = lax.broadcasted_iota(jnp.int32, (1, HEAD_W), 1)
    r = lax.broadcasted_iota(jnp.int32, (tq, tk), 0) + qi * tq
    c = lax.broadcasted_iota(jnp.int32, (tq, tk), 1) + nfull * tk
    causal = jnp.concatenate([c <= r, c <= r], axis=0)
    ones_blk = jnp.broadcast_to(jnp.where(lane == 0, 1.0, 0.0).astype(BF16), (tk, HEAD_W))
    qqs = []
    for hh in range(hps):
        q = q_ref[:, hh * HEAD_W:(hh + 1) * HEAD_W]
        zero = jnp.zeros_like(q)
        qqs.append(jnp.concatenate([jnp.where(lane < D_HA, q, zero), jnp.where(lane >= D_HA, q, zero)],
                                   axis=0))

    def step(j, carry, masked):
        off = pl.multiple_of(j * tk, tk)
        out = []
        for hh in range(hps):
            m, acc = carry[hh]
            sl = slice(hh * HEAD_W, (hh + 1) * HEAD_W)
            k = k_ref[pl.ds(off, tk), sl]
            vext = jnp.concatenate([v_ref[pl.ds(off, tk), sl], ones_blk], axis=1)
            s = lax.dot_general(qqs[hh], k, NT_DIMS, preferred_element_type=F32)
            if masked:
                s = jnp.where(causal, s, -jnp.inf)
            m_new = jnp.maximum(m, jnp.max(s, axis=1, keepdims=True))
            p = jnp.exp(s - m_new)
            acc = jnp.exp(m - m_new) * acc + jnp.dot(p.astype(BF16), vext, preferred_element_type=F32)
            out.append((m_new, acc))
        return tuple(out)

    init = tuple((jnp.full((2 * tq, 1), -jnp.inf, F32), jnp.zeros((2 * tq, 2 * HEAD_W), F32))
                 for _ in range(hps))
    carry = lax.fori_loop(0, nfull, lambda j, cr: step(j, cr, False), init)
    carry = step(nfull, carry, True)
    lam = _diff_lambda(lam_ref, lam_init)
    for hh in range(hps):
        acc = carry[hh][1]
        o1 = acc[:tq, :HEAD_W] / acc[:tq, HEAD_W:HEAD_W + 1]
        o2 = acc[tq:, :HEAD_W] / acc[tq:, HEAD_W:HEAD_W + 1]
        o = o1 - lam * o2
        y = o * lax.rsqrt(jnp.mean(o * o, axis=1, keepdims=True) + LN_EPS) * g_ref[...] * (1.0 - lam_init)
        o_ref[:, hh * HEAD_W:(hh + 1) * HEAD_W] = y.astype(o_ref.dtype)


def _attn_prompt(q, k, v, lam_p, subln_g, batch, seq, lam_init, side):
    t = q.shape[0]
    tk = _pick_tile(seq, ATTN_TK)
    tq = _pick_tile(tk, ATTN_TQ)
    nq = seq // tq
    hps = ATTN_HEADS_PER_STEP
    nh = H_A // hps
    w = hps * HEAD_W
    steps = batch * nh * nq
    side2 = [a.reshape(-1, a.shape[-1]) for a in side]
    slab = [a.shape[0] // steps for a in side2]
    if any(a.shape[0] % steps or s % 16 or s * a.shape[1] * 4 > SIDE_SLAB_BYTES for a, s in zip(side2, slab)):
        side2, slab = [], []
    side_specs = [pl.BlockSpec((s, a.shape[1]), lambda b, h, i: ((b * nh + h) * nq + i, 0))
                  for a, s in zip(side2, slab)]
    outs = pl.pallas_call(
        functools.partial(_attn_prompt_kernel, tq=tq, tk=tk, hps=hps, lam_init=lam_init),
        grid=(batch, nh, nq),
        in_specs=[pl.BlockSpec((tq, w), lambda b, h, i: (b * nq + i, h)),
                  pl.BlockSpec((seq, w), lambda b, h, i: (b, h)),
                  pl.BlockSpec((seq, w), lambda b, h, i: (b, h)),
                  pl.BlockSpec(lam_p.shape, lambda b, h, i: (0, 0)),
                  pl.BlockSpec((1, HEAD_W), lambda b, h, i: (0, 0))] + side_specs,
        out_specs=[pl.BlockSpec((tq, w), lambda b, h, i: (b * nq + i, h))] + side_specs,
        out_shape=[jax.ShapeDtypeStruct((t, QA_W), BF16)]
        + [jax.ShapeDtypeStruct(a.shape, BF16) for a in side2],
        compiler_params=_params("parallel", "parallel", "parallel"),
        name="attn_prompt",
    )(q, k, v, lam_p, subln_g, *side2)
    if side2:
        return outs[0], [o.reshape(a.shape) for o, a in zip(outs[1:], side)]
    return outs[0], [a.astype(BF16) for a in side]


def _mlstm_prompt_kernel(q_ref, k_ref, v_ref, gcol_ref, grow_ref, bcol_ref, brow_ref, og_ref, g_ref,
                         min_ref, cn_ref, m_ref, *, lc, bps):
    ci = pl.program_id(1)

    @pl.when(ci == 0)
    def _():
        cn_ref[...] = jnp.zeros_like(cn_ref)
        m_ref[...] = jnp.zeros_like(m_ref)

    for i in range(bps):
        _mlstm_chunk(q_ref.at[0, i], k_ref.at[0, i], v_ref.at[0, i], gcol_ref.at[0, i], grow_ref[:, 0, i, :],
                     bcol_ref, brow_ref, og_ref.at[0, i], g_ref, min_ref.at[0, i], cn_ref.at[0, i],
                     m_ref.at[0, i], lc)


def _mlstm_chunk(q_ref, k_ref, v_ref, gcol_ref, grow, bcol_ref, brow_ref, og_ref, g_ref,
                 min_ref, cn_ref, m_ref, lc):
    gc = _softcap(gcol_ref[...] + bcol_ref[...])
    lfc = _log_sigmoid(gc)
    gr = _softcap(grow + brow_ref[...])
    lfr = _log_sigmoid(gr)
    ri = lax.broadcasted_iota(jnp.int32, (lc, lc), 0)
    cj = lax.broadcasted_iota(jnp.int32, (lc, lc), 1)
    tril = cj <= ri
    pc = _split3_bf16(lfc)
    t = jnp.dot(jnp.where(tril, 1.0, 0.0).astype(BF16), jnp.concatenate(pc, axis=1),
                preferred_element_type=F32)
    b_col = t[:, :LANES] + t[:, LANES:2 * LANES] + t[:, 2 * LANES:]
    pr = _split3_bf16(lfr)
    t = jnp.dot(jnp.concatenate(pr + (jnp.zeros_like(pr[0]),), axis=0),
                jnp.where(ri <= cj, 1.0, 0.0).astype(BF16), preferred_element_type=F32)
    b_row = t[:N_GATE] + t[N_GATE:2 * N_GATE] + t[2 * N_GATE:3 * N_GATE]

    cn = cn_ref[...]
    cn_bf = cn.astype(BF16)
    q = q_ref[...]
    k = k_ref[...]
    v = v_ref[...]
    m_all = m_ref[...]
    lane_q = lax.broadcasted_iota(jnp.int32, (1, QM_W), 1) // DK_M
    row_q = lax.broadcasted_iota(jnp.int32, (QM_W, 1), 0) // DK_M
    lane_v = lax.broadcasted_iota(jnp.int32, (1, LANES), 1)
    ones_blk = jnp.broadcast_to(jnp.where(lane_v == 0, 1.0, 0.0).astype(BF16), (lc, LANES))
    new_cn = jnp.zeros_like(cn)
    new_m = m_all
    for h in range(H_M):
        sl = slice(h * DV_M, (h + 1) * DV_M)
        qh = jnp.where(lane_q == h, q, jnp.zeros_like(q))
        kh = jnp.where(lane_q == h, k, jnp.zeros_like(k))
        b_c = b_col[:, H_M + h:H_M + h + 1]
        ig_c = gc[:, h:h + 1]
        b_r = b_row[H_M + h:H_M + h + 1, :]
        ig_r = gr[h:h + 1, :]
        m_prev = m_all[:, h:h + 1]
        dmat = jnp.where(tril, b_c - b_r + ig_r, -jnp.inf)
        m_inter = b_c + m_prev
        m_t = jnp.maximum(m_inter, jnp.max(dmat, axis=1, keepdims=True))
        w_inter = jnp.exp(m_inter - m_t)
        s = lax.dot_general(qh, kh, NT_DIMS, preferred_element_type=F32) * jnp.exp(dmat - m_t)
        vext = jnp.concatenate([v[:, sl], ones_blk], axis=1)
        nd = (w_inter * jnp.dot(qh, cn_bf, preferred_element_type=F32)
              + jnp.dot(s.astype(BF16), vext, preferred_element_type=F32))
        num = nd[:, :DV_M]
        den = nd[:, DV_M:DV_M + 1]
        hout = num / jnp.maximum(jnp.abs(den), jnp.exp(-m_t))
        mu = jnp.mean(hout, axis=1, keepdims=True)
        hc = hout - mu
        var = jnp.mean(hc * hc, axis=1, keepdims=True)
        y = hc * lax.rsqrt(var + LN_EPS) * g_ref[:, sl] * _sigmoid(og_ref[:, sl].astype(F32))
        min_ref[:, sl] = y.astype(min_ref.dtype)
        m_new = m_t[lc - 1:lc, :]
        b_last = b_c[lc - 1:lc, :]
        decay = jnp.exp(b_last + m_prev - m_new)
        wk = jnp.exp(b_last - b_c + ig_c - m_new)
        upd = lax.dot_general(kh, (wk * vext.astype(F32)).astype(BF16), TN_DIMS,
                              preferred_element_type=F32)
        new_cn = new_cn + jnp.where(row_q == h, decay * cn, 0.0) + upd
        new_m = jnp.where(lane_v == h, m_new, new_m)
    cn_ref[...] = new_cn
    m_ref[...] = new_m


def _mlstm_prompt(qm, km, vm, gcol, grow, bcol, brow, og, mh_g, batch, seq, lc):
    t = qm.shape[0]
    bps = MLSTM_SEQS_PER_STEP if batch % MLSTM_SEQS_PER_STEP == 0 else 1
    nb = batch // bps
    row = lambda w: pl.BlockSpec((1, bps, lc, w), lambda b, c: (b, 0, c, 0))
    const = lambda a: pl.BlockSpec(a.shape, lambda b, c: (0,) * a.ndim)
    seqs = lambda a: a.reshape(nb, bps, seq, a.shape[1])
    cw = 2 * LANES
    m_in, cn, m = pl.pallas_call(
        functools.partial(_mlstm_prompt_kernel, lc=lc, bps=bps),
        grid=(nb, seq // lc),
        in_specs=[row(QM_W), row(QM_W), row(VM_W), row(LANES),
                  pl.BlockSpec((N_GATE, 1, bps, lc), lambda b, c: (0, b, 0, c)),
                  const(bcol), const(brow), row(VM_W), const(mh_g)],
        out_specs=[row(VM_W),
                   pl.BlockSpec((1, bps, QM_W, cw), lambda b, c: (b, 0, 0, 0)),
                   pl.BlockSpec((1, bps, 1, LANES), lambda b, c: (b, 0, 0, 0))],
        out_shape=[jax.ShapeDtypeStruct((nb, bps, seq, VM_W), BF16),
                   jax.ShapeDtypeStruct((nb, bps, QM_W, cw), F32),
                   jax.ShapeDtypeStruct((nb, bps, 1, LANES), F32)],
        compiler_params=_params("parallel", "arbitrary"),
        name="mlstm_prompt",
    )(seqs(qm), seqs(km), seqs(vm), seqs(gcol), grow.reshape(N_GATE, nb, bps, seq), bcol, brow,
      seqs(og), mh_g)
    return m_in.reshape(t, VM_W), cn.reshape(batch, QM_W, cw), m.reshape(batch, 1, LANES)


def _mix_kernel(a_ref, mi_ref, ga_ref, gm_ref, x_ref, wba_ref, wbm_ref, wo_ref, g1_ref, b1_ref,
                wrt_ref, brt_ref, h_ref, hb_ref, comb_ref, mask_ref, *, alpha, precise):
    def mm(a, w_ref):
        if precise:
            return jnp.dot(a.astype(F32), w_ref[...], precision=HIGHEST, preferred_element_type=F32)
        return jnp.dot(a.astype(BF16), w_ref[...], preferred_element_type=F32)

    a_br = mm(a_ref[...], wba_ref)
    m_br = mm(mi_ref[...], wbm_ref)
    merged = _sigmoid(ga_ref[...].astype(F32)) * a_br + _sigmoid(gm_ref[...].astype(F32)) * m_br
    mix = mm(merged, wo_ref)
    h = _layer_norm(alpha * x_ref[...] + mix, g1_ref[...], b1_ref[...])
    h_ref[...] = h
    hb_ref[...] = h.astype(BF16)

    ne = wrt_ref.shape[0]
    if precise:
        logits = lax.dot_general(wrt_ref[...], h, NT_DIMS, precision=HIGHEST, preferred_element_type=F32)
    else:
        ws = _stack_split(wrt_ref[...])
        h_hi, h_lo = _split_bf16(h)
        logits = (_fold_split(lax.dot_general(ws, h_hi, NT_DIMS, preferred_element_type=F32))
                  + lax.dot_general(ws[:ne], h_lo, NT_DIMS, preferred_element_type=F32))
    logits = logits + brt_ref[...]
    tm = logits.shape[1]
    eidx = lax.broadcasted_iota(jnp.int32, (ne, tm), 0)
    work = logits
    sel = jnp.zeros((ne, tm), jnp.bool_)
    top = None
    for _ in range(TOP_K):
        mx = jnp.max(work, axis=0, keepdims=True)
        top = mx if top is None else top
        first = jnp.min(jnp.where(work == mx, eidx, ne), axis=0, keepdims=True)
        pick = eidx == first
        sel = jnp.logical_or(sel, pick)
        work = jnp.where(pick, -jnp.inf, work)
    ex = jnp.where(sel, jnp.exp(logits - top), 0.0)
    comb_ref[...] = ex / jnp.sum(ex, axis=0, keepdims=True)
    mask_ref[...] = jnp.where(sel, 1.0, 0.0)


def _mix(a_in, m_in, ga, gm, x2, wba, wbm, wo, g1, b1, wrt, brt, alpha, precise, tm, hb_rows=None):
    t, d = x2.shape
    hb_rows = t if hb_rows is None else hb_rows
    ne = wrt.shape[0]
    row = lambda w: pl.BlockSpec((tm, w), lambda i: (i, 0))
    col = pl.BlockSpec((ne, tm), lambda i: (0, i))
    const = lambda a: pl.BlockSpec(a.shape, lambda i: (0,) * a.ndim)
    return pl.pallas_call(
        functools.partial(_mix_kernel, alpha=alpha, precise=precise),
        grid=(t // tm,),
        in_specs=[row(QA_W), row(VM_W), row(d), row(d), row(d), const(wba), const(wbm), const(wo),
                  const(g1), const(b1), const(wrt), const(brt)],
        out_specs=[row(d), row(d), col, col],
        out_shape=[jax.ShapeDtypeStruct((t, d), F32), jax.ShapeDtypeStruct((hb_rows, d), BF16),
                   jax.ShapeDtypeStruct((ne, t), F32), jax.ShapeDtypeStruct((ne, t), F32)],
        compiler_params=_params("parallel"),
        name="mix_sample" if precise else "mix_prompt",
    )(a_in, m_in, ga, gm, x2, wba, wbm, wo, g1, b1, wrt, brt)


def _append_rows_kernel(big_ref, tail_ref, o_ref):
    del big_ref
    o_ref[...] = jnp.zeros_like(o_ref)

    @pl.when(pl.program_id(0) == 0)
    def _():
        o_ref[:tail_ref.shape[0], :] = tail_ref[...]


def _append_rows(big, tail, start):
    rows, d = big.shape
    blk = math.gcd(start, rows - start)
    assert blk % 16 == 0 and blk >= tail.shape[0]
    return pl.pallas_call(
        _append_rows_kernel,
        grid=((rows - start) // blk,),
        in_specs=[pl.BlockSpec(memory_space=pl.ANY), pl.BlockSpec(tail.shape, lambda i: (0, 0))],
        out_specs=pl.BlockSpec((blk, d), lambda i: (start // blk + i, 0)),
        out_shape=jax.ShapeDtypeStruct(big.shape, big.dtype),
        input_output_aliases={0: 0},
        compiler_params=_params("arbitrary"),
        name="append_rows",
    )(big, tail)


def _route_kernel(mask_ref, pos_ref, cnt_ref):
    maskf = mask_ref[...]
    g = maskf.shape[1]
    r = lax.broadcasted_iota(jnp.int32, (g, g), 0)
    c = lax.broadcasted_iota(jnp.int32, (g, g), 1)
    before = jnp.dot(maskf.astype(BF16), (r < c).astype(BF16), preferred_element_type=F32)
    pos_ref[...] = jnp.where(maskf > 0.0, before, -1.0)
    cnt_ref[0] = jnp.broadcast_to(jnp.sum(maskf, axis=1, keepdims=True), cnt_ref.shape[1:])


def _route(mask, group):
    ne, t = mask.shape
    ng = t // group
    return pl.pallas_call(
        _route_kernel,
        grid=(ng,),
        in_specs=[pl.BlockSpec((ne, group), lambda g: (0, g))],
        out_specs=[pl.BlockSpec((ne, group), lambda g: (0, g)),
                   pl.BlockSpec((1, ne, LANES), lambda g: (g, 0, 0))],
        out_shape=[jax.ShapeDtypeStruct((ne, t), F32), jax.ShapeDtypeStruct((ng, ne, LANES), F32)],
        compiler_params=_params("parallel"),
        name="route",
    )(mask)


def _moe_kernel(cnt_ref, nact_ref, h_ref, pos_ref, comb_ref, wg_ref, bg_ref, wu_ref, bu_ref,
                wd_ref, bd_ref, o_ref, y_scr, xg_scr, *, rows, group, nsub):
    sg = pl.program_id(0)
    e = pl.program_id(1)
    ne = pl.num_programs(1)
    n_exp = pos_ref.shape[0]
    slab = y_scr.shape[1] // rows

    @pl.when(e == 0)
    def _():
        o_ref[...] = jnp.zeros_like(o_ref)

    slot0 = lax.broadcasted_iota(jnp.int32, (rows, 1), 0).astype(F32)

    def gather(hit, tok):
        return jnp.dot(jnp.where(hit, 1.0, 0.0).astype(BF16), h_ref[tok, :],
                       preferred_element_type=F32).astype(BF16)

    def ffn(hit, tok):
        return ffn_rows(gather(hit, tok))

    def ffn_rows(xg):
        gate = jnp.minimum(jnp.dot(xg, wg_ref[0], preferred_element_type=F32) + bg_ref[0], SWIGLU_LIMIT)
        up = jnp.clip(jnp.dot(xg, wu_ref[0], preferred_element_type=F32) + bu_ref[0],
                      -SWIGLU_LIMIT, SWIGLU_LIMIT)
        hid = gate * _sigmoid(SWIGLU_ALPHA * gate) * (up + 1.0)
        return (jnp.dot(hid.astype(BF16), wd_ref[0], preferred_element_type=F32) + bd_ref[0]).astype(BF16)

    yrows = pl.ds(pl.multiple_of(lax.rem(e, slab) * rows, rows), rows)
    nact = nact_ref[sg]

    gb = xg_scr.shape[1] // rows
    e_loc = lax.rem(e, gb)
    xrows = pl.ds(pl.multiple_of(e_loc * rows, rows), rows)

    def gather_batch(sub):
        tok = slice(sub * group, (sub + 1) * group)
        onehot = jnp.concatenate(
            [jnp.where(pos_ref[pl.ds(e + i, 1), tok] == slot0, 1.0, 0.0).astype(BF16) for i in range(gb)],
            axis=0)
        xg_scr[sub] = jnp.dot(onehot, h_ref[tok, :], preferred_element_type=F32).astype(BF16)

    @pl.when(e_loc == 0)
    def _():
        for sub in range(nsub):
            pl.when(sub < nact)(functools.partial(gather_batch, sub))

    def first_tile(sub):
        y_scr[sub, yrows, :] = ffn_rows(xg_scr[sub, xrows, :])

    @pl.when(nact == nsub)
    def _():
        y = ffn_rows(jnp.concatenate([xg_scr[sub, xrows, :] for sub in range(nsub)], axis=0))
        for sub in range(nsub):
            y_scr[sub, yrows, :] = y[sub * rows:(sub + 1) * rows]

    @pl.when(nact < nsub)
    def _():
        for sub in range(nsub):
            pl.when(sub < nact)(functools.partial(first_tile, sub))

            @pl.when(sub >= nact)
            def _(sub=sub):
                y_scr[sub, yrows, :] = jnp.zeros((rows, y_scr.shape[2]), y_scr.dtype)

    for sub in range(nsub):
        tok = slice(sub * group, (sub + 1) * group)
        n = cnt_ref[(sg * nsub + sub) * ne + e]
        pos = pos_ref[pl.ds(e, 1), tok]
        cw = comb_ref[pl.ds(e, 1), tok]

        def tile(s, carry, tok=tok, pos=pos, cw=cw):
            hit = pos == slot0 + (s * rows).astype(F32)
            gw = jnp.where(hit, cw, 0.0).astype(BF16)
            o_ref[tok, :] += lax.dot_general(gw, ffn(hit, tok), TN_DIMS, preferred_element_type=F32)
            return carry

        lax.fori_loop(1, (n + rows - 1) // rows, tile, 0)

    def scatter_slab(e0):
        for sub in range(nsub):
            tok = slice(sub * group, (sub + 1) * group)
            for c0 in range(0, slab, MOE_SCATTER_EXPERTS):
                gw = jnp.concatenate(
                    [jnp.where(pos_ref[ee:ee + 1, tok] == slot0, comb_ref[ee:ee + 1, tok], 0.0).astype(BF16)
                     for ee in range(e0 + c0, e0 + c0 + MOE_SCATTER_EXPERTS)], axis=0)
                ys = y_scr[sub, c0 * rows:(c0 + MOE_SCATTER_EXPERTS) * rows, :]
                o_ref[tok, :] += lax.dot_general(gw, ys, TN_DIMS, preferred_element_type=F32)

    for e0 in range(0, n_exp, slab):
        pl.when(e == e0 + slab - 1)(functools.partial(scatter_slab, e0))


def _moe(counts, nact, hb, pos, comb, wg, bg, wu, bu, wd, bd, group, nsub, rows):
    t, d = hb.shape
    ne = wg.shape[0]
    sgroup = group * nsub
    tok = pl.BlockSpec((ne, sgroup), lambda g, e, c, a: (0, g))
    wspec = lambda w: pl.BlockSpec((1,) + w.shape[1:], lambda g, e, c, a: (e, 0, 0))
    grid_spec = pltpu.PrefetchScalarGridSpec(
        num_scalar_prefetch=2,
        grid=(t // sgroup, ne),
        in_specs=[pl.BlockSpec((sgroup, d), lambda g, e, c, a: (g, 0), pipeline_mode=pl.Buffered(1)),
                  tok, tok, wspec(wg), wspec(bg), wspec(wu), wspec(bu), wspec(wd), wspec(bd)],
        out_specs=pl.BlockSpec((sgroup, d), lambda g, e, c, a: (g, 0)),
        scratch_shapes=[pltpu.VMEM((nsub, MOE_SLAB_EXPERTS * rows, d), BF16),
                        pltpu.VMEM((nsub, MOE_GATHER_EXPERTS * rows, d), BF16)],
    )
    return pl.pallas_call(
        functools.partial(_moe_kernel, rows=rows, group=group, nsub=nsub),
        grid_spec=grid_spec,
        out_shape=jax.ShapeDtypeStruct((t, d), F32),
        compiler_params=_params("parallel", "arbitrary"),
        name="moe",
    )(counts, nact, hb, pos, comb, wg, bg, wu, bu, wd, bd)


def _ln2_kernel(h_ref, ff_ref, g_ref, b_ref, y_ref, *, alpha):
    y_ref[...] = _layer_norm(alpha * h_ref[...] + ff_ref[...], g_ref[...], b_ref[...])


def _ln2(h, ff, g2, b2, alpha, tm, ff_block0):
    t, d = h.shape
    const = lambda a: pl.BlockSpec(a.shape, lambda i: (0,) * a.ndim)
    return pl.pallas_call(
        functools.partial(_ln2_kernel, alpha=alpha),
        grid=(t // tm,),
        in_specs=[pl.BlockSpec((tm, d), lambda i: (i, 0)),
                  pl.BlockSpec((tm, d), lambda i: (ff_block0 + i, 0)), const(g2), const(b2)],
        out_specs=pl.BlockSpec((tm, d), lambda i: (i, 0)),
        out_shape=jax.ShapeDtypeStruct((t, d), F32),
        compiler_params=_params("parallel"),
        name="ln2",
    )(h, ff, g2, b2)


def _proj_sample_kernel(x_ref, w_ref, z_ref):
    z_ref[...] = jnp.dot(x_ref[...], w_ref[...], precision=HIGHEST, preferred_element_type=F32)


def _proj_sample(xs, w, chunk):
    n, d = xs.shape
    width = w.shape[1]
    return pl.pallas_call(
        _proj_sample_kernel,
        grid=(width // chunk,),
        in_specs=[pl.BlockSpec((n, d), lambda j: (0, 0)), pl.BlockSpec((d, chunk), lambda j: (0, j))],
        out_specs=pl.BlockSpec((n, chunk), lambda j: (0, j)),
        out_shape=jax.ShapeDtypeStruct((n, width), F32),
        compiler_params=_params("parallel"),
        name="proj_sample",
    )(xs, w)


def _paged_attn_kernel(pt_ref, z_ref, ra_ref, rb_ref, rc_ref, lam_ref, g_ref, *rest,
                       ppb, lam_init):
    k_refs = rest[:ppb]
    v_refs = rest[ppb:2 * ppb]
    kout_ref, vout_ref, a_ref, q_scr, qs_scr, m_scr, l_scr, acc_scr = rest[2 * ppb:]
    j = pl.program_id(1)
    nj = pl.num_programs(1)
    nrow = 2 * H_A
    prow = PAGE_SIZE * H_A
    ra, rb, rc = ra_ref[...], rb_ref[...], rc_ref[...]
    lane = lax.broadcasted_iota(jnp.int32, (1, HEAD_W), 1)

    def per_row(t):
        return jnp.concatenate([t[:, (r // 2) * HEAD_W:(r // 2 + 1) * HEAD_W] for r in range(nrow)], axis=0)

    @pl.when(j == 0)
    def _():
        z = z_ref[0]
        rows = []
        for h in range(H_A):
            qh = _rope(z[:, h * HEAD_W:(h + 1) * HEAD_W], ra, rb, rc) * (D_HA ** -0.5)
            rows += [jnp.where(lane < D_HA, qh, 0.0), jnp.where(lane >= D_HA, qh, 0.0)]
        q = jnp.concatenate(rows, axis=0)
        q_scr[...] = q
        qs_scr[...] = _stack_split(q)
        m_scr[...] = jnp.full_like(m_scr, -jnp.inf)
        l_scr[...] = jnp.zeros_like(l_scr)
        acc_scr[...] = jnp.zeros_like(acc_scr)

    qs = qs_scr[...]
    parts = []
    for kr in k_refs:
        k_hi, k_lo = _split_bf16(kr[0])
        t = (lax.dot_general(qs, k_hi, NT_DIMS, preferred_element_type=F32)
             + lax.dot_general(qs, k_lo, NT_DIMS, preferred_element_type=F32))
        parts.append(_fold_split(t))
    s = jnp.concatenate(parts, axis=1)
    r_head = lax.broadcasted_iota(jnp.int32, s.shape, 0) // 2
    c_head = lax.broadcasted_iota(jnp.int32, s.shape, 1) % H_A
    s = jnp.where(r_head == c_head, s, -jnp.inf)
    m_old = m_scr[...]
    m_new = jnp.maximum(m_old, jnp.max(s, axis=1, keepdims=True))
    alpha = jnp.exp(m_old - m_new)
    p = jnp.exp(s - m_new)
    l_scr[...] = alpha * l_scr[...] + jnp.sum(p, axis=1, keepdims=True)
    ps = _stack_split(p)
    pv = None
    for i, vr in enumerate(v_refs):
        v_hi, v_lo = _split_bf16(vr[0])
        pi = ps[:, i * prow:(i + 1) * prow]
        t = jnp.dot(pi, v_hi, preferred_element_type=F32) + jnp.dot(pi, v_lo, preferred_element_type=F32)
        pv = t if pv is None else pv + t
    acc_scr[...] = alpha * acc_scr[...] + _fold_split(pv)
    m_scr[...] = m_new

    @pl.when(j == nj - 1)
    def _():
        z = z_ref[0]
        k_new = jnp.concatenate([_rope(z[:, QA_W + h * HEAD_W:QA_W + (h + 1) * HEAD_W], ra, rb, rc)
                                 for h in range(H_A)], axis=1)
        v_new = z[:, 2 * QA_W:3 * QA_W]
        kout_ref[0] = k_new
        vout_ref[0] = v_new
        s_self = jnp.sum(q_scr[...] * per_row(k_new), axis=1, keepdims=True)
        m_old2 = m_scr[...]
        m_fin = jnp.maximum(m_old2, s_self)
        a2 = jnp.exp(m_old2 - m_fin)
        p_self = jnp.exp(s_self - m_fin)
        l_fin = a2 * l_scr[...] + p_self
        o_all = (a2 * acc_scr[...] + p_self * per_row(v_new)) / l_fin
        lam = _diff_lambda(lam_ref, lam_init)
        outs = []
        for h in range(H_A):
            o = o_all[2 * h:2 * h + 1] - lam * o_all[2 * h + 1:2 * h + 2]
            outs.append(o * lax.rsqrt(jnp.mean(o * o, axis=1, keepdims=True) + LN_EPS)
                        * g_ref[...] * (1.0 - lam_init))
        a_ref[0] = jnp.concatenate(outs, axis=1)


def _paged_attn(page_table, z3, tabs, lam_p, subln_g, ck, cv, ppb, lam_init):
    nb, npages = page_table.shape
    width = z3.shape[2]
    nj = npages // ppb
    pt = page_table.reshape(-1)
    prow = PAGE_SIZE * H_A
    nrow = 2 * H_A
    const = lambda a: pl.BlockSpec(a.shape, lambda b, j, p: (0,) * a.ndim)
    page = lambda i: pl.BlockSpec((1, prow, HEAD_W),
                                  lambda b, j, p: (p[b * npages + j * ppb + i], 0, 0))
    seq3 = lambda w: pl.BlockSpec((1, 1, w), lambda b, j, p: (b, 0, 0))
    grid_spec = pltpu.PrefetchScalarGridSpec(
        num_scalar_prefetch=1,
        grid=(nb, nj),
        in_specs=[seq3(width), const(tabs[0]), const(tabs[1]), const(tabs[2]), const(lam_p),
                  const(subln_g)] + [page(i) for i in range(ppb)] + [page(i) for i in range(ppb)],
        out_specs=[seq3(QA_W), seq3(QA_W), seq3(QA_W)],
        scratch_shapes=[pltpu.VMEM((nrow, HEAD_W), F32), pltpu.VMEM((2 * nrow, HEAD_W), BF16),
                        pltpu.VMEM((nrow, 1), F32), pltpu.VMEM((nrow, 1), F32),
                        pltpu.VMEM((nrow, HEAD_W), F32)],
    )
    sds = jax.ShapeDtypeStruct((nb, 1, QA_W), F32)
    return pl.pallas_call(
        functools.partial(_paged_attn_kernel, ppb=ppb, lam_init=lam_init),
        grid_spec=grid_spec,
        out_shape=[sds, sds, sds],
        compiler_params=_params("parallel", "arbitrary"),
        name="paged_attn",
    )(pt, z3, *tabs, lam_p, subln_g, *([ck] * ppb), *([cv] * ppb))


def _mlstm_step_kernel(gates_ref, m0_ref, big_ref, bfg_ref, qc_ref, kc_ref, v_ref, og_ref, c0_ref,
                       n0_ref, g_ref, min_ref, c_ref, n_ref, m_ref):
    spb = c0_ref.shape[0]
    for i in range(spb):
        _mlstm_step_one(pl.program_id(0) * spb + i, i, gates_ref, m0_ref, big_ref, bfg_ref, qc_ref, kc_ref,
                        v_ref, og_ref, c0_ref, n0_ref, g_ref, min_ref, c_ref, n_ref, m_ref)


def _mlstm_step_one(b, i, gates_ref, m0_ref, big_ref, bfg_ref, qc_ref, kc_ref, v_ref, og_ref, c0_ref,
                    n0_ref, g_ref, min_ref, c_ref, n_ref, m_ref):
    outs = []
    for h in range(H_M):
        sl = slice(h * DV_M, (h + 1) * DV_M)
        ig = _softcap(jnp.full((1, LANES), gates_ref[b, h] + big_ref[h], F32))
        lf = _log_sigmoid(_softcap(jnp.full((1, LANES), gates_ref[b, H_M + h] + bfg_ref[h], F32)))
        m0 = jnp.full((1, LANES), m0_ref[b, h], F32)
        m_inter = lf + m0
        m_t = jnp.maximum(m_inter, ig)
        w_inter = jnp.exp(m_inter - m_t)
        w_new = jnp.exp(ig - m_t)
        qc = qc_ref[i, h]
        kc = kc_ref[i, h] * (DK_M ** -0.5)
        vr = v_ref[i][:, sl]
        c0 = c0_ref[i, h]
        n0 = n0_ref[i, h]
        qk = jnp.sum(qc * kc, axis=0, keepdims=True)
        s = qk * w_new
        num = w_inter * jnp.sum(qc * c0, axis=0, keepdims=True) + s * vr
        den = w_inter * jnp.sum(qc * n0, axis=0, keepdims=True) + s
        hout = num / jnp.maximum(jnp.abs(den), jnp.exp(-m_t))
        mu = jnp.mean(hout, axis=1, keepdims=True)
        hc = hout - mu
        var = jnp.mean(hc * hc, axis=1, keepdims=True)
        outs.append(hc * lax.rsqrt(var + LN_EPS) * g_ref[:, sl] * _sigmoid(og_ref[i][:, sl]))
        c_ref[i, h] = w_inter * c0 + (w_new * kc) * vr
        n_ref[i, h] = w_inter[:, 0:1] * n0 + w_new[:, 0:1] * kc
        m_ref[i, h:h + 1, :] = m_t
    min_ref[i] = jnp.concatenate(outs, axis=1)


def _mlstm_step(gates, m0, big, bfg, qc, kc, v3, og3, c0, n0c, mh_g):
    nb = c0.shape[0]
    spb = _pick_tile(nb, 8)
    smem = pl.BlockSpec(memory_space=pltpu.SMEM)
    per = lambda a: pl.BlockSpec((spb,) + a.shape[1:], lambda b: (b,) + (0,) * (a.ndim - 1))
    const = lambda a: pl.BlockSpec(a.shape, lambda b: (0,) * a.ndim)
    return pl.pallas_call(
        _mlstm_step_kernel,
        grid=(nb // spb,),
        in_specs=[smem, smem, smem, smem, per(qc), per(kc), per(v3), per(og3), per(c0), per(n0c),
                  const(mh_g)],
        out_specs=[per(v3), per(c0), per(n0c), pl.BlockSpec((spb, H_M, LANES), lambda b: (b, 0, 0))],
        out_shape=[jax.ShapeDtypeStruct(v3.shape, F32), jax.ShapeDtypeStruct(c0.shape, F32),
                   jax.ShapeDtypeStruct(n0c.shape, F32),
                   jax.ShapeDtypeStruct((nb, H_M, LANES), F32)],
        compiler_params=_params("parallel"),
        name="mlstm_step",
    )(gates, m0, big, bfg, qc, kc, v3, og3, c0, n0c, mh_g)


def _pick_tile(n, target):
    t = min(n, target)
    while n % t:
        t //= 2
    return t


def _layer(l, depth, x_p, x_s, cache_k, cache_v, c0, n0, m0, page_table, w):
    (w_in, b_ig, b_fg, lq1, lk1, lq2, lk2, subln_g, mh_g, w_ba, w_bm, w_o, ln1_g, ln1_b,
     w_router, b_router, w_gate, b_gate, w_up, b_up, w_down, b_down, ln2_g, ln2_b) = w
    alpha = (2.0 * depth) ** 0.25
    lam_init = 0.8 - 0.6 * math.exp(-0.3 * l)
    bp, seq, d = x_p.shape
    ns = x_s.shape[0]
    t = bp * seq
    ne = w_router.shape[1]
    past = page_table.shape[1] * PAGE_SIZE

    w_a = w_in[:, :W_A]
    w_gt = w_in[:, W_A:W_A + N_GATE]
    w_r = w_in[:, W_A + N_GATE:]
    w_gc = jnp.pad(w_gt, ((0, 0), (0, LANES - N_GATE)))
    lam_p = jnp.stack([lq1, lk1, lq2, lk2])
    sub_g = subln_g.reshape(1, HEAD_W)
    mh_g2 = mh_g.reshape(1, VM_W)
    bcol = jnp.pad(jnp.concatenate([b_ig, b_fg]), (0, LANES - N_GATE)).reshape(1, LANES)
    brow = jnp.concatenate([b_ig, b_fg]).reshape(N_GATE, 1)
    g1, b1 = ln1_g.reshape(1, d), ln1_b.reshape(1, d)
    g2, b2 = ln2_g.reshape(1, d), ln2_b.reshape(1, d)
    wrt = w_router.T
    brt = b_router.reshape(ne, 1)

    x2 = x_p.reshape(t, d)
    tm = _pick_tile(seq, PROJ_TM)
    tabs_p = _rope_tables(jnp.arange(seq, dtype=jnp.int32))
    (qa, k_f, k_b, v_f, v_b, qm, km, vm, gcol, grow, og, ga, gm) = _proj_prompt(
        x2, w_a.astype(BF16), w_gc.astype(BF16), w_gt.T.astype(BF16), w_r.astype(BF16), tabs_p, seq, tm)
    a_in, (wg_b, wu_b, wd_b) = _attn_prompt(qa, k_b, v_b, lam_p, sub_g, bp, seq, lam_init,
                                            (w_gate, w_up, w_down))
    m_in, cn_p, m_p = _mlstm_prompt(qm, km, vm, gcol, grow, bcol, brow, og, mh_g2, bp, seq,
                                    _pick_tile(seq, MLSTM_CHUNK))
    sgroup = MOE_GROUP * MOE_NSUB
    t_pad = -(-(t + ns) // sgroup) * sgroup
    h_p, hb_p, comb_p, mask_p = _mix(
        a_in, m_in, ga, gm, x2, w_ba.astype(BF16), w_bm.astype(BF16), w_o.astype(BF16), g1, b1, wrt, brt,
        alpha, False, _pick_tile(seq, MIX_TM), hb_rows=t_pad)

    xs2 = x_s.reshape(ns, d)
    chunk = 7 * LANES
    width = -(-w_in.shape[1] // chunk) * chunk
    z_s = _proj_sample(xs2, jnp.pad(w_in, ((0, 0), (0, width - w_in.shape[1]))), chunk)
    tabs_s = _rope_tables(jnp.full((1,), past, jnp.int32))
    ppb = _pick_tile(page_table.shape[1], 32)
    n_pool = cache_k.shape[1]
    k_s, v_s, a_s = _paged_attn(page_table + l * n_pool, z_s.reshape(ns, 1, width), tabs_s, lam_p, sub_g,
                                cache_k.reshape(-1, PAGE_SIZE * H_A, HEAD_W),
                                cache_v.reshape(-1, PAGE_SIZE * H_A, HEAD_W), ppb, lam_init)
    o = 3 * QA_W
    qc = z_s[:, o:o + QM_W].reshape(ns, H_M, DK_M, 1)
    kc = z_s[:, o + QM_W:o + 2 * QM_W].reshape(ns, H_M, DK_M, 1)
    v3 = z_s[:, o + 2 * QM_W:W_A].reshape(ns, 1, VM_W)
    gates = z_s[:, W_A:W_A + N_GATE]
    r0 = W_A + N_GATE
    og3 = z_s[:, r0:r0 + VM_W].reshape(ns, 1, VM_W)
    ga_s = z_s[:, r0 + VM_W:r0 + VM_W + d]
    gm_s = z_s[:, r0 + VM_W + d:r0 + VM_W + 2 * d]
    m_in_s, c_s, n_s, m_s = _mlstm_step(gates, m0, b_ig, b_fg, qc, kc, v3, og3, c0,
                                        n0.reshape(ns, H_M, DK_M, 1), mh_g2)
    h_s, hb_s, comb_s, mask_s = _mix(
        a_s.reshape(ns, QA_W), m_in_s.reshape(ns, VM_W), ga_s, gm_s, xs2, w_ba, w_bm, w_o, g1, b1,
        wrt, brt, alpha, True, ns)

    padt = lambda p, s: jnp.pad(jnp.concatenate([p, s], axis=1), ((0, 0), (0, t_pad - t - ns)))
    hb_all = _append_rows(hb_p, hb_s, t)
    comb_all = padt(comb_p, comb_s)
    pos_all, cnt = _route(padt(mask_p, mask_s), MOE_GROUP)
    counts = cnt[:, :, 0].astype(jnp.int32)
    used = (jnp.sum(counts, axis=1) > 0).reshape(-1, MOE_NSUB)
    nact = jnp.max(jnp.where(used, jnp.arange(1, MOE_NSUB + 1, dtype=jnp.int32), 0), axis=1)
    ff = _moe(counts.reshape(-1), nact, hb_all, pos_all, comb_all,
              wg_b, b_gate.reshape(ne, 1, -1), wu_b, b_up.reshape(ne, 1, -1),
              wd_b, b_down.reshape(ne, 1, -1), MOE_GROUP, MOE_NSUB, MOE_ROWS)
    y_p = _ln2(h_p, ff, g2, b2, alpha, tm, 0)
    y_s = _ln2(h_s, ff, g2, b2, alpha, ns, t // ns)

    cw = cn_p[:, :, :DV_M].reshape(bp, H_M, DK_M, DV_M)
    nw = cn_p[:, :, DV_M].reshape(bp, H_M, DK_M)
    outs_p = (y_p.reshape(bp, seq, d), k_f.reshape(bp, seq, H_A, HEAD_W), v_f.reshape(bp, seq, H_A, HEAD_W),
              cw, nw, m_p[:, 0, :H_M])
    outs_s = (y_s.reshape(ns, 1, d), k_s.reshape(ns, 1, H_A, HEAD_W), v_s.reshape(ns, 1, H_A, HEAD_W),
              c_s, n_s.reshape(ns, H_M, DK_M), m_s[:, :, 0])
    return outs_p, outs_s


def kernel(x_prompt, x_sample, cache_k, cache_v, state_c, state_n, state_m, page_table, w_in, b_igate, b_fgate, lambda_q1, lambda_k1, lambda_q2, lambda_k2, subln_g, mh_norm_g, w_ba, w_bm, w_o, ln1_g, ln1_b, w_router, b_router, w_gate, b_gate, w_up, b_up, w_down, b_down, ln2_g, ln2_b):
    depth = w_in.shape[0]
    assert x_sample.shape[1] == 1, "the sample pass handles one new token per sequence"
    weights = (w_in, b_igate, b_fgate, lambda_q1, lambda_k1, lambda_q2, lambda_k2, subln_g, mh_norm_g,
               w_ba, w_bm, w_o, ln1_g, ln1_b, w_router, b_router, w_gate, b_gate, w_up, b_up,
               w_down, b_down, ln2_g, ln2_b)
    y_p, y_s = x_prompt, x_sample
    acc_p = [[] for _ in range(5)]
    acc_s = [[] for _ in range(5)]
    for l in range(depth):
        outs_p, outs_s = _layer(l, depth, y_p, y_s, cache_k, cache_v, state_c[l], state_n[l],
                                state_m[l], page_table, tuple(a[l] for a in weights))
        y_p, y_s = outs_p[0], outs_s[0]
        for i in range(5):
            acc_p[i].append(outs_p[1 + i])
            acc_s[i].append(outs_s[1 + i])
    return (y_p, y_s, *(jnp.stack(a) for a in acc_p), *(jnp.stack(a) for a in acc_s))
```

```python
import functools
import math

import jax
import jax.numpy as jnp
from jax import lax
from jax.experimental import pallas as pl
from jax.experimental.pallas import tpu as pltpu

F32 = jnp.float32
BF16 = jnp.bfloat16
HIGHEST = lax.Precision.HIGHEST

H_A = 4
D_HA = 64
ROT_DIM = D_HA // 4
ROPE_THETA = 500000.0
H_M = 4
DK_M = 64
DV_M = 128
GATE_SOFTCAP = 15.0
N_EXPERTS = 32
TOP_K = 4
SWIGLU_LIMIT = 7.0
SWIGLU_ALPHA = 1.702
LN_EPS = 1e-5
PAGE_SIZE = 128

QA_W = H_A * 2 * D_HA
QM_W = H_M * DK_M
VM_W = H_M * DV_M
HEAD_W = 2 * D_HA
N_GATE = 2 * H_M
W_A = 3 * QA_W + 2 * QM_W + VM_W

LANES = 128
VMEM_LIMIT = 56 * 1024 * 1024
SIDE_SLAB_BYTES = 2 * 1024 * 1024

ATTN_TQ = 256
ATTN_TK = 512
ATTN_HEADS_PER_STEP = 4
PROJ_TM = 512
MIX_TM = 1024
MLSTM_SEQS_PER_STEP = 2
MLSTM_CHUNK = 256
MOE_ROWS = 128
MOE_GROUP = 7 * LANES
MOE_NSUB = 2
MOE_SLAB_EXPERTS = N_EXPERTS
MOE_SCATTER_EXPERTS = 8

NT_DIMS = (((1,), (1,)), ((), ()))
TN_DIMS = (((0,), (0,)), ((), ()))


def _params(*sem):
    return pltpu.CompilerParams(dimension_semantics=sem, vmem_limit_bytes=VMEM_LIMIT)


def _softcap(x):
    return GATE_SOFTCAP * jnp.tanh(x / GATE_SOFTCAP)


def _log_sigmoid(x):
    return jnp.minimum(x, 0.0) - jnp.log1p(jnp.exp(-jnp.abs(x)))


def _sigmoid(x):
    return 1.0 / (1.0 + jnp.exp(-x))


def _split_bf16(x):
    hi = x.astype(BF16)
    lo = (x - hi.astype(F32)).astype(BF16)
    return hi, lo


def _split3_bf16(x):
    hi = x.astype(BF16)
    r = x - hi.astype(F32)
    mid = r.astype(BF16)
    lo = (r - mid.astype(F32)).astype(BF16)
    return hi, mid, lo


def _stack_split(x):
    hi, lo = _split_bf16(x)
    return jnp.concatenate([hi, lo], axis=0)


def _fold_split(t):
    n = t.shape[0] // 2
    return t[:n] + t[n:]


def _layer_norm(x, g, b):
    mu = jnp.mean(x, axis=-1, keepdims=True)
    xc = x - mu
    var = jnp.mean(xc * xc, axis=-1, keepdims=True)
    return xc * lax.rsqrt(var + LN_EPS) * g + b


def _rope(t, ra, rb, rc):
    return t * ra + pltpu.roll(t, 8, 1) * rb + pltpu.roll(t, HEAD_W - 8, 1) * rc


def _rope_tables(pos):
    inv = ROPE_THETA ** (-jnp.arange(0, ROT_DIM, 2, dtype=F32) / ROT_DIM)
    ang = pos.astype(F32)[:, None] * inv[None, :]
    cos, sin = jnp.cos(ang), jnp.sin(ang)
    n = pos.shape[0]
    half = ROT_DIM // 2
    rest = D_HA - ROT_DIM
    a = jnp.concatenate([cos, cos, jnp.ones((n, rest), F32)], axis=1)
    b = jnp.concatenate([jnp.zeros((n, half), F32), sin, jnp.zeros((n, rest), F32)], axis=1)
    c = jnp.concatenate([-sin, jnp.zeros((n, half + rest), F32)], axis=1)
    return tuple(jnp.tile(t, (1, 2)) for t in (a, b, c))


def _diff_lambda(lam_ref, lam_init):
    lp = lam_ref[...]
    s1 = jnp.sum(lp[0:1] * lp[1:2], axis=1, keepdims=True)
    s2 = jnp.sum(lp[2:3] * lp[3:4], axis=1, keepdims=True)
    return jnp.exp(s1) - jnp.exp(s2) + lam_init


def _proj_prompt_kernel(x_ref, wa_ref, wgc_ref, wgr_ref, wr_ref, ra_ref, rb_ref, rc_ref,
                        q_ref, kf_ref, kb_ref, vf_ref, vb_ref, qm_ref, km_ref, vm_ref,
                        gcol_ref, grow_ref, og_ref, ga_ref, gm_ref):
    x = x_ref[...].astype(BF16)
    tm = x.shape[0]
    za = jnp.dot(x, wa_ref[...], preferred_element_type=F32)
    ra, rb, rc = ra_ref[...], rb_ref[...], rc_ref[...]
    for h in range(H_A):
        sl = slice(h * HEAD_W, (h + 1) * HEAD_W)
        qh = _rope(za[:, sl], ra, rb, rc)
        q_ref[:, sl] = (qh * (D_HA ** -0.5)).astype(BF16)
        kh = _rope(za[:, QA_W + h * HEAD_W:QA_W + (h + 1) * HEAD_W], ra, rb, rc)
        kf_ref[pl.ds(h, tm, stride=H_A), :] = kh
        kb_ref[:, sl] = kh.astype(BF16)
        vf_ref[pl.ds(h, tm, stride=H_A), :] = za[:, 2 * QA_W + h * HEAD_W:2 * QA_W + (h + 1) * HEAD_W]
    vb_ref[...] = za[:, 2 * QA_W:3 * QA_W].astype(BF16)
    o = 3 * QA_W
    qm_ref[...] = za[:, o:o + QM_W].astype(BF16)
    km_ref[...] = (za[:, o + QM_W:o + 2 * QM_W] * (DK_M ** -0.5)).astype(BF16)
    vm_ref[...] = za[:, o + 2 * QM_W:].astype(BF16)
    gcol_ref[...] = jnp.dot(x, wgc_ref[...], preferred_element_type=F32)
    grow_ref[...] = lax.dot_general(wgr_ref[...], x, NT_DIMS, preferred_element_type=F32)
    zr = jnp.dot(x, wr_ref[...], preferred_element_type=F32)
    d = ga_ref.shape[1]
    og_ref[...] = zr[:, :VM_W].astype(BF16)
    ga_ref[...] = zr[:, VM_W:VM_W + d].astype(BF16)
    gm_ref[...] = zr[:, VM_W + d:].astype(BF16)


def _proj_prompt(x2, wa, wgc, wgr, wr, tabs, seq, tm):
    t, d = x2.shape
    nrep = seq // tm
    row = lambda w: pl.BlockSpec((tm, w), lambda i: (i, 0))
    full = lambda a: pl.BlockSpec(a.shape, lambda i: (0,) * a.ndim, pipeline_mode=pl.Buffered(1))
    tab = pl.BlockSpec((tm, HEAD_W), lambda i: (i % nrep, 0))
    sds = lambda w, dt: jax.ShapeDtypeStruct((t, w), dt)
    kv_spec = pl.BlockSpec((tm * H_A, HEAD_W), lambda i: (i, 0))
    kv_sds = jax.ShapeDtypeStruct((t * H_A, HEAD_W), F32)
    return pl.pallas_call(
        _proj_prompt_kernel,
        grid=(t // tm,),
        in_specs=[row(d), full(wa), full(wgc), full(wgr), full(wr), tab, tab, tab],
        out_specs=[row(QA_W), kv_spec, row(QA_W), kv_spec, row(QA_W),
                   row(QM_W), row(QM_W), row(VM_W), row(LANES),
                   pl.BlockSpec((N_GATE, tm), lambda i: (0, i)),
                   row(VM_W), row(d), row(d)],
        out_shape=[sds(QA_W, BF16), kv_sds, sds(QA_W, BF16), kv_sds, sds(QA_W, BF16),
                   sds(QM_W, BF16), sds(QM_W, BF16), sds(VM_W, BF16), sds(LANES, F32),
                   jax.ShapeDtypeStruct((N_GATE, t), F32),
                   sds(VM_W, BF16), sds(d, BF16), sds(d, BF16)],
        compiler_params=_params("parallel"),
        name="proj_prompt",
    )(x2, wa, wgc, wgr, wr, *tabs)


def _attn_prompt_kernel(q_ref, k_ref, v_ref, lam_ref, g_ref, *rest, tq, tk, hps, lam_init):
    nside = (len(rest) - 1) // 2
    o_ref = rest[nside]
    for src, dst in zip(rest[:nside], rest[nside + 1:]):
        dst[...] = src[...].astype(dst.dtype)
    qi = pl.program_id(2)
    nfull = (qi * tq) // tk
    lane = lax.broadcasted_iota(jnp.int32, (1, HEAD_W), 1)
    r = lax.broadcasted_iota(jnp.int32, (tq, tk), 0) + qi * tq
    c = lax.broadcasted_iota(jnp.int32, (tq, tk), 1) + nfull * tk
    causal = jnp.concatenate([c <= r, c <= r], axis=0)
    ones_blk = jnp.broadcast_to(jnp.where(lane == 0, 1.0, 0.0).astype(BF16), (tk, HEAD_W))
    qqs = []
    for hh in range(hps):
        q = q_ref[:, hh * HEAD_W:(hh + 1) * HEAD_W]
        zero = jnp.zeros_like(q)
        qqs.append(jnp.concatenate([jnp.where(lane < D_HA, q, zero), jnp.where(lane >= D_HA, q, zero)],
                                   axis=0))

    def step(j, carry, masked):
        off = pl.multiple_of(j * tk, tk)
        out = []
        for hh in range(hps):
            m, acc = carry[hh]
            sl = slice(hh * HEAD_W, (hh + 1) * HEAD_W)
            k = k_ref[pl.ds(off, tk), sl]
            vext = jnp.concatenate([v_ref[pl.ds(off, tk), sl], ones_blk], axis=1)
            s = lax.dot_general(qqs[hh], k, NT_DIMS, preferred_element_type=F32)
            if masked:
                s = jnp.where(causal, s, -jnp.inf)
            m_new = jnp.maximum(m, jnp.max(s, axis=1, keepdims=True))
            p = jnp.exp(s - m_new)
            acc = jnp.exp(m - m_new) * acc + jnp.dot(p.astype(BF16), vext, preferred_element_type=F32)
            out.append((m_new, acc))
        return tuple(out)

    init = tuple((jnp.full((2 * tq, 1), -jnp.inf, F32), jnp.zeros((2 * tq, 2 * HEAD_W), F32))
                 for _ in range(hps))
    carry = lax.fori_loop(0, nfull, lambda j, cr: step(j, cr, False), init)
    carry = step(nfull, carry, True)
    lam = _diff_lambda(lam_ref, lam_init)
    for hh in range(hps):
        acc = carry[hh][1]
        o1 = acc[:tq, :HEAD_W] / acc[:tq, HEAD_W:HEAD_W + 1]
        o2 = acc[tq:, :HEAD_W] / acc[tq:, HEAD_W:HEAD_W + 1]
        o = o1 - lam * o2
        y = o * lax.rsqrt(jnp.mean(o * o, axis=1, keepdims=True) + LN_EPS) * g_ref[...] * (1.0 - lam_init)
        o_ref[:, hh * HEAD_W:(hh + 1) * HEAD_W] = y.astype(o_ref.dtype)


def _attn_prompt(q, k, v, lam_p, subln_g, batch, seq, lam_init, side):
    t = q.shape[0]
    tk = _pick_tile(seq, ATTN_TK)
    tq = _pick_tile(tk, ATTN_TQ)
    nq = seq // tq
    hps = ATTN_HEADS_PER_STEP
    nh = H_A // hps
    w = hps * HEAD_W
    steps = batch * nh * nq
    side2 = [a.reshape(-1, a.shape[-1]) for a in side]
    slab = [a.shape[0] // steps for a in side2]
    if any(a.shape[0] % steps or s % 16 or s * a.shape[1] * 4 > SIDE_SLAB_BYTES for a, s in zip(side2, slab)):
        side2, slab = [], []
    side_specs = [pl.BlockSpec((s, a.shape[1]), lambda b, h, i: ((b * nh + h) * nq + i, 0))
                  for a, s in zip(side2, slab)]
    outs = pl.pallas_call(
        functools.partial(_attn_prompt_kernel, tq=tq, tk=tk, hps=hps, lam_init=lam_init),
        grid=(batch, nh, nq),
        in_specs=[pl.BlockSpec((tq, w), lambda b, h, i: (b * nq + i, h)),
                  pl.BlockSpec((seq, w), lambda b, h, i: (b, h)),
                  pl.BlockSpec((seq, w), lambda b, h, i: (b, h)),
                  pl.BlockSpec(lam_p.shape, lambda b, h, i: (0, 0)),
                  pl.BlockSpec((1, HEAD_W), lambda b, h, i: (0, 0))] + side_specs,
        out_specs=[pl.BlockSpec((tq, w), lambda b, h, i: (b * nq + i, h))] + side_specs,
        out_shape=[jax.ShapeDtypeStruct((t, QA_W), BF16)]
        + [jax.ShapeDtypeStruct(a.shape, BF16) for a in side2],
        compiler_params=_params("parallel", "parallel", "parallel"),
        name="attn_prompt",
    )(q, k, v, lam_p, subln_g, *side2)
    if side2:
        return outs[0], [o.reshape(a.shape) for o, a in zip(outs[1:], side)]
    return outs[0], [a.astype(BF16) for a in side]


def _mlstm_prompt_kernel(q_ref, k_ref, v_ref, gcol_ref, grow_ref, bcol_ref, brow_ref, og_ref, g_ref,
                         min_ref, cn_ref, m_ref, *, lc, bps):
    ci = pl.program_id(1)

    @pl.when(ci == 0)
    def _():
        cn_ref[...] = jnp.zeros_like(cn_ref)
        m_ref[...] = jnp.zeros_like(m_ref)

    for i in range(bps):
        _mlstm_chunk(q_ref.at[0, i], k_ref.at[0, i], v_ref.at[0, i], gcol_ref.at[0, i], grow_ref[:, 0, i, :],
                     bcol_ref, brow_ref, og_ref.at[0, i], g_ref, min_ref.at[0, i], cn_ref.at[0, i],
                     m_ref.at[0, i], lc)


def _mlstm_chunk(q_ref, k_ref, v_ref, gcol_ref, grow, bcol_ref, brow_ref, og_ref, g_ref,
                 min_ref, cn_ref, m_ref, lc):
    gc = _softcap(gcol_ref[...] + bcol_ref[...])
    lfc = _log_sigmoid(gc)
    gr = _softcap(grow + brow_ref[...])
    lfr = _log_sigmoid(gr)
    ri = lax.broadcasted_iota(jnp.int32, (lc, lc), 0)
    cj = lax.broadcasted_iota(jnp.int32, (lc, lc), 1)
    tril = cj <= ri
    pc = _split3_bf16(lfc)
    t = jnp.dot(jnp.where(tril, 1.0, 0.0).astype(BF16), jnp.concatenate(pc, axis=1),
                preferred_element_type=F32)
    b_col = t[:, :LANES] + t[:, LANES:2 * LANES] + t[:, 2 * LANES:]
    pr = _split3_bf16(lfr)
    t = jnp.dot(jnp.concatenate(pr + (jnp.zeros_like(pr[0]),), axis=0),
                jnp.where(ri <= cj, 1.0, 0.0).astype(BF16), preferred_element_type=F32)
    b_row = t[:N_GATE] + t[N_GATE:2 * N_GATE] + t[2 * N_GATE:3 * N_GATE]

    cn = cn_ref[...]
    cn_bf = cn.astype(BF16)
    q = q_ref[...]
    k = k_ref[...]
    v = v_ref[...]
    m_all = m_ref[...]
    lane_q = lax.broadcasted_iota(jnp.int32, (1, QM_W), 1) // DK_M
    row_q = lax.broadcasted_iota(jnp.int32, (QM_W, 1), 0) // DK_M
    lane_v = lax.broadcasted_iota(jnp.int32, (1, LANES), 1)
    ones_blk = jnp.broadcast_to(jnp.where(lane_v == 0, 1.0, 0.0).astype(BF16), (lc, LANES))
    new_cn = jnp.zeros_like(cn)
    new_m = m_all
    for h in range(H_M):
        sl = slice(h * DV_M, (h + 1) * DV_M)
        qh = jnp.where(lane_q == h, q, jnp.zeros_like(q))
        kh = jnp.where(lane_q == h, k, jnp.zeros_like(k))
        b_c = b_col[:, H_M + h:H_M + h + 1]
        ig_c = gc[:, h:h + 1]
        b_r = b_row[H_M + h:H_M + h + 1, :]
        ig_r = gr[h:h + 1, :]
        m_prev = m_all[:, h:h + 1]
        dmat = jnp.where(tril, b_c - b_r + ig_r, -jnp.inf)
        m_inter = b_c + m_prev
        m_t = jnp.maximum(m_inter, jnp.max(dmat, axis=1, keepdims=True))
        w_inter = jnp.exp(m_inter - m_t)
        s = lax.dot_general(qh, kh, NT_DIMS, preferred_element_type=F32) * jnp.exp(dmat - m_t)
        vext = jnp.concatenate([v[:, sl], ones_blk], axis=1)
        nd = (w_inter * jnp.dot(qh, cn_bf, preferred_element_type=F32)
              + jnp.dot(s.astype(BF16), vext, preferred_element_type=F32))
        num = nd[:, :DV_M]
        den = nd[:, DV_M:DV_M + 1]
        hout = num / jnp.maximum(jnp.abs(den), jnp.exp(-m_t))
        mu = jnp.mean(hout, axis=1, keepdims=True)
        hc = hout - mu
        var = jnp.mean(hc * hc, axis=1, keepdims=True)
        y = hc * lax.rsqrt(var + LN_EPS) * g_ref[:, sl] * _sigmoid(og_ref[:, sl].astype(F32))
        min_ref[:, sl] = y.astype(min_ref.dtype)
        m_new = m_t[lc - 1:lc, :]
        b_last = b_c[lc - 1:lc, :]
        decay = jnp.exp(b_last + m_prev - m_new)
        wk = jnp.exp(b_last - b_c + ig_c - m_new)
        upd = lax.dot_general(kh, (wk * vext.astype(F32)).astype(BF16), TN_DIMS,
                              preferred_element_type=F32)
        new_cn = new_cn + jnp.where(row_q == h, decay * cn, 0.0) + upd
        new_m = jnp.where(lane_v == h, m_new, new_m)
    cn_ref[...] = new_cn
    m_ref[...] = new_m


def _mlstm_prompt(qm, km, vm, gcol, grow, bcol, brow, og, mh_g, batch, seq, lc):
    t = qm.shape[0]
    bps = MLSTM_SEQS_PER_STEP if batch % MLSTM_SEQS_PER_STEP == 0 else 1
    nb = batch // bps
    row = lambda w: pl.BlockSpec((1, bps, lc, w), lambda b, c: (b, 0, c, 0))
    const = lambda a: pl.BlockSpec(a.shape, lambda b, c: (0,) * a.ndim)
    seqs = lambda a: a.reshape(nb, bps, seq, a.shape[1])
    cw = 2 * LANES
    m_in, cn, m = pl.pallas_call(
        functools.partial(_mlstm_prompt_kernel, lc=lc, bps=bps),
        grid=(nb, seq // lc),
        in_specs=[row(QM_W), row(QM_W), row(VM_W), row(LANES),
                  pl.BlockSpec((N_GATE, 1, bps, lc), lambda b, c: (0, b, 0, c)),
                  const(bcol), const(brow), row(VM_W), const(mh_g)],
        out_specs=[row(VM_W),
                   pl.BlockSpec((1, bps, QM_W, cw), lambda b, c: (b, 0, 0, 0)),
                   pl.BlockSpec((1, bps, 1, LANES), lambda b, c: (b, 0, 0, 0))],
        out_shape=[jax.ShapeDtypeStruct((nb, bps, seq, VM_W), BF16),
                   jax.ShapeDtypeStruct((nb, bps, QM_W, cw), F32),
                   jax.ShapeDtypeStruct((nb, bps, 1, LANES), F32)],
        compiler_params=_params("parallel", "arbitrary"),
        name="mlstm_prompt",
    )(seqs(qm), seqs(km), seqs(vm), seqs(gcol), grow.reshape(N_GATE, nb, bps, seq), bcol, brow,
      seqs(og), mh_g)
    return m_in.reshape(t, VM_W), cn.reshape(batch, QM_W, cw), m.reshape(batch, 1, LANES)


def _mix_kernel(a_ref, mi_ref, ga_ref, gm_ref, x_ref, wba_ref, wbm_ref, wo_ref, g1_ref, b1_ref,
                wrt_ref, brt_ref, h_ref, hb_ref, comb_ref, mask_ref, *, alpha, precise):
    def mm(a, w_ref):
        if precise:
            return jnp.dot(a.astype(F32), w_ref[...], precision=HIGHEST, preferred_element_type=F32)
        return jnp.dot(a.astype(BF16), w_ref[...], preferred_element_type=F32)

    a_br = mm(a_ref[...], wba_ref)
    m_br = mm(mi_ref[...], wbm_ref)
    merged = _sigmoid(ga_ref[...].astype(F32)) * a_br + _sigmoid(gm_ref[...].astype(F32)) * m_br
    mix = mm(merged, wo_ref)
    h = _layer_norm(alpha * x_ref[...] + mix, g1_ref[...], b1_ref[...])
    h_ref[...] = h
    hb_ref[...] = h.astype(BF16)

    ne = wrt_ref.shape[0]
    if precise:
        logits = lax.dot_general(wrt_ref[...], h, NT_DIMS, precision=HIGHEST, preferred_element_type=F32)
    else:
        ws = _stack_split(wrt_ref[...])
        h_hi, h_lo = _split_bf16(h)
        logits = (_fold_split(lax.dot_general(ws, h_hi, NT_DIMS, preferred_element_type=F32))
                  + lax.dot_general(ws[:ne], h_lo, NT_DIMS, preferred_element_type=F32))
    logits = logits + brt_ref[...]
    tm = logits.shape[1]
    eidx = lax.broadcasted_iota(jnp.int32, (ne, tm), 0)
    work = logits
    sel = jnp.zeros((ne, tm), jnp.bool_)
    top = None
    for _ in range(TOP_K):
        mx = jnp.max(work, axis=0, keepdims=True)
        top = mx if top is None else top
        first = jnp.min(jnp.where(work == mx, eidx, ne), axis=0, keepdims=True)
        pick = eidx == first
        sel = jnp.logical_or(sel, pick)
        work = jnp.where(pick, -jnp.inf, work)
    ex = jnp.where(sel, jnp.exp(logits - top), 0.0)
    comb_ref[...] = ex / jnp.sum(ex, axis=0, keepdims=True)
    mask_ref[...] = jnp.where(sel, 1.0, 0.0)


def _mix(a_in, m_in, ga, gm, x2, wba, wbm, wo, g1, b1, wrt, brt, alpha, precise, tm, hb_rows=None):
    t, d = x2.shape
    hb_rows = t if hb_rows is None else hb_rows
    ne = wrt.shape[0]
    row = lambda w: pl.BlockSpec((tm, w), lambda i: (i, 0))
    col = pl.BlockSpec((ne, tm), lambda i: (0, i))
    const = lambda a: pl.BlockSpec(a.shape, lambda i: (0,) * a.ndim)
    return pl.pallas_call(
        functools.partial(_mix_kernel, alpha=alpha, precise=precise),
        grid=(t // tm,),
        in_specs=[row(QA_W), row(VM_W), row(d), row(d), row(d), const(wba), const(wbm), const(wo),
                  const(g1), const(b1), const(wrt), const(brt)],
        out_specs=[row(d), row(d), col, col],
        out_shape=[jax.ShapeDtypeStruct((t, d), F32), jax.ShapeDtypeStruct((hb_rows, d), BF16),
                   jax.ShapeDtypeStruct((ne, t), F32), jax.ShapeDtypeStruct((ne, t), F32)],
        compiler_params=_params("parallel"),
        name="mix_sample" if precise else "mix_prompt",
    )(a_in, m_in, ga, gm, x2, wba, wbm, wo, g1, b1, wrt, brt)


def _append_rows_kernel(big_ref, tail_ref, o_ref):
    del big_ref
    o_ref[...] = jnp.zeros_like(o_ref)

    @pl.when(pl.program_id(0) == 0)
    def _():
        o_ref[:tail_ref.shape[0], :] = tail_ref[...]


def _append_rows(big, tail, start):
    rows, d = big.shape
    blk = math.gcd(start, rows - start)
    assert blk % 16 == 0 and blk >= tail.shape[0]
    return pl.pallas_call(
        _append_rows_kernel,
        grid=((rows - start) // blk,),
        in_specs=[pl.BlockSpec(memory_space=pl.ANY), pl.BlockSpec(tail.shape, lambda i: (0, 0))],
        out_specs=pl.BlockSpec((blk, d), lambda i: (start // blk + i, 0)),
        out_shape=jax.ShapeDtypeStruct(big.shape, big.dtype),
        input_output_aliases={0: 0},
        compiler_params=_params("arbitrary"),
        name="append_rows",
    )(big, tail)


def _route_kernel(mask_ref, pos_ref, cnt_ref):
    maskf = mask_ref[...]
    g = maskf.shape[1]
    r = lax.broadcasted_iota(jnp.int32, (g, g), 0)
    c = lax.broadcasted_iota(jnp.int32, (g, g), 1)
    before = jnp.dot(maskf.astype(BF16), (r < c).astype(BF16), preferred_element_type=F32)
    pos_ref[...] = jnp.where(maskf > 0.0, before, -1.0)
    cnt_ref[0] = jnp.broadcast_to(jnp.sum(maskf, axis=1, keepdims=True), cnt_ref.shape[1:])


def _route(mask, group):
    ne, t = mask.shape
    ng = t // group
    return pl.pallas_call(
        _route_kernel,
        grid=(ng,),
        in_specs=[pl.BlockSpec((ne, group), lambda g: (0, g))],
        out_specs=[pl.BlockSpec((ne, group), lambda g: (0, g)),
                   pl.BlockSpec((1, ne, LANES), lambda g: (g, 0, 0))],
        out_shape=[jax.ShapeDtypeStruct((ne, t), F32), jax.ShapeDtypeStruct((ng, ne, LANES), F32)],
        compiler_params=_params("parallel"),
        name="route",
    )(mask)


def _moe_kernel(cnt_ref, nact_ref, h_ref, pos_ref, comb_ref, wg_ref, bg_ref, wu_ref, bu_ref,
                wd_ref, bd_ref, o_ref, y_scr, *, rows, group, nsub):
    sg = pl.program_id(0)
    e = pl.program_id(1)
    ne = pl.num_programs(1)
    n_exp = pos_ref.shape[0]
    slab = y_scr.shape[1] // rows

    @pl.when(e == 0)
    def _():
        o_ref[...] = jnp.zeros_like(o_ref)

    slot0 = lax.broadcasted_iota(jnp.int32, (rows, 1), 0).astype(F32)

    def gather(hit, tok):
        return jnp.dot(jnp.where(hit, 1.0, 0.0).astype(BF16), h_ref[tok, :],
                       preferred_element_type=F32).astype(BF16)

    def ffn(hit, tok):
        return ffn_rows(gather(hit, tok))

    def ffn_rows(xg):
        gate = jnp.minimum(jnp.dot(xg, wg_ref[0], preferred_element_type=F32) + bg_ref[0], SWIGLU_LIMIT)
        up = jnp.clip(jnp.dot(xg, wu_ref[0], preferred_element_type=F32) + bu_ref[0],
                      -SWIGLU_LIMIT, SWIGLU_LIMIT)
        hid = gate * _sigmoid(SWIGLU_ALPHA * gate) * (up + 1.0)
        return (jnp.dot(hid.astype(BF16), wd_ref[0], preferred_element_type=F32) + bd_ref[0]).astype(BF16)

    yrows = pl.ds(pl.multiple_of(lax.rem(e, slab) * rows, rows), rows)

    def first_tile(sub):
        tok = slice(sub * group, (sub + 1) * group)
        y_scr[sub, yrows, :] = ffn(pos_ref[pl.ds(e, 1), tok] == slot0, tok)

    nact = nact_ref[sg]

    @pl.when(nact == nsub)
    def _():
        xg = jnp.concatenate(
            [gather(pos_ref[pl.ds(e, 1), sub * group:(sub + 1) * group] == slot0,
                    slice(sub * group, (sub + 1) * group)) for sub in range(nsub)], axis=0)
        y = ffn_rows(xg)
        for sub in range(nsub):
            y_scr[sub, yrows, :] = y[sub * rows:(sub + 1) * rows]

    @pl.when(nact < nsub)
    def _():
        for sub in range(nsub):
            pl.when(sub < nact)(functools.partial(first_tile, sub))

            @pl.when(sub >= nact)
            def _(sub=sub):
                y_scr[sub, yrows, :] = jnp.zeros((rows, y_scr.shape[2]), y_scr.dtype)

    for sub in range(nsub):
        tok = slice(sub * group, (sub + 1) * group)
        n = cnt_ref[(sg * nsub + sub) * ne + e]
        pos = pos_ref[pl.ds(e, 1), tok]
        cw = comb_ref[pl.ds(e, 1), tok]

        def tile(s, carry, tok=tok, pos=pos, cw=cw):
            hit = pos == slot0 + (s * rows).astype(F32)
            gw = jnp.where(hit, cw, 0.0).astype(BF16)
            o_ref[tok, :] += lax.dot_general(gw, ffn(hit, tok), TN_DIMS, preferred_element_type=F32)
            return carry

        lax.fori_loop(1, (n + rows - 1) // rows, tile, 0)

    def scatter_slab(e0):
        for sub in range(nsub):
            tok = slice(sub * group, (sub + 1) * group)
            for c0 in range(0, slab, MOE_SCATTER_EXPERTS):
                gw = jnp.concatenate(
                    [jnp.where(pos_ref[ee:ee + 1, tok] == slot0, comb_ref[ee:ee + 1, tok], 0.0).astype(BF16)
                     for ee in range(e0 + c0, e0 + c0 + MOE_SCATTER_EXPERTS)], axis=0)
                ys = y_scr[sub, c0 * rows:(c0 + MOE_SCATTER_EXPERTS) * rows, :]
                o_ref[tok, :] += lax.dot_general(gw, ys, TN_DIMS, preferred_element_type=F32)

    for e0 in range(0, n_exp, slab):
        pl.when(e == e0 + slab - 1)(functools.partial(scatter_slab, e0))


def _moe(counts, nact, hb, pos, comb, wg, bg, wu, bu, wd, bd, group, nsub, rows):
    t, d = hb.shape
    ne = wg.shape[0]
    sgroup = group * nsub
    tok = pl.BlockSpec((ne, sgroup), lambda g, e, c, a: (0, g))
    wspec = lambda w: pl.BlockSpec((1,) + w.shape[1:], lambda g, e, c, a: (e, 0, 0))
    grid_spec = pltpu.PrefetchScalarGridSpec(
        num_scalar_prefetch=2,
        grid=(t // sgroup, ne),
        in_specs=[pl.BlockSpec((sgroup, d), lambda g, e, c, a: (g, 0)),
                  tok, tok, wspec(wg), wspec(bg), wspec(wu), wspec(bu), wspec(wd), wspec(bd)],
        out_specs=pl.BlockSpec((sgroup, d), lambda g, e, c, a: (g, 0)),
        scratch_shapes=[pltpu.VMEM((nsub, MOE_SLAB_EXPERTS * rows, d), BF16)],
    )
    return pl.pallas_call(
        functools.partial(_moe_kernel, rows=rows, group=group, nsub=nsub),
        grid_spec=grid_spec,
        out_shape=jax.ShapeDtypeStruct((t, d), F32),
        compiler_params=_params("parallel", "arbitrary"),
        name="moe",
    )(counts, nact, hb, pos, comb, wg, bg, wu, bu, wd, bd)


def _ln2_kernel(h_ref, ff_ref, g_ref, b_ref, y_ref, *, alpha):
    y_ref[...] = _layer_norm(alpha * h_ref[...] + ff_ref[...], g_ref[...], b_ref[...])


def _ln2(h, ff, g2, b2, alpha, tm, ff_block0):
    t, d = h.shape
    const = lambda a: pl.BlockSpec(a.shape, lambda i: (0,) * a.ndim)
    return pl.pallas_call(
        functools.partial(_ln2_kernel, alpha=alpha),
        grid=(t // tm,),
        in_specs=[pl.BlockSpec((tm, d), lambda i: (i, 0)),
                  pl.BlockSpec((tm, d), lambda i: (ff_block0 + i, 0)), const(g2), const(b2)],
        out_specs=pl.BlockSpec((tm, d), lambda i: (i, 0)),
        out_shape=jax.ShapeDtypeStruct((t, d), F32),
        compiler_params=_params("parallel"),
        name="ln2",
    )(h, ff, g2, b2)


def _proj_sample_kernel(x_ref, w_ref, z_ref):
    z_ref[...] = jnp.dot(x_ref[...], w_ref[...], precision=HIGHEST, preferred_element_type=F32)


def _proj_sample(xs, w, chunk):
    n, d = xs.shape
    width = w.shape[1]
    return pl.pallas_call(
        _proj_sample_kernel,
        grid=(width // chunk,),
        in_specs=[pl.BlockSpec((n, d), lambda j: (0, 0)), pl.BlockSpec((d, chunk), lambda j: (0, j))],
        out_specs=pl.BlockSpec((n, chunk), lambda j: (0, j)),
        out_shape=jax.ShapeDtypeStruct((n, width), F32),
        compiler_params=_params("parallel"),
        name="proj_sample",
    )(xs, w)


def _paged_attn_kernel(pt_ref, z_ref, ra_ref, rb_ref, rc_ref, lam_ref, g_ref, *rest,
                       ppb, lam_init):
    k_refs = rest[:ppb]
    v_refs = rest[ppb:2 * ppb]
    kout_ref, vout_ref, a_ref, q_scr, qs_scr, m_scr, l_scr, acc_scr = rest[2 * ppb:]
    j = pl.program_id(1)
    nj = pl.num_programs(1)
    nrow = 2 * H_A
    prow = PAGE_SIZE * H_A
    ra, rb, rc = ra_ref[...], rb_ref[...], rc_ref[...]
    lane = lax.broadcasted_iota(jnp.int32, (1, HEAD_W), 1)

    def per_row(t):
        return jnp.concatenate([t[:, (r // 2) * HEAD_W:(r // 2 + 1) * HEAD_W] for r in range(nrow)], axis=0)

    @pl.when(j == 0)
    def _():
        z = z_ref[0]
        rows = []
        for h in range(H_A):
            qh = _rope(z[:, h * HEAD_W:(h + 1) * HEAD_W], ra, rb, rc) * (D_HA ** -0.5)
            rows += [jnp.where(lane < D_HA, qh, 0.0), jnp.where(lane >= D_HA, qh, 0.0)]
        q = jnp.concatenate(rows, axis=0)
        q_scr[...] = q
        qs_scr[...] = _stack_split(q)
        m_scr[...] = jnp.full_like(m_scr, -jnp.inf)
        l_scr[...] = jnp.zeros_like(l_scr)
        acc_scr[...] = jnp.zeros_like(acc_scr)

    qs = qs_scr[...]
    parts = []
    for kr in k_refs:
        k_hi, k_lo = _split_bf16(kr[0])
        t = (lax.dot_general(qs, k_hi, NT_DIMS, preferred_element_type=F32)
             + lax.dot_general(qs, k_lo, NT_DIMS, preferred_element_type=F32))
        parts.append(_fold_split(t))
    s = jnp.concatenate(parts, axis=1)
    r_head = lax.broadcasted_iota(jnp.int32, s.shape, 0) // 2
    c_head = lax.broadcasted_iota(jnp.int32, s.shape, 1) % H_A
    s = jnp.where(r_head == c_head, s, -jnp.inf)
    m_old = m_scr[...]
    m_new = jnp.maximum(m_old, jnp.max(s, axis=1, keepdims=True))
    alpha = jnp.exp(m_old - m_new)
    p = jnp.exp(s - m_new)
    l_scr[...] = alpha * l_scr[...] + jnp.sum(p, axis=1, keepdims=True)
    ps = _stack_split(p)
    pv = None
    for i, vr in enumerate(v_refs):
        v_hi, v_lo = _split_bf16(vr[0])
        pi = ps[:, i * prow:(i + 1) * prow]
        t = jnp.dot(pi, v_hi, preferred_element_type=F32) + jnp.dot(pi, v_lo, preferred_element_type=F32)
        pv = t if pv is None else pv + t
    acc_scr[...] = alpha * acc_scr[...] + _fold_split(pv)
    m_scr[...] = m_new

    @pl.when(j == nj - 1)
    def _():
        z = z_ref[0]
        k_new = jnp.concatenate([_rope(z[:, QA_W + h * HEAD_W:QA_W + (h + 1) * HEAD_W], ra, rb, rc)
                                 for h in range(H_A)], axis=1)
        v_new = z[:, 2 * QA_W:3 * QA_W]
        kout_ref[0] = k_new
        vout_ref[0] = v_new
        s_self = jnp.sum(q_scr[...] * per_row(k_new), axis=1, keepdims=True)
        m_old2 = m_scr[...]
        m_fin = jnp.maximum(m_old2, s_self)
        a2 = jnp.exp(m_old2 - m_fin)
        p_self = jnp.exp(s_self - m_fin)
        l_fin = a2 * l_scr[...] + p_self
        o_all = (a2 * acc_scr[...] + p_self * per_row(v_new)) / l_fin
        lam = _diff_lambda(lam_ref, lam_init)
        outs = []
        for h in range(H_A):
            o = o_all[2 * h:2 * h + 1] - lam * o_all[2 * h + 1:2 * h + 2]
            outs.append(o * lax.rsqrt(jnp.mean(o * o, axis=1, keepdims=True) + LN_EPS)
                        * g_ref[...] * (1.0 - lam_init))
        a_ref[0] = jnp.concatenate(outs, axis=1)


def _paged_attn(page_table, z3, tabs, lam_p, subln_g, ck, cv, ppb, lam_init):
    nb, npages = page_table.shape
    width = z3.shape[2]
    nj = npages // ppb
    pt = page_table.reshape(-1)
    prow = PAGE_SIZE * H_A
    nrow = 2 * H_A
    const = lambda a: pl.BlockSpec(a.shape, lambda b, j, p: (0,) * a.ndim)
    page = lambda i: pl.BlockSpec((1, prow, HEAD_W),
                                  lambda b, j, p: (p[b * npages + j * ppb + i], 0, 0))
    seq3 = lambda w: pl.BlockSpec((1, 1, w), lambda b, j, p: (b, 0, 0))
    grid_spec = pltpu.PrefetchScalarGridSpec(
        num_scalar_prefetch=1,
        grid=(nb, nj),
        in_specs=[seq3(width), const(tabs[0]), const(tabs[1]), const(tabs[2]), const(lam_p),
                  const(subln_g)] + [page(i) for i in range(ppb)] + [page(i) for i in range(ppb)],
        out_specs=[seq3(QA_W), seq3(QA_W), seq3(QA_W)],
        scratch_shapes=[pltpu.VMEM((nrow, HEAD_W), F32), pltpu.VMEM((2 * nrow, HEAD_W), BF16),
                        pltpu.VMEM((nrow, 1), F32), pltpu.VMEM((nrow, 1), F32),
                        pltpu.VMEM((nrow, HEAD_W), F32)],
    )
    sds = jax.ShapeDtypeStruct((nb, 1, QA_W), F32)
    return pl.pallas_call(
        functools.partial(_paged_attn_kernel, ppb=ppb, lam_init=lam_init),
        grid_spec=grid_spec,
        out_shape=[sds, sds, sds],
        compiler_params=_params("parallel", "arbitrary"),
        name="paged_attn",
    )(pt, z3, *tabs, lam_p, subln_g, *([ck] * ppb), *([cv] * ppb))


def _mlstm_step_kernel(gates_ref, m0_ref, big_ref, bfg_ref, qc_ref, kc_ref, v_ref, og_ref, c0_ref,
                       n0_ref, g_ref, min_ref, c_ref, n_ref, m_ref):
    spb = c0_ref.shape[0]
    for i in range(spb):
        _mlstm_step_one(pl.program_id(0) * spb + i, i, gates_ref, m0_ref, big_ref, bfg_ref, qc_ref, kc_ref,
                        v_ref, og_ref, c0_ref, n0_ref, g_ref, min_ref, c_ref, n_ref, m_ref)


def _mlstm_step_one(b, i, gates_ref, m0_ref, big_ref, bfg_ref, qc_ref, kc_ref, v_ref, og_ref, c0_ref,
                    n0_ref, g_ref, min_ref, c_ref, n_ref, m_ref):
    outs = []
    for h in range(H_M):
        sl = slice(h * DV_M, (h + 1) * DV_M)
        ig = _softcap(jnp.full((1, LANES), gates_ref[b, h] + big_ref[h], F32))
        lf = _log_sigmoid(_softcap(jnp.full((1, LANES), gates_ref[b, H_M + h] + bfg_ref[h], F32)))
        m0 = jnp.full((1, LANES), m0_ref[b, h], F32)
        m_inter = lf + m0
        m_t = jnp.maximum(m_inter, ig)
        w_inter = jnp.exp(m_inter - m_t)
        w_new = jnp.exp(ig - m_t)
        qc = qc_ref[i, h]
        kc = kc_ref[i, h] * (DK_M ** -0.5)
        vr = v_ref[i][:, sl]
        c0 = c0_ref[i, h]
        n0 = n0_ref[i, h]
        qk = jnp.sum(qc * kc, axis=0, keepdims=True)
        s = qk * w_new
        num = w_inter * jnp.sum(qc * c0, axis=0, keepdims=True) + s * vr
        den = w_inter * jnp.sum(qc * n0, axis=0, keepdims=True) + s
        hout = num / jnp.maximum(jnp.abs(den), jnp.exp(-m_t))
        mu = jnp.mean(hout, axis=1, keepdims=True)
        hc = hout - mu
        var = jnp.mean(hc * hc, axis=1, keepdims=True)
        outs.append(hc * lax.rsqrt(var + LN_EPS) * g_ref[:, sl] * _sigmoid(og_ref[i][:, sl]))
        c_ref[i, h] = w_inter * c0 + (w_new * kc) * vr
        n_ref[i, h] = w_inter[:, 0:1] * n0 + w_new[:, 0:1] * kc
        m_ref[i, h:h + 1, :] = m_t
    min_ref[i] = jnp.concatenate(outs, axis=1)


def _mlstm_step(gates, m0, big, bfg, qc, kc, v3, og3, c0, n0c, mh_g):
    nb = c0.shape[0]
    spb = _pick_tile(nb, 8)
    smem = pl.BlockSpec(memory_space=pltpu.SMEM)
    per = lambda a: pl.BlockSpec((spb,) + a.shape[1:], lambda b: (b,) + (0,) * (a.ndim - 1))
    const = lambda a: pl.BlockSpec(a.shape, lambda b: (0,) * a.ndim)
    return pl.pallas_call(
        _mlstm_step_kernel,
        grid=(nb // spb,),
        in_specs=[smem, smem, smem, smem, per(qc), per(kc), per(v3), per(og3), per(c0), per(n0c),
                  const(mh_g)],
        out_specs=[per(v3), per(c0), per(n0c), pl.BlockSpec((spb, H_M, LANES), lambda b: (b, 0, 0))],
        out_shape=[jax.ShapeDtypeStruct(v3.shape, F32), jax.ShapeDtypeStruct(c0.shape, F32),
                   jax.ShapeDtypeStruct(n0c.shape, F32),
                   jax.ShapeDtypeStruct((nb, H_M, LANES), F32)],
        compiler_params=_params("parallel"),
        name="mlstm_step",
    )(gates, m0, big, bfg, qc, kc, v3, og3, c0, n0c, mh_g)


def _pick_tile(n, target):
    t = min(n, target)
    while n % t:
        t //= 2
    return t


def _layer(l, depth, x_p, x_s, cache_k, cache_v, c0, n0, m0, page_table, w):
    (w_in, b_ig, b_fg, lq1, lk1, lq2, lk2, subln_g, mh_g, w_ba, w_bm, w_o, ln1_g, ln1_b,
     w_router, b_router, w_gate, b_gate, w_up, b_up, w_down, b_down, ln2_g, ln2_b) = w
    alpha = (2.0 * depth) ** 0.25
    lam_init = 0.8 - 0.6 * math.exp(-0.3 * l)
    bp, seq, d = x_p.shape
    ns = x_s.shape[0]
    t = bp * seq
    ne = w_router.shape[1]
    past = page_table.shape[1] * PAGE_SIZE

    w_a = w_in[:, :W_A]
    w_gt = w_in[:, W_A:W_A + N_GATE]
    w_r = w_in[:, W_A + N_GATE:]
    w_gc = jnp.pad(w_gt, ((0, 0), (0, LANES - N_GATE)))
    lam_p = jnp.stack([lq1, lk1, lq2, lk2])
    sub_g = subln_g.reshape(1, HEAD_W)
    mh_g2 = mh_g.reshape(1, VM_W)
    bcol = jnp.pad(jnp.concatenate([b_ig, b_fg]), (0, LANES - N_GATE)).reshape(1, LANES)
    brow = jnp.concatenate([b_ig, b_fg]).reshape(N_GATE, 1)
    g1, b1 = ln1_g.reshape(1, d), ln1_b.reshape(1, d)
    g2, b2 = ln2_g.reshape(1, d), ln2_b.reshape(1, d)
    wrt = w_router.T
    brt = b_router.reshape(ne, 1)

    x2 = x_p.reshape(t, d)
    tm = _pick_tile(seq, PROJ_TM)
    tabs_p = _rope_tables(jnp.arange(seq, dtype=jnp.int32))
    (qa, k_f, k_b, v_f, v_b, qm, km, vm, gcol, grow, og, ga, gm) = _proj_prompt(
        x2, w_a.astype(BF16), w_gc.astype(BF16), w_gt.T.astype(BF16), w_r.astype(BF16), tabs_p, seq, tm)
    a_in, (wg_b, wu_b, wd_b) = _attn_prompt(qa, k_b, v_b, lam_p, sub_g, bp, seq, lam_init,
                                            (w_gate, w_up, w_down))
    m_in, cn_p, m_p = _mlstm_prompt(qm, km, vm, gcol, grow, bcol, brow, og, mh_g2, bp, seq,
                                    _pick_tile(seq, MLSTM_CHUNK))
    sgroup = MOE_GROUP * MOE_NSUB
    t_pad = -(-(t + ns) // sgroup) * sgroup
    h_p, hb_p, comb_p, mask_p = _mix(
        a_in, m_in, ga, gm, x2, w_ba.astype(BF16), w_bm.astype(BF16), w_o.astype(BF16), g1, b1, wrt, brt,
        alpha, False, _pick_tile(seq, MIX_TM), hb_rows=t_pad)

    xs2 = x_s.reshape(ns, d)
    chunk = 7 * LANES
    width = -(-w_in.shape[1] // chunk) * chunk
    z_s = _proj_sample(xs2, jnp.pad(w_in, ((0, 0), (0, width - w_in.shape[1]))), chunk)
    tabs_s = _rope_tables(jnp.full((1,), past, jnp.int32))
    ppb = _pick_tile(page_table.shape[1], 32)
    n_pool = cache_k.shape[1]
    k_s, v_s, a_s = _paged_attn(page_table + l * n_pool, z_s.reshape(ns, 1, width), tabs_s, lam_p, sub_g,
                                cache_k.reshape(-1, PAGE_SIZE * H_A, HEAD_W),
                                cache_v.reshape(-1, PAGE_SIZE * H_A, HEAD_W), ppb, lam_init)
    o = 3 * QA_W
    qc = z_s[:, o:o + QM_W].reshape(ns, H_M, DK_M, 1)
    kc = z_s[:, o + QM_W:o + 2 * QM_W].reshape(ns, H_M, DK_M, 1)
    v3 = z_s[:, o + 2 * QM_W:W_A].reshape(ns, 1, VM_W)
    gates = z_s[:, W_A:W_A + N_GATE]
    r0 = W_A + N_GATE
    og3 = z_s[:, r0:r0 + VM_W].reshape(ns, 1, VM_W)
    ga_s = z_s[:, r0 + VM_W:r0 + VM_W + d]
    gm_s = z_s[:, r0 + VM_W + d:r0 + VM_W + 2 * d]
    m_in_s, c_s, n_s, m_s = _mlstm_step(gates, m0, b_ig, b_fg, qc, kc, v3, og3, c0,
                                        n0.reshape(ns, H_M, DK_M, 1), mh_g2)
    h_s, hb_s, comb_s, mask_s = _mix(
        a_s.reshape(ns, QA_W), m_in_s.reshape(ns, VM_W), ga_s, gm_s, xs2, w_ba, w_bm, w_o, g1, b1,
        wrt, brt, alpha, True, ns)

    padt = lambda p, s: jnp.pad(jnp.concatenate([p, s], axis=1), ((0, 0), (0, t_pad - t - ns)))
    hb_all = _append_rows(hb_p, hb_s, t)
    comb_all = padt(comb_p, comb_s)
    pos_all, cnt = _route(padt(mask_p, mask_s), MOE_GROUP)
    counts = cnt[:, :, 0].astype(jnp.int32)
    used = (jnp.sum(counts, axis=1) > 0).reshape(-1, MOE_NSUB)
    nact = jnp.max(jnp.where(used, jnp.arange(1, MOE_NSUB + 1, dtype=jnp.int32), 0), axis=1)
    ff = _moe(counts.reshape(-1), nact, hb_all, pos_all, comb_all,
              wg_b, b_gate.reshape(ne, 1, -1), wu_b, b_up.reshape(ne, 1, -1),
              wd_b, b_down.reshape(ne, 1, -1), MOE_GROUP, MOE_NSUB, MOE_ROWS)
    y_p = _ln2(h_p, ff, g2, b2, alpha, tm, 0)
    y_s = _ln2(h_s, ff, g2, b2, alpha, ns, t // ns)

    cw = cn_p[:, :, :DV_M].reshape(bp, H_M, DK_M, DV_M)
    nw = cn_p[:, :, DV_M].reshape(bp, H_M, DK_M)
    outs_p = (y_p.reshape(bp, seq, d), k_f.reshape(bp, seq, H_A, HEAD_W), v_f.reshape(bp, seq, H_A, HEAD_W),
              cw, nw, m_p[:, 0, :H_M])
    outs_s = (y_s.reshape(ns, 1, d), k_s.reshape(ns, 1, H_A, HEAD_W), v_s.reshape(ns, 1, H_A, HEAD_W),
              c_s, n_s.reshape(ns, H_M, DK_M), m_s[:, :, 0])
    return outs_p, outs_s


def kernel(x_prompt, x_sample, cache_k, cache_v, state_c, state_n, state_m, page_table, w_in, b_igate, b_fgate, lambda_q1, lambda_k1, lambda_q2, lambda_k2, subln_g, mh_norm_g, w_ba, w_bm, w_o, ln1_g, ln1_b, w_router, b_router, w_gate, b_gate, w_up, b_up, w_down, b_down, ln2_g, ln2_b):
    depth = w_in.shape[0]
    assert x_sample.shape[1] == 1, "the sample pass handles one new token per sequence"
    weights = (w_in, b_igate, b_fgate, lambda_q1, lambda_k1, lambda_q2, lambda_k2, subln_g, mh_norm_g,
               w_ba, w_bm, w_o, ln1_g, ln1_b, w_router, b_router, w_gate, b_gate, w_up, b_up,
               w_down, b_down, ln2_g, ln2_b)
    y_p, y_s = x_prompt, x_sample
    acc_p = [[] for _ in range(5)]
    acc_s = [[] for _ in range(5)]
    for l in range(depth):
        outs_p, outs_s = _layer(l, depth, y_p, y_s, cache_k, cache_v, state_c[l], state_n[l],
                                state_m[l], page_table, tuple(a[l] for a in weights))
        y_p, y_s = outs_p[0], outs_s[0]
        for i in range(5):
            acc_p[i].append(outs_p[1 + i])
            acc_s[i].append(outs_s[1 + i])
    return (y_p, y_s, *(jnp.stack(a) for a in acc_p), *(jnp.stack(a) for a in acc_s))
```
